```python
import math
import jax
import jax.numpy as jnp
from jax import lax
import numpy as np

D_MODEL = 1024
BATCH = 8
SEQ = 2048
DEPTH = 4
DEC_BATCH = 128
DEC_SEQ = 4
PAST_LEN = 2048
PAGE_SIZE = 128

HD = 64
A_WIDTH = D_MODEL // 2
A_HEADS = A_WIDTH // (2 * HD)
S_WIDTH = D_MODEL // 4
S_CH = 16
S_GROUPS = S_WIDTH // S_CH
S_STATE = 64
C_WIDTH = D_MODEL - A_WIDTH - S_WIDTH
C_HEADS = C_WIDTH // HD
C_KV = 1
C_GROUP = C_HEADS // C_KV
MIX_WIDTH = A_WIDTH + S_WIDTH + C_WIDTH
CMP_BLOCK = 32
SEL_BLOCK = 64
TOP_N = 8
WINDOW = 512
Q_BLOCK = 128
ROPE_THETA = 10000.0
RMS_EPS = 1e-6
DT_MIN = 1e-3
DT_MAX = 1e-1
MASK_VALUE = -1e30
FORCE_SCORE = 1e4
IN_SIZES = (A_WIDTH, A_WIDTH, A_WIDTH, A_WIDTH,
            S_WIDTH, S_WIDTH,
            C_WIDTH, 6 * C_KV * HD, 3 * C_HEADS, C_WIDTH)
IN_WIDTH = 4 * A_WIDTH + 2 * S_WIDTH + 2 * C_WIDTH + 6 * C_KV * HD + 3 * C_HEADS

kernel_name = 'hybrid_diffattn_s5_nsa_decode_step'


def rms_norm(x, g):
    xf = x.astype(jnp.float32)
    y = xf * lax.rsqrt(jnp.mean(xf * xf, axis=-1, keepdims=True) + RMS_EPS)
    return (y * g.astype(jnp.float32)).astype(x.dtype)


def rope(x, pos):
    half = x.shape[-1] // 2
    inv = ROPE_THETA ** (-jnp.arange(half, dtype=jnp.float32) / half)
    ang = pos.astype(jnp.float32)[:, None] * inv[None, :]
    shp = (1, pos.shape[0]) + (1,) * (x.ndim - 3) + (half,)
    cos = jnp.cos(ang).reshape(shp)
    sin = jnp.sin(ang).reshape(shp)
    xf = x.astype(jnp.float32)
    x1, x2 = xf[..., :half], xf[..., half:]
    return jnp.concatenate([x1 * cos - x2 * sin, x2 * cos + x1 * sin], axis=-1).astype(x.dtype)


def masked_softmax(s, mask):
    s = jnp.where(mask, s.astype(jnp.float32), MASK_VALUE)
    m = jnp.max(s, axis=-1, keepdims=True)
    e = jnp.exp(s - m) * mask
    return e / jnp.maximum(jnp.sum(e, axis=-1, keepdims=True), 1e-30)


def split_in(p):
    out, off = [], 0
    for n in IN_SIZES:
        out.append(p[..., off:off + n])
        off += n
    return out


def over_query_blocks(fn, pos, *qs):
    T = pos.shape[0]
    nb = T // Q_BLOCK
    pb = pos.reshape(nb, Q_BLOCK)
    qb = tuple(q.reshape((q.shape[0], nb, Q_BLOCK) + q.shape[2:]).swapaxes(0, 1) for q in qs)
    out = lax.map(lambda a: fn(a[0], *a[1:]), (pb,) + qb)
    out = out.swapaxes(0, 1)
    return out.reshape((out.shape[0], T) + out.shape[3:])


def gather_pages(pool, layer, page_table):
    rows = pool[layer, page_table]
    return rows.reshape((rows.shape[0], rows.shape[1] * rows.shape[2]) + rows.shape[3:])


def lambda_value(lam_p, layer):
    lam_init = 0.8 - 0.6 * math.exp(-0.3 * layer)
    lp = lam_p.astype(jnp.float32)
    lam = jnp.exp(jnp.sum(lp[0] * lp[1])) - jnp.exp(jnp.sum(lp[2] * lp[3])) + lam_init
    return lam, lam_init


def diff_attend(q, q_pos, k, v, lam, lam_init, norm_g):
    L = k.shape[1]
    s = jnp.einsum('bqhcd,bkhcd->bchqk', q, k) * (HD ** -0.5)
    mask = jnp.arange(L)[None, :] <= q_pos[:, None]
    p = masked_softmax(s, mask)
    w = p[:, 0] - lam * p[:, 1]
    o = jnp.einsum('bhqk,bkhe->bqhe', w.astype(v.dtype), v)
    return rms_norm(o, norm_g) * (1.0 - lam_init)


def _ssm_combine(e1, e2):
    a1r, a1i, b1r, b1i = e1
    a2r, a2i, b2r, b2i = e2
    return (a1r * a2r - a1i * a2i, a1r * a2i + a1i * a2r,
            a2r * b1r - a2i * b1i + b2r, a2r * b1i + a2i * b1r + b2i)


def s5_scan(u, h_re0, h_im0, lp):
    f32 = jnp.float32
    a_re = lp['ssm_A_re'].astype(f32)
    a_im = lp['ssm_A_im'].astype(f32)
    dt = jnp.exp(lp['ssm_log_dt'].astype(f32))[:, None]
    mag = jnp.exp(dt * a_re)
    ab_re = mag * jnp.cos(dt * a_im)
    ab_im = mag * jnp.sin(dt * a_im)
    den = a_re * a_re + a_im * a_im
    num_re = ab_re - 1.0
    c_re = (num_re * a_re + ab_im * a_im) / den
    c_im = (ab_im * a_re - num_re * a_im) / den
    uf = u.astype(f32)
    bu_r = jnp.einsum('btgc,gpc->btgp', uf, lp['ssm_B_re'].astype(f32))
    bu_i = jnp.einsum('btgc,gpc->btgp', uf, lp['ssm_B_im'].astype(f32))
    b_re = c_re * bu_r - c_im * bu_i
    b_im = c_re * bu_i + c_im * bu_r
    h_re0 = h_re0.astype(f32)
    h_im0 = h_im0.astype(f32)
    b_re = b_re.at[:, 0].add(ab_re * h_re0 - ab_im * h_im0)
    b_im = b_im.at[:, 0].add(ab_re * h_im0 + ab_im * h_re0)
    shape = b_re.shape
    elems = (jnp.broadcast_to(ab_re, shape), jnp.broadcast_to(ab_im, shape), b_re, b_im)
    _, _, h_re, h_im = lax.associative_scan(_ssm_combine, elems, axis=1)
    y = (jnp.einsum('btgp,gcp->btgc', h_re, lp['ssm_C_re'].astype(f32))
         - jnp.einsum('btgp,gcp->btgc', h_im, lp['ssm_C_im'].astype(f32))
         + lp['ssm_D'].astype(f32) * uf)
    B, T = u.shape[:2]
    y = jax.nn.gelu(y.reshape(B, T, S_WIDTH))
    y = y * jax.nn.sigmoid(y @ lp['ssm_w_glu'].astype(f32) + lp['ssm_b_glu'].astype(f32))
    return y.astype(u.dtype), h_re[:, -1], h_im[:, -1]


def nsa_rows(kv, pos):
    k = rope(kv[:, :, 0::2], pos)
    v = kv[:, :, 1::2]
    rows = jnp.stack([k, v], axis=3).reshape(kv.shape)
    return rows[:, :, :4], rows[:, :, 4:]


def nsa_prepare(kv4, cmp_w):
    B, L = kv4.shape[:2]
    nc = L // CMP_BLOCK
    cb = kv4[:, :nc * CMP_BLOCK, 0:2].reshape(B, nc, CMP_BLOCK, 2, C_KV, HD)
    cmp = jnp.einsum('bnjchd,cj->bnchd', cb, cmp_w)
    ns = -(-L // SEL_BLOCK)
    sel = jnp.pad(kv4[:, :, 2:4], ((0, 0), (0, ns * SEL_BLOCK - L), (0, 0), (0, 0), (0, 0)))
    sel = sel.reshape(B, ns, SEL_BLOCK, 2, C_KV, HD).transpose(3, 0, 4, 1, 2, 5)
    return cmp[:, :, 0], cmp[:, :, 1], sel[0], sel[1]


def nsa_attend(q, q_pos, kc, vc, ks, vs, kw, kw_pos, gates):
    B, Tq = q.shape[:2]
    scale = HD ** -0.5
    qg = q.reshape(B, Tq, C_KV, C_GROUP, HD)
    t = q_pos[:, None]
    nc = kc.shape[1]
    cmask = ((jnp.arange(nc) + 1) * CMP_BLOCK - 1)[None, :] <= t
    p_cmp = masked_softmax(jnp.einsum('bqhgd,bnhd->bqhgn', qg, kc) * scale, cmask[None, :, None, None, :])
    o_cmp = jnp.einsum('bqhgn,bnhd->bqhgd', p_cmp.astype(vc.dtype), vc)
    ns = ks.shape[2]
    ratio = SEL_BLOCK // CMP_BLOCK
    imp = jnp.pad(p_cmp.sum(3), ((0, 0), (0, 0), (0, 0), (0, ns * ratio - nc)))
    imp = imp.reshape(B, Tq, C_KV, ns, ratio).sum(-1)
    blk = jnp.arange(ns)[None, :]
    cur = t // SEL_BLOCK
    valid = blk <= cur
    forced = (blk == 0) | (blk == cur) | (blk == cur - 1)
    score = jnp.where(valid[None, :, None, :], imp + FORCE_SCORE * forced[None, :, None, :], -1.0)
    _, idx = lax.top_k(score, min(TOP_N, ns))
    n = idx.shape[-1]
    bi = jnp.arange(B)[:, None, None, None]
    hi = jnp.arange(C_KV)[None, None, :, None]
    ksel = ks[bi, hi, idx].reshape(B, Tq, C_KV, n * SEL_BLOCK, HD)
    vsel = vs[bi, hi, idx].reshape(B, Tq, C_KV, n * SEL_BLOCK, HD)
    kpos = (idx[..., None] * SEL_BLOCK + jnp.arange(SEL_BLOCK)).reshape(B, Tq, C_KV, n * SEL_BLOCK)
    smask = (kpos <= q_pos[None, :, None, None])[:, :, :, None, :]
    p_sel = masked_softmax(jnp.einsum('bqhgd,bqhmd->bqhgm', qg, ksel) * scale, smask)
    o_sel = jnp.einsum('bqhgm,bqhmd->bqhgd', p_sel.astype(vsel.dtype), vsel)
    kp = kw_pos[None, :]
    wmask = (kp <= t) & (kp > t - WINDOW) & (kp >= 0)
    p_win = masked_softmax(jnp.einsum('bqhgd,bkhd->bqhgk', qg, kw[:, :, 0]) * scale, wmask[None, :, None, None, :])
    o_win = jnp.einsum('bqhgk,bkhd->bqhgd', p_win.astype(kw.dtype), kw[:, :, 1])
    g = gates.reshape(B, Tq, C_KV, C_GROUP, 3)
    o = g[..., 0:1] * o_cmp + g[..., 1:2] * o_sel + g[..., 2:3] * o_win
    return o.reshape(B, Tq, C_WIDTH)


def branch_inputs(x, pos, lp):
    B, T = x.shape[:2]
    h = rms_norm(x, lp['norm_g'])
    qa, ka, va, za, us, zs, qc, kvc, gc, zc = split_in(h @ lp['w_in'])
    qa = rope(qa.reshape(B, T, A_HEADS, 2, HD), pos)
    ka = rope(ka.reshape(B, T, A_HEADS, 2, HD), pos)
    diff_rows = jnp.stack([ka.reshape(B, T, A_HEADS, 2 * HD), va.reshape(B, T, A_HEADS, 2 * HD)], axis=2)
    qc = rope(qc.reshape(B, T, C_HEADS, HD), pos)
    nsa_paged, win_rows = nsa_rows(kvc.reshape(B, T, 6, C_KV, HD), pos)
    gates = jax.nn.sigmoid(gc).reshape(B, T, C_HEADS, 3)
    u = us.reshape(B, T, S_GROUPS, S_CH)
    return qa, diff_rows, u, qc, nsa_paged, win_rows, gates, (za, zs, zc)


def merge(x, oa, os_, oc, zs3, lp):
    za, zs, zc = zs3
    B, T = x.shape[:2]
    mix = jnp.concatenate([oa.reshape(B, T, A_WIDTH) * jax.nn.silu(za),
                           os_ * jax.nn.silu(zs),
                           oc * jax.nn.silu(zc)], axis=-1)
    return x + mix @ lp['w_out']


def prompt_layer(x, lp, layer):
    B, T = x.shape[:2]
    pos = jnp.arange(T, dtype=jnp.int32)
    qa, diff_rows, u, qc, nsa_paged, win_rows, gates, zs3 = branch_inputs(x, pos, lp)
    lam, lam_init = lambda_value(lp['diff_lambda'], layer)
    ka = diff_rows[:, :, 0].reshape(B, T, A_HEADS, 2, HD)
    va = diff_rows[:, :, 1]
    oa = over_query_blocks(lambda p, q: diff_attend(q, p, ka, va, lam, lam_init, lp['diff_norm_g']), pos, qa)
    zero = jnp.zeros((B, S_GROUPS, S_STATE), jnp.float32)
    os_, h_re, h_im = s5_scan(u, zero, zero, lp)
    kc, vc, ks, vs = nsa_prepare(nsa_paged, lp['nsa_cmp_w'])
    kw_pad = jnp.pad(win_rows, ((0, 0), (WINDOW, 0), (0, 0), (0, 0), (0, 0)))

    def nsa_block(p, q, g):
        start = p[0]
        kw = lax.dynamic_slice_in_dim(kw_pad, start, WINDOW + Q_BLOCK, axis=1)
        kw_pos = start - WINDOW + jnp.arange(WINDOW + Q_BLOCK, dtype=jnp.int32)
        return nsa_attend(q, p, kc, vc, ks, vs, kw, kw_pos, g)

    oc = over_query_blocks(nsa_block, pos, qc, gates)
    y = merge(x, oa, os_, oc, zs3, lp)
    w_keep = min(WINDOW, T)
    return y, diff_rows, nsa_paged, win_rows[:, T - w_keep:], h_re, h_im


def sample_layer(x, lp, layer, past_diff, past_nsa, win_buf, h_re0, h_im0):
    B, T = x.shape[:2]
    past = past_diff.shape[1]
    pos = past + jnp.arange(T, dtype=jnp.int32)
    qa, diff_rows, u, qc, nsa_paged, win_rows, gates, zs3 = branch_inputs(x, pos, lp)
    lam, lam_init = lambda_value(lp['diff_lambda'], layer)
    full = jnp.concatenate([past_diff, diff_rows], axis=1)
    L = past + T
    oa = diff_attend(qa, pos, full[:, :, 0].reshape(B, L, A_HEADS, 2, HD), full[:, :, 1],
                     lam, lam_init, lp['diff_norm_g'])
    os_, h_re, h_im = s5_scan(u, h_re0, h_im0, lp)
    kc, vc, ks, vs = nsa_prepare(jnp.concatenate([past_nsa, nsa_paged], axis=1), lp['nsa_cmp_w'])
    w_buf = win_buf.shape[1]
    kw = jnp.concatenate([win_buf, win_rows], axis=1)
    kw_pos = past - w_buf + jnp.arange(w_buf + T, dtype=jnp.int32)
    oc = nsa_attend(qc, pos, kc, vc, ks, vs, kw, kw_pos, gates)
    y = merge(x, oa, os_, oc, zs3, lp)
    return y, diff_rows, nsa_paged, kw[:, T:], h_re, h_im


def setup_inputs(seed: int = 0) -> dict:
    key = jax.random.key(seed)
    ks = jax.random.split(key, 32)
    f32 = jnp.float32
    n_pages = PAST_LEN // PAGE_SIZE
    n_used = DEC_BATCH * n_pages
    n_pool = (5 * n_used + 3) // 4
    perm = jax.random.permutation(ks[0], n_pool)
    page_table = perm[:n_used].reshape(DEC_BATCH, n_pages).astype(jnp.int32)
    w_buf = min(WINDOW, PAST_LEN)

    def nrm(k, shape, s):
        return s * jax.random.normal(k, shape, f32)

    n_idx = jnp.arange(S_STATE, dtype=f32)
    return {
        'x_prompt': nrm(ks[1], (BATCH, SEQ, D_MODEL), 1.0),
        'x_sample': nrm(ks[2], (DEC_BATCH, DEC_SEQ, D_MODEL), 1.0),
        'cache_diff_kv': nrm(ks[3], (DEPTH, n_pool, PAGE_SIZE, 2, A_HEADS, 2 * HD), 1.0),
        'cache_nsa_kv': nrm(ks[4], (DEPTH, n_pool, PAGE_SIZE, 4, C_KV, HD), 1.0),
        'cache_win_kv': nrm(ks[5], (DEPTH, DEC_BATCH, w_buf, 2, C_KV, HD), 1.0),
        'state_ssm_re': nrm(ks[6], (DEPTH, DEC_BATCH, S_GROUPS, S_STATE), 0.2),
        'state_ssm_im': nrm(ks[7], (DEPTH, DEC_BATCH, S_GROUPS, S_STATE), 0.2),
        'page_table': page_table,
        'norm_g': 1.0 + nrm(ks[8], (DEPTH, D_MODEL), 0.02),
        'w_in': nrm(ks[9], (DEPTH, D_MODEL, IN_WIDTH), D_MODEL ** -0.5),
        'w_out': nrm(ks[10], (DEPTH, MIX_WIDTH, D_MODEL), MIX_WIDTH ** -0.5),
        'diff_lambda': nrm(ks[11], (DEPTH, 4, HD), 0.1),
        'diff_norm_g': 1.0 + nrm(ks[12], (DEPTH, 2 * HD), 0.02),
        'ssm_A_re': -0.5 + nrm(ks[13], (DEPTH, S_GROUPS, S_STATE), 0.01),
        'ssm_A_im': math.pi * n_idx + nrm(ks[14], (DEPTH, S_GROUPS, S_STATE), 0.01),
        'ssm_log_dt': jax.random.uniform(ks[15], (DEPTH, S_GROUPS), f32, math.log(DT_MIN), math.log(DT_MAX)),
        'ssm_B_re': nrm(ks[16], (DEPTH, S_GROUPS, S_STATE, S_CH), (2 * S_CH) ** -0.5),
        'ssm_B_im': nrm(ks[17], (DEPTH, S_GROUPS, S_STATE, S_CH), (2 * S_CH) ** -0.5),
        'ssm_C_re': nrm(ks[18], (DEPTH, S_GROUPS, S_CH, S_STATE), S_STATE ** -0.5),
        'ssm_C_im': nrm(ks[19], (DEPTH, S_GROUPS, S_CH, S_STATE), S_STATE ** -0.5),
        'ssm_D': nrm(ks[20], (DEPTH, S_GROUPS, S_CH), 1.0),
        'ssm_w_glu': nrm(ks[21], (DEPTH, S_WIDTH, S_WIDTH), S_WIDTH ** -0.5),
        'ssm_b_glu': nrm(ks[22], (DEPTH, S_WIDTH), 0.01),
        'nsa_cmp_w': 1.0 / CMP_BLOCK + nrm(ks[23], (DEPTH, 2, CMP_BLOCK), 0.01),
        'final_norm_g': 1.0 + nrm(ks[24], (D_MODEL,), 0.02),
    }


def reference(x_prompt, x_sample, cache_diff_kv, cache_nsa_kv, cache_win_kv, state_ssm_re, state_ssm_im,
              page_table, norm_g, w_in, w_out, diff_lambda, diff_norm_g, ssm_A_re, ssm_A_im, ssm_log_dt,
              ssm_B_re, ssm_B_im, ssm_C_re, ssm_C_im, ssm_D, ssm_w_glu, ssm_b_glu, nsa_cmp_w, final_norm_g):
    xp, xs = x_prompt, x_sample
    dkp, dks, nkp, nks, wkp, wks, hrp, hip, hrs, his = [], [], [], [], [], [], [], [], [], []
    for layer in range(DEPTH):
        lp = {'norm_g': norm_g[layer], 'w_in': w_in[layer], 'w_out': w_out[layer],
              'diff_lambda': diff_lambda[layer], 'diff_norm_g': diff_norm_g[layer],
              'ssm_A_re': ssm_A_re[layer], 'ssm_A_im': ssm_A_im[layer], 'ssm_log_dt': ssm_log_dt[layer],
              'ssm_B_re': ssm_B_re[layer], 'ssm_B_im': ssm_B_im[layer],
              'ssm_C_re': ssm_C_re[layer], 'ssm_C_im': ssm_C_im[layer], 'ssm_D': ssm_D[layer],
              'ssm_w_glu': ssm_w_glu[layer], 'ssm_b_glu': ssm_b_glu[layer], 'nsa_cmp_w': nsa_cmp_w[layer]}
        xp, r_d, r_n, r_w, h_r, h_i = prompt_layer(xp, lp, layer)
        dkp.append(r_d)
        nkp.append(r_n)
        wkp.append(r_w)
        hrp.append(h_r)
        hip.append(h_i)
        past_diff = gather_pages(cache_diff_kv, layer, page_table)
        past_nsa = gather_pages(cache_nsa_kv, layer, page_table)
        xs, r_d, r_n, r_w, h_r, h_i = sample_layer(xs, lp, layer, past_diff, past_nsa, cache_win_kv[layer],
                                                   state_ssm_re[layer], state_ssm_im[layer])
        dks.append(r_d)
        nks.append(r_n)
        wks.append(r_w)
        hrs.append(h_r)
        his.append(h_i)
    y_prompt = rms_norm(xp, final_norm_g)
    y_sample = rms_norm(xs, final_norm_g)
    return (y_prompt, y_sample, jnp.stack(dkp), jnp.stack(dks), jnp.stack(nkp), jnp.stack(nks),
            jnp.stack(wkp), jnp.stack(wks), jnp.stack(hrp), jnp.stack(hip), jnp.stack(hrs), jnp.stack(his))
```

```python
import functools
import math

import jax
import jax.numpy as jnp
import numpy as np
from jax import lax
from jax.experimental import pallas as pl
from jax.experimental.pallas import tpu as pltpu

F32 = jnp.float32
BF16 = jnp.bfloat16

D_MODEL = 1024
PAGE_SIZE = 128
HD = 64
A_WIDTH = D_MODEL // 2
A_HEADS = A_WIDTH // (2 * HD)
S_WIDTH = D_MODEL // 4
S_CH = 16
S_GROUPS = S_WIDTH // S_CH
S_STATE = 64
C_WIDTH = D_MODEL - A_WIDTH - S_WIDTH
C_HEADS = C_WIDTH // HD
CMP_BLOCK = 32
SEL_BLOCK = 64
TOP_N = 8
WINDOW = 512
ROPE_THETA = 10000.0
RMS_EPS = 1e-6
MASK_VALUE = -1e30
FORCE_SCORE = 1e4
SCALE = HD ** -0.5
GP = S_GROUPS * S_STATE

LANES = 128
SUBLANES = 8
VMEM_LIMIT_BYTES = 56 * 1024 * 1024

COL_QA = 0
COL_KA = COL_QA + A_WIDTH
COL_VA = COL_KA + A_WIDTH
COL_ZA = COL_VA + A_WIDTH
COL_US = COL_ZA + A_WIDTH
COL_ZS = COL_US + S_WIDTH
COL_QC = COL_ZS + S_WIDTH
COL_KVC = COL_QC + C_WIDTH
COL_ZC = COL_KVC + 6 * HD
COL_GC = COL_ZC + C_WIDTH
IN_PAD = COL_GC + LANES
N_GATES = 3 * C_HEADS

Q_TILE = 128
SEL_TILE = 256
WIN_TILE = 128
HIGHEST = lax.Precision.HIGHEST


def _params(semantics):
    return pltpu.CompilerParams(dimension_semantics=semantics, vmem_limit_bytes=VMEM_LIMIT_BYTES)


def _dot_nt(a, b, precision=None):
    return lax.dot_general(a, b, (((1,), (1,)), ((), ())), preferred_element_type=F32, precision=precision)


def _dot(a, b, precision=None):
    return jnp.dot(a, b, preferred_element_type=F32, precision=precision)


def _rep_lanes(x, width):
    if width == LANES:
        return x
    if width < LANES:
        return x[:, :width]
    return jnp.concatenate([x] * (width // LANES), axis=1)


def _inproj_kernel(x_ref, g_ref, w_ref, tab_ref,
                   qa_ref, dr_ref, kvb_ref, za_ref, us_ref, zs_ref, qc_ref, nsa_ref, win_ref, zc_ref, gate_ref):
    x = x_ref[...]
    ms = jnp.mean(x * x, axis=-1, keepdims=True)
    xn = (x * lax.rsqrt(ms + RMS_EPS) * g_ref[...]).astype(BF16)
    tm = x.shape[0]
    cos = tab_ref[:, 0:LANES]
    sin = tab_ref[:, LANES:2 * LANES]
    cos_kv = tab_ref[:, 2 * LANES:3 * LANES]
    sin_kv = tab_ref[:, 3 * LANES:4 * LANES]
    lane = lax.broadcasted_iota(jnp.int32, (tm, LANES), 1)
    first_half = (lane % HD) < (HD // 2)

    def rope(v, c, s):
        partner = jnp.where(first_half, pltpu.roll(v, LANES - HD // 2, 1), pltpu.roll(v, HD // 2, 1))
        return v * c + partner * s

    def proj(c0, width):
        return _dot(xn, w_ref[:, c0:c0 + width])

    def chunks(y):
        return [y[:, j * LANES:(j + 1) * LANES] for j in range(y.shape[1] // LANES)]

    y = proj(COL_QA, A_WIDTH)
    for j, c in enumerate(chunks(y)):
        qa_ref[:, j * LANES:(j + 1) * LANES] = (rope(c, cos, sin) * SCALE).astype(BF16)
    y = proj(COL_KA, A_WIDTH)
    for j, c in enumerate(chunks(y)):
        r = rope(c, cos, sin)
        dr_ref[:, j * LANES:(j + 1) * LANES] = r
        kvb_ref[:, j * LANES:(j + 1) * LANES] = r.astype(BF16)
    y = proj(COL_VA, A_WIDTH)
    dr_ref[:, A_WIDTH:2 * A_WIDTH] = y
    kvb_ref[:, A_WIDTH:2 * A_WIDTH] = y.astype(BF16)
    za_ref[...] = proj(COL_ZA, A_WIDTH)
    y = proj(COL_US, 2 * S_WIDTH)
    us_ref[...] = y[:, :S_WIDTH]
    zs_ref[...] = y[:, S_WIDTH:]
    y = proj(COL_QC, C_WIDTH)
    for j, c in enumerate(chunks(y)):
        qc_ref[:, j * LANES:(j + 1) * LANES] = rope(c, cos, sin) * SCALE
    y = proj(COL_KVC, 6 * HD)
    kv = [rope(c, cos_kv, sin_kv) for c in chunks(y)]
    nsa_ref[:, 0:LANES] = kv[0]
    nsa_ref[:, LANES:2 * LANES] = kv[1]
    win_ref[...] = kv[2]
    y = proj(COL_ZC, C_WIDTH + LANES)
    zc_ref[...] = y[:, :C_WIDTH]
    gate_ref[...] = jax.nn.sigmoid(y[:, C_WIDTH:])


def _inproj(x, norm_g, w, tab, seq_len, tm, us_time_major):
    n = x.shape[0]
    nt = tab.shape[0] // tm
    grid = (n // tm,)
    row = lambda i: (i, 0)
    nb = n // seq_len
    if us_time_major:
        us_shape = jax.ShapeDtypeStruct((seq_len, nb * S_WIDTH), F32)
        us_spec = pl.BlockSpec((tm, S_WIDTH), lambda i: (i % nt, i // nt))
    else:
        us_shape = jax.ShapeDtypeStruct((n, S_WIDTH), F32)
        us_spec = pl.BlockSpec((tm, S_WIDTH), row)
    out_shape = (
        jax.ShapeDtypeStruct((n, A_WIDTH), BF16),
        jax.ShapeDtypeStruct((n, 2 * A_WIDTH), F32),
        jax.ShapeDtypeStruct((n, 2 * A_WIDTH), BF16),
        jax.ShapeDtypeStruct((n, A_WIDTH), F32),
        us_shape,
        jax.ShapeDtypeStruct((n, S_WIDTH), F32),
        jax.ShapeDtypeStruct((n, C_WIDTH), F32),
        jax.ShapeDtypeStruct((n, 4 * HD), F32),
        jax.ShapeDtypeStruct((n, 2 * HD), F32),
        jax.ShapeDtypeStruct((n, C_WIDTH), F32),
        jax.ShapeDtypeStruct((n, LANES), F32),
    )
    out_specs = (
        pl.BlockSpec((tm, A_WIDTH), row), pl.BlockSpec((tm, 2 * A_WIDTH), row), pl.BlockSpec((tm, 2 * A_WIDTH), row),
        pl.BlockSpec((tm, A_WIDTH), row), us_spec, pl.BlockSpec((tm, S_WIDTH), row),
        pl.BlockSpec((tm, C_WIDTH), row), pl.BlockSpec((tm, 4 * HD), row), pl.BlockSpec((tm, 2 * HD), row),
        pl.BlockSpec((tm, C_WIDTH), row), pl.BlockSpec((tm, LANES), row),
    )
    return pl.pallas_call(
        _inproj_kernel,
        grid=grid,
        in_specs=[pl.BlockSpec((tm, D_MODEL), row),
                  pl.BlockSpec((1, D_MODEL), lambda i: (0, 0)),
                  pl.BlockSpec((D_MODEL, IN_PAD), lambda i: (0, 0)),
                  pl.BlockSpec((tm, 4 * LANES), lambda i: (i % nt, 0))],
        out_specs=out_specs,
        out_shape=out_shape,
        compiler_params=_params(("parallel",)),
        name="inproj",
    )(x, norm_g, w, tab)


def _lambda(dl_ref, lam_init):
    dl = dl_ref[...]
    a = jnp.sum(dl[0:1] * dl[1:2], axis=-1, keepdims=True)
    b = jnp.sum(dl[2:3] * dl[3:4], axis=-1, keepdims=True)
    return jnp.exp(a) - jnp.exp(b) + lam_init


def _split_components(q):
    lane = lax.broadcasted_iota(jnp.int32, q.shape, 1)
    zero = jnp.zeros_like(q)
    return jnp.concatenate([jnp.where(lane < HD, q, zero), jnp.where(lane >= HD, q, zero)], axis=0)


def _head_norm(o, g, lam_init):
    ms = jnp.mean(o * o, axis=-1, keepdims=True)
    return o * lax.rsqrt(ms + RMS_EPS) * g * (1.0 - lam_init)


def _diff_prompt_kernel(dl_ref, g_ref, q_ref, k_ref, v_ref, o_ref, m_sc, l_sc, acc_sc, *, tq, lam_init):
    qi = pl.program_id(2)
    q2 = _split_components(q_ref[...])
    m_sc[...] = jnp.full(m_sc.shape, MASK_VALUE, F32)
    l_sc[...] = jnp.zeros(l_sc.shape, F32)
    acc_sc[...] = jnp.zeros(acc_sc.shape, F32)

    def step(kt, diagonal):
        off = pl.multiple_of(kt * tq, tq)
        k = k_ref[pl.ds(off, tq), :]
        v = v_ref[pl.ds(off, tq), :]
        s = _dot_nt(q2, k)
        if diagonal:
            r = lax.broadcasted_iota(jnp.int32, s.shape, 0)
            c = lax.broadcasted_iota(jnp.int32, s.shape, 1)
            r = jnp.where(r >= tq, r - tq, r)
            s = jnp.where(c <= r, s, MASK_VALUE)
        m_prev = m_sc[...]
        m_new = jnp.maximum(m_prev, jnp.max(s, axis=1, keepdims=True))
        alpha = jnp.exp(m_prev - m_new)
        p = jnp.exp(s - _rep_lanes(m_new, tq))
        l_sc[...] = alpha * l_sc[...] + jnp.sum(p, axis=1, keepdims=True)
        acc_sc[...] = alpha * acc_sc[...] + _dot(p.astype(BF16), v)
        m_sc[...] = m_new

    def body(kt, carry):
        step(kt, False)
        return carry

    lax.fori_loop(0, qi, body, 0)
    step(qi, True)
    o_all = acc_sc[...] / l_sc[...]
    lam = _lambda(dl_ref, lam_init)
    o = o_all[:tq] - lam * o_all[tq:]
    o_ref[...] = _head_norm(o, g_ref[...], lam_init)


def _diff_prompt(qa, kvb, diff_lambda, diff_norm_g, nb, seq_len, lam_init):
    tq = min(256, seq_len)
    nq = seq_len // tq
    kern = functools.partial(_diff_prompt_kernel, tq=tq, lam_init=lam_init)
    return pl.pallas_call(
        kern,
        grid=(nb, A_HEADS, nq),
        in_specs=[pl.BlockSpec((4, HD), lambda b, h, i: (0, 0)),
                  pl.BlockSpec((1, 2 * HD), lambda b, h, i: (0, 0)),
                  pl.BlockSpec((tq, 2 * HD), lambda b, h, i: (b * nq + i, h)),
                  pl.BlockSpec((seq_len, 2 * HD), lambda b, h, i: (b, h)),
                  pl.BlockSpec((seq_len, 2 * HD), lambda b, h, i: (b, A_HEADS + h))],
        out_specs=pl.BlockSpec((tq, 2 * HD), lambda b, h, i: (b * nq + i, h)),
        out_shape=jax.ShapeDtypeStruct((nb * seq_len, A_WIDTH), F32),
        scratch_shapes=[pltpu.VMEM((2 * tq, LANES), F32), pltpu.VMEM((2 * tq, LANES), F32),
                        pltpu.VMEM((2 * tq, 2 * HD), F32)],
        compiler_params=_params(("parallel", "parallel", "arbitrary")),
        name="diff_prompt",
    )(diff_lambda, diff_norm_g, qa, kvb, kvb)


def _diff_sample_kernel(pt_ref, dl_ref, g_ref, q_ref, new_ref, *rest, n_pages, past, lam_init):
    del pt_ref
    pages = rest[:n_pages]
    o_ref = rest[n_pages]
    kv_sc = rest[n_pages + 1]
    for j in range(n_pages):
        kv_sc[j * PAGE_SIZE:(j + 1) * PAGE_SIZE, :] = pages[j][...].astype(BF16)
    new8 = new_ref[...]
    pad = jnp.zeros((PAGE_SIZE - SUBLANES, 2 * A_WIDTH), F32)
    kv_sc[past:past + PAGE_SIZE, :] = jnp.concatenate([new8, pad], axis=0).astype(BF16)
    lk = past + PAGE_SIZE
    q = q_ref[...]
    lam = _lambda(dl_ref, lam_init)
    kpos = lax.broadcasted_iota(jnp.int32, (2 * SUBLANES, lk), 1)
    r = lax.broadcasted_iota(jnp.int32, (2 * SUBLANES, lk), 0)
    tok = jnp.where(r >= SUBLANES, r - SUBLANES, r)
    mask = kpos <= past + tok
    for h in range(A_HEADS):
        q2 = _split_components(q[:, h * 2 * HD:(h + 1) * 2 * HD])
        k = kv_sc[:, h * 2 * HD:(h + 1) * 2 * HD]
        v = kv_sc[:, A_WIDTH + h * 2 * HD:A_WIDTH + (h + 1) * 2 * HD]
        s = jnp.where(mask, _dot_nt(q2, k), MASK_VALUE)
        m = jnp.max(s, axis=1, keepdims=True)
        e = jnp.where(mask, jnp.exp(s - m), 0.0)
        p = e / jnp.maximum(jnp.sum(e, axis=1, keepdims=True), 1e-30)
        pv = _dot(p.astype(BF16), v)
        o = pv[:SUBLANES] - lam * pv[SUBLANES:]
        o_ref[:, h * 2 * HD:(h + 1) * 2 * HD] = _head_norm(o, g_ref[...], lam_init)


def _diff_sample(page_table, cache, layer, qa8, new8, diff_lambda, diff_norm_g, lam_init):
    nb, n_pages = page_table.shape
    past = n_pages * PAGE_SIZE
    kern = functools.partial(_diff_sample_kernel, n_pages=n_pages, past=past, lam_init=lam_init)

    def page_spec(j):
        return pl.BlockSpec((None, None, PAGE_SIZE, 2 * A_WIDTH), lambda b, pt: (layer, pt[b, j], 0, 0))

    grid_spec = pltpu.PrefetchScalarGridSpec(
        num_scalar_prefetch=1,
        grid=(nb,),
        in_specs=[pl.BlockSpec((4, HD), lambda b, pt: (0, 0)),
                  pl.BlockSpec((1, 2 * HD), lambda b, pt: (0, 0)),
                  pl.BlockSpec((None, SUBLANES, A_WIDTH), lambda b, pt: (b, 0, 0)),
                  pl.BlockSpec((None, SUBLANES, 2 * A_WIDTH), lambda b, pt: (b, 0, 0))]
                 + [page_spec(j) for j in range(n_pages)],
        out_specs=pl.BlockSpec((None, SUBLANES, A_WIDTH), lambda b, pt: (b, 0, 0)),
        scratch_shapes=[pltpu.VMEM((past + PAGE_SIZE, 2 * A_WIDTH), BF16)],
    )
    return pl.pallas_call(
        kern,
        grid_spec=grid_spec,
        out_shape=jax.ShapeDtypeStruct((nb, SUBLANES, A_WIDTH), F32),
        compiler_params=_params(("parallel",)),
        name="diff_sample",
    )(page_table, diff_lambda, diff_norm_g, qa8, new8, *([cache] * n_pages))


def _s5_kernel(u_ref, bbd_ref, cbd_ref, d_ref, are_ref, aim_ref, ldt_ref, wglu_ref, bglu_ref, h0re_ref, h0im_ref,
               y_ref, hre_ref, him_ref, buf, hs_re, hs_im, *, nb, tc):
    @pl.when(pl.program_id(0) == 0)
    def _():
        hs_re[...] = h0re_ref[...]
        hs_im[...] = h0im_ref[...]

    a_re = are_ref[...]
    a_im = aim_ref[...]
    dt = jnp.exp(ldt_ref[...])
    mag = jnp.exp(dt * a_re)
    ab_re = mag * jnp.cos(dt * a_im)
    ab_im = mag * jnp.sin(dt * a_im)
    den = a_re * a_re + a_im * a_im
    num_re = ab_re - 1.0
    c_re = (num_re * a_re + ab_im * a_im) / den
    c_im = (ab_im * a_re - num_re * a_im) / den
    full = (SUBLANES, GP)
    ab_re, ab_im, c_re, c_im = [jnp.broadcast_to(v, full) for v in (ab_re, ab_im, c_re, c_im)]

    u = u_ref[...]
    buf[...] = _dot(u.astype(BF16), bbd_ref[...])

    def group(rg, carry):
        r0 = pl.multiple_of(rg * SUBLANES, SUBLANES)

        def step(t, h):
            hr, hi = h
            row = pl.multiple_of(t * nb + r0, SUBLANES)
            br = buf[pl.ds(row, SUBLANES), 0:GP]
            bi = buf[pl.ds(row, SUBLANES), GP:2 * GP]
            nr = ab_re * hr - ab_im * hi + (c_re * br - c_im * bi)
            ni = ab_re * hi + ab_im * hr + (c_re * bi + c_im * br)
            buf[pl.ds(row, SUBLANES), 0:GP] = nr
            buf[pl.ds(row, SUBLANES), GP:2 * GP] = ni
            return nr, ni

        hr, hi = lax.fori_loop(0, tc, step, (hs_re[pl.ds(r0, SUBLANES), :], hs_im[pl.ds(r0, SUBLANES), :]))
        hs_re[pl.ds(r0, SUBLANES), :] = hr
        hs_im[pl.ds(r0, SUBLANES), :] = hi
        return carry

    lax.fori_loop(0, nb // SUBLANES, group, 0)
    y = _dot(buf[...].astype(BF16), cbd_ref[...]) + d_ref[...] * u
    y = jax.nn.gelu(y)
    z = _dot(y.astype(BF16), wglu_ref[...]) + bglu_ref[...]
    y_ref[...] = y * jax.nn.sigmoid(z)
    hre_ref[...] = hs_re[...]
    him_ref[...] = hs_im[...]


def _s5(u, sp, h0_re, h0_im, nb, tc):
    rows = u.shape[0]
    rc = tc * nb
    const = lambda i: (0, 0)
    kern = functools.partial(_s5_kernel, nb=nb, tc=tc)
    return pl.pallas_call(
        kern,
        grid=(rows // rc,),
        in_specs=[pl.BlockSpec((rc, S_WIDTH), lambda i: (i, 0)),
                  pl.BlockSpec((S_WIDTH, 2 * GP), const), pl.BlockSpec((2 * GP, S_WIDTH), const),
                  pl.BlockSpec((1, S_WIDTH), const),
                  pl.BlockSpec((1, GP), const), pl.BlockSpec((1, GP), const), pl.BlockSpec((1, GP), const),
                  pl.BlockSpec((S_WIDTH, S_WIDTH), const), pl.BlockSpec((1, S_WIDTH), const),
                  pl.BlockSpec((nb, GP), const), pl.BlockSpec((nb, GP), const)],
        out_specs=(pl.BlockSpec((rc, S_WIDTH), lambda i: (i, 0)),
                   pl.BlockSpec((nb, GP), const), pl.BlockSpec((nb, GP), const)),
        out_shape=(jax.ShapeDtypeStruct((rows, S_WIDTH), F32),
                   jax.ShapeDtypeStruct((nb, GP), F32), jax.ShapeDtypeStruct((nb, GP), F32)),
        scratch_shapes=[pltpu.VMEM((rc, 2 * GP), F32), pltpu.VMEM((nb, GP), F32), pltpu.VMEM((nb, GP), F32)],
        compiler_params=_params(("arbitrary",)),
        name="s5",
    )(u, sp["bbd"], sp["cbd"], sp["d"], sp["a_re"], sp["a_im"], sp["log_dt"], sp["w_glu"], sp["b_glu"], h0_re, h0_im)


def _masked_softmax(s, mask):
    s = jnp.where(mask, s, MASK_VALUE)
    m = jnp.max(s, axis=-1, keepdims=True)
    e = jnp.where(mask, jnp.exp(s - m), 0.0)
    return e / jnp.maximum(jnp.sum(e, axis=-1, keepdims=True), 1e-30)


def _stack_heads(q):
    return jnp.concatenate([q[:, g * HD:(g + 1) * HD] for g in range(C_HEADS)], axis=0)


def _compress(rows, wcol):
    n = rows.shape[0] // CMP_BLOCK
    return jnp.sum(rows.reshape(n, CMP_BLOCK, HD) * wcol[None], axis=1)


def _select_blocks(psum, pair, t, ns):
    imp = _dot(psum, pair, precision=HIGHEST)
    blk = lax.broadcasted_iota(jnp.int32, imp.shape, 1)
    cur = t // SEL_BLOCK
    valid = blk <= cur
    forced = jnp.where(blk == 0, 1.0, jnp.where(blk == cur, 1.0, jnp.where(blk == cur - 1, 1.0, 0.0)))
    score = jnp.where(valid, imp + FORCE_SCORE * forced, -1.0)
    score = jnp.where(blk < ns, score, -2.0)
    rank = jnp.zeros(imp.shape, F32)
    for s2 in range(ns):
        col = score[:, s2:s2 + 1]
        ahead = jnp.where(col > score, 1.0, jnp.where(col == score, jnp.where(blk > s2, 1.0, 0.0), 0.0))
        rank = rank + ahead
    chosen = jnp.where(rank < float(min(TOP_N, ns)), 1.0, 0.0)
    return jnp.where(blk < ns, chosen, 0.0)


def _combine(o_cmp, o_sel, o_win, gates, r):
    outs = []
    for g in range(C_HEADS):
        sl = slice(g * r, (g + 1) * r)
        outs.append(gates[:, 3 * g:3 * g + 1] * o_cmp[sl] + gates[:, 3 * g + 1:3 * g + 2] * o_sel[sl]
                    + gates[:, 3 * g + 2:3 * g + 3] * o_win[sl])
    return jnp.concatenate(outs, axis=1)


def _nsa_prompt_kernel(q_ref, nsa_ref, win_ref, gate_ref, wcol_ref, pair_ref, e_ref, o_ref,
                       kc_sc, vc_sc, selk_sc, selv_sc, wink_sc, winv_sc, selexp_sc, m_sc, l_sc, acc_sc, *, seq_len):
    i = pl.program_id(1)
    nc = seq_len // CMP_BLOCK
    ns = seq_len // SEL_BLOCK
    rows = C_HEADS * Q_TILE

    @pl.when(i == 0)
    def _():
        r = nsa_ref[...]
        kc_sc[...] = _compress(r[:, 0:HD], wcol_ref[0])
        vc_sc[...] = _compress(r[:, HD:2 * HD], wcol_ref[1])
        selk_sc[...] = r[:, 2 * HD:3 * HD].astype(BF16)
        selv_sc[...] = r[:, 3 * HD:4 * HD].astype(BF16)
        w = win_ref[...]
        wink_sc[...] = w[:, 0:HD].astype(BF16)
        winv_sc[...] = w[:, HD:2 * HD].astype(BF16)

    qs = _stack_heads(q_ref[...])
    qb = qs.astype(BF16)
    t1 = i * Q_TILE + lax.broadcasted_iota(jnp.int32, (Q_TILE, 1), 0)
    t4 = jnp.concatenate([t1] * C_HEADS, axis=0)

    sc = _dot_nt(qs, kc_sc[...], precision=HIGHEST)
    n_idx = lax.broadcasted_iota(jnp.int32, sc.shape, 1)
    p_cmp = _masked_softmax(sc, (n_idx + 1) * CMP_BLOCK - 1 <= t4)
    o_cmp = _dot(p_cmp, vc_sc[...], precision=HIGHEST)
    psum = p_cmp[0:Q_TILE]
    for g in range(1, C_HEADS):
        psum = psum + p_cmp[g * Q_TILE:(g + 1) * Q_TILE]
    chosen = _select_blocks(psum, pair_ref[...], jnp.broadcast_to(t1, (Q_TILE, LANES)), ns)
    selexp_sc[...] = _dot(chosen.astype(BF16), e_ref[...])

    def flash(k_sc, v_sc, first_tile, n_tiles, tk, mask_fn):
        m_sc[...] = jnp.full(m_sc.shape, MASK_VALUE, F32)
        l_sc[...] = jnp.zeros(l_sc.shape, F32)
        acc_sc[...] = jnp.zeros(acc_sc.shape, F32)

        def body(j, carry):
            off = pl.multiple_of((first_tile + j) * tk, tk)
            k = k_sc[pl.ds(off, tk), :]
            v = v_sc[pl.ds(off, tk), :]
            kpos = off + lax.broadcasted_iota(jnp.int32, (rows, tk), 1)
            mask = mask_fn(off, kpos)
            s = jnp.where(mask, _dot_nt(qb, k), MASK_VALUE)
            m_prev = m_sc[...]
            m_new = jnp.maximum(m_prev, jnp.max(s, axis=1, keepdims=True))
            alpha = jnp.exp(m_prev - m_new)
            p = jnp.where(mask, jnp.exp(s - _rep_lanes(m_new, tk)), 0.0)
            l_sc[...] = alpha * l_sc[...] + jnp.sum(p, axis=1, keepdims=True)
            acc_sc[...] = alpha[:, :HD] * acc_sc[...] + _dot(p.astype(BF16), v)
            m_sc[...] = m_new
            return carry

        lax.fori_loop(0, n_tiles, body, 0)
        return acc_sc[...] / jnp.maximum(l_sc[...], 1e-30)[:, :HD]

    def sel_mask(off, kpos):
        sx = selexp_sc[:, pl.ds(off, SEL_TILE)]
        sx = jnp.concatenate([sx] * C_HEADS, axis=0)
        return (sx > 0.5) & (kpos <= t4)

    def win_mask(off, kpos):
        return (kpos <= t4) & (kpos > t4 - WINDOW)

    o_sel = flash(selk_sc, selv_sc, 0, (i * Q_TILE) // SEL_TILE + 1, SEL_TILE, sel_mask)
    back = WINDOW // WIN_TILE
    first = jnp.maximum(i - back, 0)
    o_win = flash(wink_sc, winv_sc, first, i - first + 1, WIN_TILE, win_mask)
    o_ref[...] = _combine(o_cmp, o_sel, o_win, gate_ref[...], Q_TILE)


def _nsa_prompt(qc, nsa, win, gates, wcol, nb, seq_len):
    nqb = seq_len // Q_TILE
    nc = seq_len // CMP_BLOCK
    pair = (np.arange(nc)[:, None] // (SEL_BLOCK // CMP_BLOCK) == np.arange(LANES)[None, :]).astype(np.float32)
    expand = (np.arange(LANES)[:, None] == np.arange(seq_len)[None, :] // SEL_BLOCK).astype(np.float32)
    kern = functools.partial(_nsa_prompt_kernel, seq_len=seq_len)
    rows = C_HEADS * Q_TILE
    return pl.pallas_call(
        kern,
        grid=(nb, nqb),
        in_specs=[pl.BlockSpec((Q_TILE, C_WIDTH), lambda b, i: (b * nqb + i, 0)),
                  pl.BlockSpec((seq_len, 4 * HD), lambda b, i: (b, 0)),
                  pl.BlockSpec((seq_len, 2 * HD), lambda b, i: (b, 0)),
                  pl.BlockSpec((Q_TILE, LANES), lambda b, i: (b * nqb + i, 0)),
                  pl.BlockSpec((2, CMP_BLOCK, HD), lambda b, i: (0, 0, 0)),
                  pl.BlockSpec((nc, LANES), lambda b, i: (0, 0)),
                  pl.BlockSpec((LANES, seq_len), lambda b, i: (0, 0))],
        out_specs=pl.BlockSpec((Q_TILE, C_WIDTH), lambda b, i: (b * nqb + i, 0)),
        out_shape=jax.ShapeDtypeStruct((nb * seq_len, C_WIDTH), F32),
        scratch_shapes=[pltpu.VMEM((nc, HD), F32), pltpu.VMEM((nc, HD), F32),
                        pltpu.VMEM((seq_len, HD), BF16), pltpu.VMEM((seq_len, HD), BF16),
                        pltpu.VMEM((seq_len, HD), BF16), pltpu.VMEM((seq_len, HD), BF16),
                        pltpu.VMEM((Q_TILE, seq_len), F32),
                        pltpu.VMEM((rows, LANES), F32), pltpu.VMEM((rows, LANES), F32), pltpu.VMEM((rows, HD), F32)],
        compiler_params=_params(("parallel", "arbitrary")),
        name="nsa_prompt",
    )(qc, nsa, win, gates, wcol, jnp.asarray(pair), jnp.asarray(expand, dtype=BF16))


def _nsa_sample_kernel(pt_ref, q_ref, new_ref, winbuf_ref, winnew_ref, gate_ref, wcol_ref, pair_ref, e_ref, *rest,
                       n_pages, past, w_buf, dec_seq):
    del pt_ref
    pages = rest[:n_pages]
    o_ref = rest[n_pages]
    rows_sc, win_sc = rest[n_pages + 1:]
    lk = past + PAGE_SIZE
    nc = (past + dec_seq) // CMP_BLOCK
    ns = -(-(past + dec_seq) // SEL_BLOCK)
    for j in range(n_pages):
        rows_sc[j * PAGE_SIZE:(j + 1) * PAGE_SIZE, :] = pages[j][...]
    rows_sc[past:past + SUBLANES, :] = new_ref[...]
    rows_sc[past + SUBLANES:lk, :] = jnp.zeros((PAGE_SIZE - SUBLANES, 4 * HD), F32)
    win_sc[0:w_buf, :] = winbuf_ref[...]
    win_sc[w_buf:w_buf + SUBLANES, :] = winnew_ref[...]
    win_sc[w_buf + SUBLANES:w_buf + PAGE_SIZE, :] = jnp.zeros((PAGE_SIZE - SUBLANES, 2 * HD), F32)

    r = rows_sc[...]
    kc = _compress(r[0:nc * CMP_BLOCK, 0:HD], wcol_ref[0])
    vc = _compress(r[0:nc * CMP_BLOCK, HD:2 * HD], wcol_ref[1])
    qs = _stack_heads(q_ref[...])
    qb = qs.astype(BF16)
    nrow = C_HEADS * SUBLANES
    t1 = past + lax.broadcasted_iota(jnp.int32, (SUBLANES, 1), 0)
    t4 = jnp.concatenate([t1] * C_HEADS, axis=0)

    sc = _dot_nt(qs, kc, precision=HIGHEST)
    n_idx = lax.broadcasted_iota(jnp.int32, sc.shape, 1)
    p_cmp = _masked_softmax(sc, (n_idx + 1) * CMP_BLOCK - 1 <= t4)
    o_cmp = _dot(p_cmp, vc, precision=HIGHEST)
    psum = p_cmp[0:SUBLANES]
    for g in range(1, C_HEADS):
        psum = psum + p_cmp[g * SUBLANES:(g + 1) * SUBLANES]
    chosen = _select_blocks(psum, pair_ref[...], jnp.broadcast_to(t1, (SUBLANES, LANES)), ns)
    sx = _dot(chosen.astype(BF16), e_ref[...])
    sx = jnp.concatenate([sx] * C_HEADS, axis=0)
    kpos = lax.broadcasted_iota(jnp.int32, (nrow, lk), 1)
    smask = (sx > 0.5) & (kpos <= t4)
    p_sel = _masked_softmax(_dot_nt(qb, r[:, 2 * HD:3 * HD].astype(BF16)), smask)
    o_sel = _dot(p_sel.astype(BF16), r[:, 3 * HD:4 * HD].astype(BF16))

    w = win_sc[...]
    wpos = past - w_buf + lax.broadcasted_iota(jnp.int32, (nrow, w_buf + PAGE_SIZE), 1)
    wmask = (wpos <= t4) & (wpos > t4 - WINDOW) & (wpos >= 0)
    p_win = _masked_softmax(_dot_nt(qb, w[:, 0:HD].astype(BF16)), wmask)
    o_win = _dot(p_win.astype(BF16), w[:, HD:2 * HD].astype(BF16))
    o_ref[...] = _combine(o_cmp, o_sel, o_win, gate_ref[...], SUBLANES)


def _nsa_sample(page_table, cache, layer, qc8, new8, win_buf, winnew8, gates8, wcol, dec_seq):
    nb, n_pages = page_table.shape
    past = n_pages * PAGE_SIZE
    w_buf = win_buf.shape[2]
    lk = past + PAGE_SIZE
    nc = (past + dec_seq) // CMP_BLOCK
    pair = (np.arange(nc)[:, None] // (SEL_BLOCK // CMP_BLOCK) == np.arange(LANES)[None, :]).astype(np.float32)
    expand = (np.arange(LANES)[:, None] == np.arange(lk)[None, :] // SEL_BLOCK).astype(np.float32)
    kern = functools.partial(_nsa_sample_kernel, n_pages=n_pages, past=past, w_buf=w_buf, dec_seq=dec_seq)

    def page_spec(j):
        return pl.BlockSpec((None, None, PAGE_SIZE, 4 * HD), lambda b, pt: (layer, pt[b, j], 0, 0))

    def per_b(width):
        return pl.BlockSpec((None, SUBLANES, width), lambda b, pt: (b, 0, 0))

    grid_spec = pltpu.PrefetchScalarGridSpec(
        num_scalar_prefetch=1,
        grid=(nb,),
        in_specs=[per_b(C_WIDTH), per_b(4 * HD),
                  pl.BlockSpec((None, None, w_buf, 2 * HD), lambda b, pt: (layer, b, 0, 0)),
                  per_b(2 * HD), per_b(LANES),
                  pl.BlockSpec((2, CMP_BLOCK, HD), lambda b, pt: (0, 0, 0)),
                  pl.BlockSpec((nc, LANES), lambda b, pt: (0, 0)),
                  pl.BlockSpec((LANES, lk), lambda b, pt: (0, 0))]
                 + [page_spec(j) for j in range(n_pages)],
        out_specs=per_b(C_WIDTH),
        scratch_shapes=[pltpu.VMEM((lk, 4 * HD), F32), pltpu.VMEM((w_buf + PAGE_SIZE, 2 * HD), F32)],
    )
    return pl.pallas_call(
        kern,
        grid_spec=grid_spec,
        out_shape=jax.ShapeDtypeStruct((nb, SUBLANES, C_WIDTH), F32),
        compiler_params=_params(("parallel",)),
        name="nsa_sample",
    )(page_table, qc8, new8, win_buf, winnew8, gates8, wcol, jnp.asarray(pair), jnp.asarray(expand, dtype=BF16),
      *([cache] * n_pages))


def _merge_kernel(x_ref, oa_ref, za_ref, os_ref, zs_ref, oc_ref, zc_ref, w_ref, gf_ref, y_ref, *, final):
    mix = jnp.concatenate([oa_ref[...] * jax.nn.silu(za_ref[...]),
                           os_ref[...] * jax.nn.silu(zs_ref[...]),
                           oc_ref[...] * jax.nn.silu(zc_ref[...])], axis=1)
    y = x_ref[...] + _dot(mix.astype(BF16), w_ref[...])
    if final:
        ms = jnp.mean(y * y, axis=-1, keepdims=True)
        y = y * lax.rsqrt(ms + RMS_EPS) * gf_ref[...]
    y_ref[...] = y


def _merge(x, oa, za, os_, zs, oc, zc, w_out, final_g, tm, seq_len, os_time_major, final):
    n = x.shape[0]
    nt = seq_len // tm
    row = lambda i: (i, 0)
    os_spec = pl.BlockSpec((tm, S_WIDTH), (lambda i: (i % nt, i // nt)) if os_time_major else row)
    return pl.pallas_call(
        functools.partial(_merge_kernel, final=final),
        grid=(n // tm,),
        in_specs=[pl.BlockSpec((tm, D_MODEL), row), pl.BlockSpec((tm, A_WIDTH), row), pl.BlockSpec((tm, A_WIDTH), row),
                  os_spec, pl.BlockSpec((tm, S_WIDTH), row),
                  pl.BlockSpec((tm, C_WIDTH), row), pl.BlockSpec((tm, C_WIDTH), row),
                  pl.BlockSpec((D_MODEL, D_MODEL), lambda i: (0, 0)), pl.BlockSpec((1, D_MODEL), lambda i: (0, 0))],
        out_specs=pl.BlockSpec((tm, D_MODEL), row),
        out_shape=jax.ShapeDtypeStruct((n, D_MODEL), F32),
        compiler_params=_params(("parallel",)),
        name="merge",
    )(x, oa, za, os_, zs, oc, zc, w_out, final_g)


def _rope_tables(pos):
    half = HD // 2
    inv = ROPE_THETA ** (-jnp.arange(half, dtype=F32) / half)
    ang = pos.astype(F32)[:, None] * inv[None, :]
    cos = jnp.tile(jnp.cos(ang), (1, LANES // half))
    sin = jnp.tile(jnp.sin(ang), (1, LANES // half))
    lane = jnp.arange(LANES)
    sin = jnp.where((lane % HD) < half, -sin, sin)
    is_k = lane < HD
    return jnp.concatenate([cos, sin, jnp.where(is_k, cos, 1.0), jnp.where(is_k, sin, 0.0)], axis=1)


def _layer_weights(layer, norm_g, w_in, w_out, diff_lambda, diff_norm_g, ssm_A_re, ssm_A_im, ssm_log_dt,
                   ssm_B_re, ssm_B_im, ssm_C_re, ssm_C_im, ssm_D, ssm_w_glu, ssm_b_glu, nsa_cmp_w):
    w = w_in[layer]
    old_gc = 4 * A_WIDTH + 2 * S_WIDTH + C_WIDTH + 6 * HD
    w_pad = jnp.concatenate([w[:, :old_gc], w[:, old_gc + N_GATES:], w[:, old_gc:old_gc + N_GATES],
                             jnp.zeros((D_MODEL, LANES - N_GATES), F32)], axis=1).astype(BF16)
    eye = jnp.eye(S_GROUPS, dtype=F32)
    b_re = jnp.einsum("gpc,gh->gchp", ssm_B_re[layer], eye).reshape(S_WIDTH, GP)
    b_im = jnp.einsum("gpc,gh->gchp", ssm_B_im[layer], eye).reshape(S_WIDTH, GP)
    c_re = jnp.einsum("gcp,gh->gphc", ssm_C_re[layer], eye).reshape(GP, S_WIDTH)
    c_im = jnp.einsum("gcp,gh->gphc", ssm_C_im[layer], eye).reshape(GP, S_WIDTH)
    s5 = {
        "bbd": jnp.concatenate([b_re, b_im], axis=1).astype(BF16),
        "cbd": jnp.concatenate([c_re, -c_im], axis=0).astype(BF16),
        "d": ssm_D[layer].reshape(1, S_WIDTH),
        "a_re": ssm_A_re[layer].reshape(1, GP),
        "a_im": ssm_A_im[layer].reshape(1, GP),
        "log_dt": jnp.broadcast_to(ssm_log_dt[layer][:, None], (S_GROUPS, S_STATE)).reshape(1, GP),
        "w_glu": ssm_w_glu[layer].astype(BF16),
        "b_glu": ssm_b_glu[layer].reshape(1, S_WIDTH),
    }
    return {
        "norm_g": norm_g[layer].reshape(1, D_MODEL),
        "w_in": w_pad,
        "w_out": w_out[layer].astype(BF16),
        "diff_lambda": diff_lambda[layer],
        "diff_norm_g": diff_norm_g[layer].reshape(1, 2 * HD),
        "s5": s5,
        "wcol": jnp.broadcast_to(nsa_cmp_w[layer][:, :, None], (2, CMP_BLOCK, HD)),
        "lam_init": 0.8 - 0.6 * math.exp(-0.3 * layer),
    }


def _prompt_layer(x, lw, tab, nb, seq_len, final_g, final):
    tm = min(256, seq_len)
    qa, dr, kvb, za, us, zs, qc, nsa, win, zc, gates = _inproj(x, lw["norm_g"], lw["w_in"], tab, seq_len, tm, True)
    oa = _diff_prompt(qa, kvb, lw["diff_lambda"], lw["diff_norm_g"], nb, seq_len, lw["lam_init"])
    zero = jnp.zeros((nb, GP), F32)
    os_, h_re, h_im = _s5(us.reshape(seq_len * nb, S_WIDTH), lw["s5"], zero, zero, nb, min(128, seq_len))
    os_ = os_.reshape(seq_len, nb * S_WIDTH)
    oc = _nsa_prompt(qc, nsa, win, gates, lw["wcol"], nb, seq_len)
    y = _merge(x, oa, za, os_, zs, oc, zc, lw["w_out"], final_g, tm, seq_len, True, final)
    return y, dr, nsa, win, h_re, h_im


def _pad_rows(a, nb, t):
    a = a.reshape(nb, t, a.shape[-1])
    return jnp.pad(a, ((0, 0), (0, SUBLANES - t), (0, 0)))


def _sample_layer(x, lw, tab, layer, nb, t, page_table, cache_diff, cache_nsa, cache_win, h0_re, h0_im, final_g, final):
    n = nb * t
    qa, dr, kvb, za, us, zs, qc, nsa, win, zc, gates = _inproj(x, lw["norm_g"], lw["w_in"], tab, t, n, False)
    oa8 = _diff_sample(page_table, cache_diff, layer, _pad_rows(qa, nb, t), _pad_rows(dr, nb, t),
                       lw["diff_lambda"], lw["diff_norm_g"], lw["lam_init"])
    oa = oa8[:, :t].reshape(n, A_WIDTH)
    u_tb = us.reshape(nb, t, S_WIDTH).transpose(1, 0, 2).reshape(n, S_WIDTH)
    os_tb, h_re, h_im = _s5(u_tb, lw["s5"], h0_re, h0_im, nb, t)
    os_ = os_tb.reshape(t, nb, S_WIDTH).transpose(1, 0, 2).reshape(n, S_WIDTH)
    oc8 = _nsa_sample(page_table, cache_nsa, layer, _pad_rows(qc, nb, t), _pad_rows(nsa, nb, t), cache_win,
                      _pad_rows(win, nb, t), _pad_rows(gates, nb, t), lw["wcol"], t)
    oc = oc8[:, :t].reshape(n, C_WIDTH)
    y = _merge(x, oa, za, os_, zs, oc, zc, lw["w_out"], final_g, n, t, False, final)
    return y, dr, nsa, win, h_re, h_im


def kernel(x_prompt, x_sample, cache_diff_kv, cache_nsa_kv, cache_win_kv, state_ssm_re, state_ssm_im, page_table, norm_g, w_in, w_out, diff_lambda, diff_norm_g, ssm_A_re, ssm_A_im, ssm_log_dt, ssm_B_re, ssm_B_im, ssm_C_re, ssm_C_im, ssm_D, ssm_w_glu, ssm_b_glu, nsa_cmp_w, final_norm_g):
    nbp, seq_len, _ = x_prompt.shape
    nbs, dec_seq, _ = x_sample.shape
    depth = w_in.shape[0]
    n_pool = cache_diff_kv.shape[1]
    past = page_table.shape[1] * PAGE_SIZE
    w_buf = cache_win_kv.shape[2]
    cache_diff = cache_diff_kv.reshape(depth, n_pool, PAGE_SIZE, 2 * A_WIDTH)
    cache_nsa = cache_nsa_kv.reshape(depth, n_pool, PAGE_SIZE, 4 * HD)
    cache_win = cache_win_kv.reshape(depth, nbs, w_buf, 2 * HD)
    tab_p = _rope_tables(jnp.arange(seq_len, dtype=jnp.int32))
    tab_s = jnp.tile(_rope_tables(past + jnp.arange(dec_seq, dtype=jnp.int32)), (nbs, 1))
    final_g = final_norm_g.reshape(1, D_MODEL)

    xp = x_prompt.reshape(nbp * seq_len, D_MODEL)
    xs = x_sample.reshape(nbs * dec_seq, D_MODEL)
    outs = [[] for _ in range(10)]
    w_keep = min(WINDOW, seq_len)
    for layer in range(depth):
        lw = _layer_weights(layer, norm_g, w_in, w_out, diff_lambda, diff_norm_g, ssm_A_re, ssm_A_im, ssm_log_dt,
                            ssm_B_re, ssm_B_im, ssm_C_re, ssm_C_im, ssm_D, ssm_w_glu, ssm_b_glu, nsa_cmp_w)
        final = layer == depth - 1
        xp, dr, nsa, win, h_re, h_im = _prompt_layer(xp, lw, tab_p, nbp, seq_len, final_g, final)
        outs[0].append(dr.reshape(nbp, seq_len, 2, A_HEADS, 2 * HD))
        outs[2].append(nsa.reshape(nbp, seq_len, 4, 1, HD))
        outs[4].append(win.reshape(nbp, seq_len, 2, 1, HD)[:, seq_len - w_keep:])
        outs[6].append(h_re.reshape(nbp, S_GROUPS, S_STATE))
        outs[7].append(h_im.reshape(nbp, S_GROUPS, S_STATE))
        xs, dr, nsa, win, h_re, h_im = _sample_layer(
            xs, lw, tab_s, layer, nbs, dec_seq, page_table, cache_diff, cache_nsa, cache_win,
            state_ssm_re[layer].reshape(nbs, GP), state_ssm_im[layer].reshape(nbs, GP), final_g, final)
        outs[1].append(dr.reshape(nbs, dec_seq, 2, A_HEADS, 2 * HD))
        outs[3].append(nsa.reshape(nbs, dec_seq, 4, 1, HD))
        win_new = win.reshape(nbs, dec_seq, 2, 1, HD)
        outs[5].append(jnp.concatenate([cache_win_kv[layer], win_new], axis=1)[:, dec_seq:])
        outs[8].append(h_re.reshape(nbs, S_GROUPS, S_STATE))
        outs[9].append(h_im.reshape(nbs, S_GROUPS, S_STATE))
    stacked = [jnp.stack(o) for o in outs]
    return (xp.reshape(nbp, seq_len, D_MODEL), xs.reshape(nbs, dec_seq, D_MODEL), *stacked)
```

```python
import functools
import math

import jax
import jax.numpy as jnp
import numpy as np
from jax import lax
from jax.experimental import pallas as pl
from jax.experimental.pallas import tpu as pltpu

F32 = jnp.float32
BF16 = jnp.bfloat16

D_MODEL = 1024
PAGE_SIZE = 128
HD = 64
A_WIDTH = D_MODEL // 2
A_HEADS = A_WIDTH // (2 * HD)
S_WIDTH = D_MODEL // 4
S_CH = 16
S_GROUPS = S_WIDTH // S_CH
S_STATE = 64
C_WIDTH = D_MODEL - A_WIDTH - S_WIDTH
C_HEADS = C_WIDTH // HD
CMP_BLOCK = 32
SEL_BLOCK = 64
TOP_N = 8
WINDOW = 512
ROPE_THETA = 10000.0
RMS_EPS = 1e-6
MASK_VALUE = -1e30
FORCE_SCORE = 1e4
SCALE = HD ** -0.5
GP = S_GROUPS * S_STATE

LANES = 128
SUBLANES = 8
VMEM_LIMIT_BYTES = 56 * 1024 * 1024

COL_QA = 0
COL_KA = COL_QA + A_WIDTH
COL_VA = COL_KA + A_WIDTH
COL_ZA = COL_VA + A_WIDTH
COL_US = COL_ZA + A_WIDTH
COL_ZS = COL_US + S_WIDTH
COL_QC = COL_ZS + S_WIDTH
COL_KVC = COL_QC + C_WIDTH
COL_ZC = COL_KVC + 6 * HD
COL_GC = COL_ZC + C_WIDTH
IN_PAD = COL_GC + LANES
N_GATES = 3 * C_HEADS

Q_TILE = 128
SEL_TILE = 256
WIN_TILE = 128
HIGHEST = lax.Precision.HIGHEST


def _params(semantics):
    return pltpu.CompilerParams(dimension_semantics=semantics, vmem_limit_bytes=VMEM_LIMIT_BYTES)


def _dot_nt(a, b, precision=None):
    return lax.dot_general(a, b, (((1,), (1,)), ((), ())), preferred_element_type=F32, precision=precision)


def _dot(a, b, precision=None):
    return jnp.dot(a, b, preferred_element_type=F32, precision=precision)


def _rep_lanes(x, width):
    if width == LANES:
        return x
    if width < LANES:
        return x[:, :width]
    return jnp.concatenate([x] * (width // LANES), axis=1)


def _inproj_kernel(x_ref, g_ref, w_ref, tab_ref,
                   qa_ref, dr_ref, kvb_ref, za_ref, us_ref, zs_ref, qc_ref, nsa_ref, win_ref, zc_ref, gate_ref,
                   *transposed_refs):
    x = x_ref[...]
    ms = jnp.mean(x * x, axis=-1, keepdims=True)
    xn = (x * lax.rsqrt(ms + RMS_EPS) * g_ref[...]).astype(BF16)
    tm = x.shape[0]
    cos = tab_ref[:, 0:LANES]
    sin = tab_ref[:, LANES:2 * LANES]
    cos_kv = tab_ref[:, 2 * LANES:3 * LANES]
    sin_kv = tab_ref[:, 3 * LANES:4 * LANES]
    lane = lax.broadcasted_iota(jnp.int32, (tm, LANES), 1)
    first_half = (lane % HD) < (HD // 2)

    def rope(v, c, s):
        partner = jnp.where(first_half, pltpu.roll(v, LANES - HD // 2, 1), pltpu.roll(v, HD // 2, 1))
        return v * c + partner * s

    def proj(c0, width):
        return _dot(xn, w_ref[:, c0:c0 + width])

    def chunks(y):
        return [y[:, j * LANES:(j + 1) * LANES] for j in range(y.shape[1] // LANES)]

    y = proj(COL_QA, A_WIDTH)
    for j, c in enumerate(chunks(y)):
        qa_ref[:, j * LANES:(j + 1) * LANES] = (rope(c, cos, sin) * SCALE).astype(BF16)
    y = proj(COL_KA, A_WIDTH)
    for j, c in enumerate(chunks(y)):
        r = rope(c, cos, sin)
        dr_ref[pl.ds(j, tm, stride=2 * A_HEADS), :] = r
        kvb_ref[:, j * LANES:(j + 1) * LANES] = r.astype(BF16)
    y = proj(COL_VA, A_WIDTH)
    for j, c in enumerate(chunks(y)):
        dr_ref[pl.ds(A_HEADS + j, tm, stride=2 * A_HEADS), :] = c
    kvb_ref[:, A_WIDTH:2 * A_WIDTH] = y.astype(BF16)
    za_ref[...] = proj(COL_ZA, A_WIDTH)
    y = proj(COL_US, 2 * S_WIDTH)
    us_ref[...] = y[:, :S_WIDTH]
    zs_ref[...] = y[:, S_WIDTH:]
    y = proj(COL_QC, C_WIDTH)
    for j, c in enumerate(chunks(y)):
        qc_ref[:, j * LANES:(j + 1) * LANES] = rope(c, cos, sin) * SCALE
    y = proj(COL_KVC, 6 * HD)
    kv = [rope(c, cos_kv, sin_kv) for c in chunks(y)]
    nsa_ref[:, 0:LANES] = kv[0]
    nsa_ref[:, LANES:2 * LANES] = kv[1]
    win_ref[...] = kv[2]
    if transposed_refs:
        nsa_t_ref, win_t_ref = transposed_refs
        nsa_t_ref[0:LANES, :] = kv[0].T
        nsa_t_ref[LANES:2 * LANES, :] = kv[1].T
        win_t_ref[...] = kv[2].T
    y = proj(COL_ZC, C_WIDTH + LANES)
    zc_ref[...] = y[:, :C_WIDTH]
    gate_ref[...] = jax.nn.sigmoid(y[:, C_WIDTH:])


def _inproj(x, norm_g, w, tab, seq_len, tm, us_time_major):
    n = x.shape[0]
    nt = tab.shape[0] // tm
    grid = (n // tm,)
    row = lambda i: (i, 0)
    nb = n // seq_len
    if us_time_major:
        us_shape = jax.ShapeDtypeStruct((seq_len, nb * S_WIDTH), F32)
        us_spec = pl.BlockSpec((tm, S_WIDTH), lambda i: (i % nt, i // nt))
    else:
        us_shape = jax.ShapeDtypeStruct((n, S_WIDTH), F32)
        us_spec = pl.BlockSpec((tm, S_WIDTH), row)
    rows_per_token = 2 * A_HEADS
    out_shape = (
        jax.ShapeDtypeStruct((n, A_WIDTH), BF16),
        jax.ShapeDtypeStruct((n * rows_per_token, 2 * HD), F32),
        jax.ShapeDtypeStruct((n, 2 * A_WIDTH), BF16),
        jax.ShapeDtypeStruct((n, A_WIDTH), F32),
        us_shape,
        jax.ShapeDtypeStruct((n, S_WIDTH), F32),
        jax.ShapeDtypeStruct((n, C_WIDTH), F32),
        jax.ShapeDtypeStruct((n, 4 * HD), F32),
        jax.ShapeDtypeStruct((n, 2 * HD), F32),
        jax.ShapeDtypeStruct((n, C_WIDTH), F32),
        jax.ShapeDtypeStruct((n, LANES), F32),
    )
    out_specs = (
        pl.BlockSpec((tm, A_WIDTH), row), pl.BlockSpec((tm * rows_per_token, 2 * HD), row),
        pl.BlockSpec((tm, 2 * A_WIDTH), row),
        pl.BlockSpec((tm, A_WIDTH), row), us_spec, pl.BlockSpec((tm, S_WIDTH), row),
        pl.BlockSpec((tm, C_WIDTH), row), pl.BlockSpec((tm, 4 * HD), row), pl.BlockSpec((tm, 2 * HD), row),
        pl.BlockSpec((tm, C_WIDTH), row), pl.BlockSpec((tm, LANES), row),
    )
    if us_time_major:
        out_shape += (jax.ShapeDtypeStruct((nb, 4 * HD, seq_len), F32), jax.ShapeDtypeStruct((nb, 2 * HD, seq_len), F32))
        out_specs += (pl.BlockSpec((None, 4 * HD, tm), lambda i: (i // nt, 0, i % nt)),
                      pl.BlockSpec((None, 2 * HD, tm), lambda i: (i // nt, 0, i % nt)))
    return pl.pallas_call(
        _inproj_kernel,
        grid=grid,
        in_specs=[pl.BlockSpec((tm, D_MODEL), row),
                  pl.BlockSpec((1, D_MODEL), lambda i: (0, 0)),
                  pl.BlockSpec((D_MODEL, IN_PAD), lambda i: (0, 0)),
                  pl.BlockSpec((tm, 4 * LANES), lambda i: (i % nt, 0))],
        out_specs=out_specs,
        out_shape=out_shape,
        compiler_params=_params(("parallel",)),
        name="inproj",
    )(x, norm_g, w, tab)


def _lambda(dl_ref, lam_init):
    dl = dl_ref[...]
    a = jnp.sum(dl[0:1] * dl[1:2], axis=-1, keepdims=True)
    b = jnp.sum(dl[2:3] * dl[3:4], axis=-1, keepdims=True)
    return jnp.exp(a) - jnp.exp(b) + lam_init


def _split_components(q):
    lane = lax.broadcasted_iota(jnp.int32, q.shape, 1)
    zero = jnp.zeros_like(q)
    return jnp.concatenate([jnp.where(lane < HD, q, zero), jnp.where(lane >= HD, q, zero)], axis=0)


def _head_norm(o, g, lam_init):
    ms = jnp.mean(o * o, axis=-1, keepdims=True)
    return o * lax.rsqrt(ms + RMS_EPS) * g * (1.0 - lam_init)


def _diff_prompt_kernel(dl_ref, g_ref, q_ref, k_ref, v_ref, o_ref, m_sc, l_sc, acc_sc, *, tq, lam_init):
    qi = pl.program_id(2)
    q2 = _split_components(q_ref[...])
    m_sc[...] = jnp.full(m_sc.shape, MASK_VALUE, F32)
    l_sc[...] = jnp.zeros(l_sc.shape, F32)
    acc_sc[...] = jnp.zeros(acc_sc.shape, F32)

    def step(kt, diagonal):
        off = pl.multiple_of(kt * tq, tq)
        k = k_ref[pl.ds(off, tq), :]
        v = v_ref[pl.ds(off, tq), :]
        s = _dot_nt(q2, k)
        if diagonal:
            r = lax.broadcasted_iota(jnp.int32, s.shape, 0)
            c = lax.broadcasted_iota(jnp.int32, s.shape, 1)
            r = jnp.where(r >= tq, r - tq, r)
            s = jnp.where(c <= r, s, MASK_VALUE)
        m_prev = m_sc[...]
        m_new = jnp.maximum(m_prev, jnp.max(s, axis=1, keepdims=True))
        alpha = jnp.exp(m_prev - m_new)
        p = jnp.exp(s - _rep_lanes(m_new, tq))
        l_sc[...] = alpha * l_sc[...] + jnp.sum(p, axis=1, keepdims=True)
        acc_sc[...] = alpha * acc_sc[...] + _dot(p.astype(BF16), v)
        m_sc[...] = m_new

    def body(kt, carry):
        step(kt, False)
        return carry

    lax.fori_loop(0, qi, body, 0)
    step(qi, True)
    o_all = acc_sc[...] / l_sc[...]
    lam = _lambda(dl_ref, lam_init)
    o = o_all[:tq] - lam * o_all[tq:]
    o_ref[...] = _head_norm(o, g_ref[...], lam_init)


def _diff_prompt(qa, kvb, diff_lambda, diff_norm_g, nb, seq_len, lam_init):
    tq = min(256, seq_len)
    nq = seq_len // tq
    kern = functools.partial(_diff_prompt_kernel, tq=tq, lam_init=lam_init)
    return pl.pallas_call(
        kern,
        grid=(nb, A_HEADS, nq),
        in_specs=[pl.BlockSpec((4, HD), lambda b, h, i: (0, 0)),
                  pl.BlockSpec((1, 2 * HD), lambda b, h, i: (0, 0)),
                  pl.BlockSpec((tq, 2 * HD), lambda b, h, i: (b * nq + i, h)),
                  pl.BlockSpec((seq_len, 2 * HD), lambda b, h, i: (b, h)),
                  pl.BlockSpec((seq_len, 2 * HD), lambda b, h, i: (b, A_HEADS + h))],
        out_specs=pl.BlockSpec((tq, 2 * HD), lambda b, h, i: (b * nq + i, h)),
        out_shape=jax.ShapeDtypeStruct((nb * seq_len, A_WIDTH), F32),
        scratch_shapes=[pltpu.VMEM((2 * tq, LANES), F32), pltpu.VMEM((2 * tq, LANES), F32),
                        pltpu.VMEM((2 * tq, 2 * HD), F32)],
        compiler_params=_params(("parallel", "parallel", "arbitrary")),
        name="diff_prompt",
    )(diff_lambda, diff_norm_g, qa, kvb, kvb)


def _diff_sample_kernel(pt_ref, dl_ref, g_ref, q_ref, new_ref, *rest, n_pages, past, lam_init):
    del pt_ref
    pages = rest[:n_pages]
    o_ref = rest[n_pages]
    kv_sc = rest[n_pages + 1]
    rows_per_token = 2 * A_HEADS
    for j in range(n_pages):
        for c in range(rows_per_token):
            blk = pages[j][pl.ds(c, PAGE_SIZE, stride=rows_per_token), :]
            kv_sc[j * PAGE_SIZE:(j + 1) * PAGE_SIZE, c * 2 * HD:(c + 1) * 2 * HD] = blk.astype(BF16)
    new8 = new_ref[...].astype(F32)
    pad = jnp.zeros((PAGE_SIZE - SUBLANES, 2 * A_WIDTH), F32)
    kv_sc[past:past + PAGE_SIZE, :] = jnp.concatenate([new8, pad], axis=0).astype(BF16)
    lk = past + PAGE_SIZE
    q = q_ref[...]
    lam = _lambda(dl_ref, lam_init)
    kpos = lax.broadcasted_iota(jnp.int32, (2 * SUBLANES, lk), 1)
    r = lax.broadcasted_iota(jnp.int32, (2 * SUBLANES, lk), 0)
    tok = jnp.where(r >= SUBLANES, r - SUBLANES, r)
    mask = kpos <= past + tok
    for h in range(A_HEADS):
        q2 = _split_components(q[:, h * 2 * HD:(h + 1) * 2 * HD])
        k = kv_sc[:, h * 2 * HD:(h + 1) * 2 * HD]
        v = kv_sc[:, A_WIDTH + h * 2 * HD:A_WIDTH + (h + 1) * 2 * HD]
        s = jnp.where(mask, _dot_nt(q2, k), MASK_VALUE)
        m = jnp.max(s, axis=1, keepdims=True)
        e = jnp.where(mask, jnp.exp(s - m), 0.0)
        p = e / jnp.maximum(jnp.sum(e, axis=1, keepdims=True), 1e-30)
        pv = _dot(p.astype(BF16), v)
        o = pv[:SUBLANES] - lam * pv[SUBLANES:]
        o_ref[:, h * 2 * HD:(h + 1) * 2 * HD] = _head_norm(o, g_ref[...], lam_init)


def _diff_sample(page_table, cache, layer, qa8, new8, diff_lambda, diff_norm_g, lam_init):
    nb, n_pages = page_table.shape
    past = n_pages * PAGE_SIZE
    kern = functools.partial(_diff_sample_kernel, n_pages=n_pages, past=past, lam_init=lam_init)

    def page_spec(j):
        return pl.BlockSpec((None, None, PAGE_SIZE * 2 * A_HEADS, 2 * HD), lambda b, pt: (layer, pt[b, j], 0, 0))

    grid_spec = pltpu.PrefetchScalarGridSpec(
        num_scalar_prefetch=1,
        grid=(nb,),
        in_specs=[pl.BlockSpec((4, HD), lambda b, pt: (0, 0)),
                  pl.BlockSpec((1, 2 * HD), lambda b, pt: (0, 0)),
                  pl.BlockSpec((None, SUBLANES, A_WIDTH), lambda b, pt: (b, 0, 0)),
                  pl.BlockSpec((None, SUBLANES, 2 * A_WIDTH), lambda b, pt: (b, 0, 0))]
                 + [page_spec(j) for j in range(n_pages)],
        out_specs=pl.BlockSpec((None, SUBLANES, A_WIDTH), lambda b, pt: (b, 0, 0)),
        scratch_shapes=[pltpu.VMEM((past + PAGE_SIZE, 2 * A_WIDTH), BF16)],
    )
    return pl.pallas_call(
        kern,
        grid_spec=grid_spec,
        out_shape=jax.ShapeDtypeStruct((nb, SUBLANES, A_WIDTH), F32),
        compiler_params=_params(("parallel",)),
        name="diff_sample",
    )(page_table, diff_lambda, diff_norm_g, qa8, new8, *([cache] * n_pages))


def _s5_kernel(u_ref, bbd_ref, cbd_ref, d_ref, are_ref, aim_ref, ldt_ref, wglu_ref, bglu_ref, h0re_ref, h0im_ref,
               y_ref, hre_ref, him_ref, buf, hs_re, hs_im, *, nb, tc):
    @pl.when(pl.program_id(0) == 0)
    def _():
        hs_re[...] = h0re_ref[...]
        hs_im[...] = h0im_ref[...]

    a_re = are_ref[...]
    a_im = aim_ref[...]
    dt = jnp.exp(ldt_ref[...])
    mag = jnp.exp(dt * a_re)
    ab_re = mag * jnp.cos(dt * a_im)
    ab_im = mag * jnp.sin(dt * a_im)
    den = a_re * a_re + a_im * a_im
    num_re = ab_re - 1.0
    c_re = (num_re * a_re + ab_im * a_im) / den
    c_im = (ab_im * a_re - num_re * a_im) / den
    full = (SUBLANES, GP)
    ab_re, ab_im, c_re, c_im = [jnp.broadcast_to(v, full) for v in (ab_re, ab_im, c_re, c_im)]

    u = u_ref[...]
    buf[...] = _dot(u.astype(BF16), bbd_ref[...])

    def group(rg, carry):
        r0 = pl.multiple_of(rg * SUBLANES, SUBLANES)

        def step(t, h):
            hr, hi = h
            row = pl.multiple_of(t * nb + r0, SUBLANES)
            br = buf[pl.ds(row, SUBLANES), 0:GP]
            bi = buf[pl.ds(row, SUBLANES), GP:2 * GP]
            nr = ab_re * hr - ab_im * hi + (c_re * br - c_im * bi)
            ni = ab_re * hi + ab_im * hr + (c_re * bi + c_im * br)
            buf[pl.ds(row, SUBLANES), 0:GP] = nr
            buf[pl.ds(row, SUBLANES), GP:2 * GP] = ni
            return nr, ni

        hr, hi = lax.fori_loop(0, tc, step, (hs_re[pl.ds(r0, SUBLANES), :], hs_im[pl.ds(r0, SUBLANES), :]))
        hs_re[pl.ds(r0, SUBLANES), :] = hr
        hs_im[pl.ds(r0, SUBLANES), :] = hi
        return carry

    lax.fori_loop(0, nb // SUBLANES, group, 0)
    y = _dot(buf[...].astype(BF16), cbd_ref[...]) + d_ref[...] * u
    y = jax.nn.gelu(y)
    z = _dot(y.astype(BF16), wglu_ref[...]) + bglu_ref[...]
    y_ref[...] = y * jax.nn.sigmoid(z)
    hre_ref[...] = hs_re[...]
    him_ref[...] = hs_im[...]


def _s5(u, sp, h0_re, h0_im, nb, tc):
    rows = u.shape[0]
    rc = tc * nb
    const = lambda i: (0, 0)
    kern = functools.partial(_s5_kernel, nb=nb, tc=tc)
    return pl.pallas_call(
        kern,
        grid=(rows // rc,),
        in_specs=[pl.BlockSpec((rc, S_WIDTH), lambda i: (i, 0)),
                  pl.BlockSpec((S_WIDTH, 2 * GP), const), pl.BlockSpec((2 * GP, S_WIDTH), const),
                  pl.BlockSpec((1, S_WIDTH), const),
                  pl.BlockSpec((1, GP), const), pl.BlockSpec((1, GP), const), pl.BlockSpec((1, GP), const),
                  pl.BlockSpec((S_WIDTH, S_WIDTH), const), pl.BlockSpec((1, S_WIDTH), const),
                  pl.BlockSpec((nb, GP), const), pl.BlockSpec((nb, GP), const)],
        out_specs=(pl.BlockSpec((rc, S_WIDTH), lambda i: (i, 0)),
                   pl.BlockSpec((nb, GP), const), pl.BlockSpec((nb, GP), const)),
        out_shape=(jax.ShapeDtypeStruct((rows, S_WIDTH), F32),
                   jax.ShapeDtypeStruct((nb, GP), F32), jax.ShapeDtypeStruct((nb, GP), F32)),
        scratch_shapes=[pltpu.VMEM((rc, 2 * GP), F32), pltpu.VMEM((nb, GP), F32), pltpu.VMEM((nb, GP), F32)],
        compiler_params=_params(("arbitrary",)),
        name="s5",
    )(u, sp["bbd"], sp["cbd"], sp["d"], sp["a_re"], sp["a_im"], sp["log_dt"], sp["w_glu"], sp["b_glu"], h0_re, h0_im)


def _masked_softmax(s, mask):
    s = jnp.where(mask, s, MASK_VALUE)
    m = jnp.max(s, axis=-1, keepdims=True)
    e = jnp.where(mask, jnp.exp(s - m), 0.0)
    return e / jnp.maximum(jnp.sum(e, axis=-1, keepdims=True), 1e-30)


def _stack_heads(q):
    return jnp.concatenate([q[:, g * HD:(g + 1) * HD] for g in range(C_HEADS)], axis=0)


def _compress(rows, wcol):
    n = rows.shape[0] // CMP_BLOCK
    return jnp.sum(rows.reshape(n, CMP_BLOCK, HD) * wcol[None], axis=1)


def _select_blocks(psum, pair, t, ns):
    imp = _dot(psum, pair, precision=HIGHEST)
    blk = lax.broadcasted_iota(jnp.int32, imp.shape, 1)
    cur = t // SEL_BLOCK
    valid = blk <= cur
    forced = jnp.where(blk == 0, 1.0, jnp.where(blk == cur, 1.0, jnp.where(blk == cur - 1, 1.0, 0.0)))
    score = jnp.where(valid, imp + FORCE_SCORE * forced, -1.0)
    score = jnp.where(blk < ns, score, -2.0)
    rank = jnp.zeros(imp.shape, F32)
    for s2 in range(ns):
        col = score[:, s2:s2 + 1]
        ahead = jnp.where(col > score, 1.0, jnp.where(col == score, jnp.where(blk > s2, 1.0, 0.0), 0.0))
        rank = rank + ahead
    chosen = jnp.where(rank < float(min(TOP_N, ns)), 1.0, 0.0)
    return jnp.where(blk < ns, chosen, 0.0)


def _combine(o_cmp, o_sel, o_win, gates, r):
    outs = []
    for g in range(C_HEADS):
        sl = slice(g * r, (g + 1) * r)
        outs.append(gates[:, 3 * g:3 * g + 1] * o_cmp[sl] + gates[:, 3 * g + 1:3 * g + 2] * o_sel[sl]
                    + gates[:, 3 * g + 2:3 * g + 3] * o_win[sl])
    return jnp.concatenate(outs, axis=1)


def _nsa_prompt_kernel(q_ref, nsa_ref, win_ref, gate_ref, wcol_ref, pair_ref, e_ref, o_ref,
                       kc_sc, vc_sc, selk_sc, selv_sc, wink_sc, winv_sc, selexp_sc, m_sc, l_sc, acc_sc, *, seq_len):
    i = pl.program_id(1)
    nc = seq_len // CMP_BLOCK
    ns = seq_len // SEL_BLOCK
    rows = C_HEADS * Q_TILE

    @pl.when(i == 0)
    def _():
        r = nsa_ref[...]
        kc_sc[...] = _compress(r[:, 0:HD], wcol_ref[0])
        vc_sc[...] = _compress(r[:, HD:2 * HD], wcol_ref[1])
        selk_sc[...] = r[:, 2 * HD:3 * HD].astype(BF16)
        selv_sc[...] = r[:, 3 * HD:4 * HD].astype(BF16)
        w = win_ref[...]
        wink_sc[...] = w[:, 0:HD].astype(BF16)
        winv_sc[...] = w[:, HD:2 * HD].astype(BF16)

    qs = _stack_heads(q_ref[...])
    qb = qs.astype(BF16)
    t1 = i * Q_TILE + lax.broadcasted_iota(jnp.int32, (Q_TILE, 1), 0)
    t4 = jnp.concatenate([t1] * C_HEADS, axis=0)

    sc = _dot_nt(qs, kc_sc[...], precision=HIGHEST)
    n_idx = lax.broadcasted_iota(jnp.int32, sc.shape, 1)
    p_cmp = _masked_softmax(sc, (n_idx + 1) * CMP_BLOCK - 1 <= t4)
    o_cmp = _dot(p_cmp, vc_sc[...], precision=HIGHEST)
    psum = p_cmp[0:Q_TILE]
    for g in range(1, C_HEADS):
        psum = psum + p_cmp[g * Q_TILE:(g + 1) * Q_TILE]
    chosen = _select_blocks(psum, pair_ref[...], jnp.broadcast_to(t1, (Q_TILE, LANES)), ns)
    selexp_sc[...] = _dot(chosen.astype(BF16), e_ref[...])

    def flash(k_sc, v_sc, first_tile, n_tiles, tk, mask_fn):
        m_sc[...] = jnp.full(m_sc.shape, MASK_VALUE, F32)
        l_sc[...] = jnp.zeros(l_sc.shape, F32)
        acc_sc[...] = jnp.zeros(acc_sc.shape, F32)

        def body(j, carry):
            off = pl.multiple_of((first_tile + j) * tk, tk)
            k = k_sc[pl.ds(off, tk), :]
            v = v_sc[pl.ds(off, tk), :]
            kpos = off + lax.broadcasted_iota(jnp.int32, (rows, tk), 1)
            mask = mask_fn(off, kpos)
            s = jnp.where(mask, _dot_nt(qb, k), MASK_VALUE)
            m_prev = m_sc[...]
            m_new = jnp.maximum(m_prev, jnp.max(s, axis=1, keepdims=True))
            alpha = jnp.exp(m_prev - m_new)
            p = jnp.where(mask, jnp.exp(s - _rep_lanes(m_new, tk)), 0.0)
            l_sc[...] = alpha * l_sc[...] + jnp.sum(p, axis=1, keepdims=True)
            acc_sc[...] = alpha[:, :HD] * acc_sc[...] + _dot(p.astype(BF16), v)
            m_sc[...] = m_new
            return carry

        lax.fori_loop(0, n_tiles, body, 0)
        return acc_sc[...] / jnp.maximum(l_sc[...], 1e-30)[:, :HD]

    def sel_mask(off, kpos):
        sx = selexp_sc[:, pl.ds(off, SEL_TILE)]
        sx = jnp.concatenate([sx] * C_HEADS, axis=0)
        return (sx > 0.5) & (kpos <= t4)

    def win_mask(off, kpos):
        return (kpos <= t4) & (kpos > t4 - WINDOW)

    o_sel = flash(selk_sc, selv_sc, 0, (i * Q_TILE) // SEL_TILE + 1, SEL_TILE, sel_mask)
    back = WINDOW // WIN_TILE
    first = jnp.maximum(i - back, 0)
    o_win = flash(wink_sc, winv_sc, first, i - first + 1, WIN_TILE, win_mask)
    o_ref[...] = _combine(o_cmp, o_sel, o_win, gate_ref[...], Q_TILE)


def _nsa_prompt(qc, nsa, win, gates, wcol, nb, seq_len):
    nqb = seq_len // Q_TILE
    nc = seq_len // CMP_BLOCK
    pair = (np.arange(nc)[:, None] // (SEL_BLOCK // CMP_BLOCK) == np.arange(LANES)[None, :]).astype(np.float32)
    expand = (np.arange(LANES)[:, None] == np.arange(seq_len)[None, :] // SEL_BLOCK).astype(np.float32)
    kern = functools.partial(_nsa_prompt_kernel, seq_len=seq_len)
    rows = C_HEADS * Q_TILE
    return pl.pallas_call(
        kern,
        grid=(nb, nqb),
        in_specs=[pl.BlockSpec((Q_TILE, C_WIDTH), lambda b, i: (b * nqb + i, 0)),
                  pl.BlockSpec((seq_len, 4 * HD), lambda b, i: (b, 0)),
                  pl.BlockSpec((seq_len, 2 * HD), lambda b, i: (b, 0)),
                  pl.BlockSpec((Q_TILE, LANES), lambda b, i: (b * nqb + i, 0)),
                  pl.BlockSpec((2, CMP_BLOCK, HD), lambda b, i: (0, 0, 0)),
                  pl.BlockSpec((nc, LANES), lambda b, i: (0, 0)),
                  pl.BlockSpec((LANES, seq_len), lambda b, i: (0, 0))],
        out_specs=pl.BlockSpec((Q_TILE, C_WIDTH), lambda b, i: (b * nqb + i, 0)),
        out_shape=jax.ShapeDtypeStruct((nb * seq_len, C_WIDTH), F32),
        scratch_shapes=[pltpu.VMEM((nc, HD), F32), pltpu.VMEM((nc, HD), F32),
                        pltpu.VMEM((seq_len, HD), BF16), pltpu.VMEM((seq_len, HD), BF16),
                        pltpu.VMEM((seq_len, HD), BF16), pltpu.VMEM((seq_len, HD), BF16),
                        pltpu.VMEM((Q_TILE, seq_len), F32),
                        pltpu.VMEM((rows, LANES), F32), pltpu.VMEM((rows, LANES), F32), pltpu.VMEM((rows, HD), F32)],
        compiler_params=_params(("parallel", "arbitrary")),
        name="nsa_prompt",
    )(qc, nsa, win, gates, wcol, jnp.asarray(pair), jnp.asarray(expand, dtype=BF16))


def _dot_split(a, b):
    hi = a.astype(BF16)
    lo = (a - hi.astype(F32)).astype(BF16)
    return _dot(hi, b) + _dot(lo, b)


def _nsa_sample_kernel(pt_ref, q_ref, new_ref, wint_ref, winnew_ref, gate_ref, wtile_ref, group_ref, pair_ref, e_ref,
                       *rest, n_pages, past, w_buf, dec_seq):
    del pt_ref
    pages = rest[:n_pages]
    o_ref, wks_ref = rest[n_pages:n_pages + 2]
    rows_sc, win_sc = rest[n_pages + 2:]
    lk = past + PAGE_SIZE
    nc = past // CMP_BLOCK
    ns = -(-(past + dec_seq) // SEL_BLOCK)
    for j in range(n_pages):
        rows_sc[:, j * PAGE_SIZE:(j + 1) * PAGE_SIZE] = pages[j][...]
    pad4 = jnp.zeros((PAGE_SIZE - SUBLANES, 4 * HD), F32)
    rows_sc[:, past:lk] = jnp.concatenate([new_ref[...], pad4], axis=0).T
    pad2 = jnp.zeros((PAGE_SIZE - SUBLANES, 2 * HD), F32)
    win_sc[:, 0:w_buf] = wint_ref[...]
    win_sc[:, w_buf:w_buf + PAGE_SIZE] = jnp.concatenate([winnew_ref[...], pad2], axis=0).T

    r = rows_sc[...]
    kc_t = _dot_split(r[0:HD, 0:past] * wtile_ref[0:1, :], group_ref[...])
    vc_t = _dot_split(r[HD:2 * HD, 0:past] * wtile_ref[1:2, :], group_ref[...])
    qs = _stack_heads(q_ref[...])
    qb = qs.astype(BF16)
    nrow = C_HEADS * SUBLANES
    t1 = past + lax.broadcasted_iota(jnp.int32, (SUBLANES, 1), 0)
    t4 = jnp.concatenate([t1] * C_HEADS, axis=0)

    sc = _dot(qs, kc_t, precision=HIGHEST)
    n_idx = lax.broadcasted_iota(jnp.int32, sc.shape, 1)
    p_cmp = _masked_softmax(sc, ((n_idx + 1) * CMP_BLOCK - 1 <= t4) & (n_idx < nc))
    o_cmp = _dot_nt(p_cmp, vc_t, precision=HIGHEST)
    psum = p_cmp[0:SUBLANES]
    for g in range(1, C_HEADS):
        psum = psum + p_cmp[g * SUBLANES:(g + 1) * SUBLANES]
    chosen = _select_blocks(psum, pair_ref[...], jnp.broadcast_to(t1, (SUBLANES, LANES)), ns)
    sx = _dot(chosen.astype(BF16), e_ref[...])
    sx = jnp.concatenate([sx] * C_HEADS, axis=0)
    kpos = lax.broadcasted_iota(jnp.int32, (nrow, lk), 1)
    smask = (sx > 0.5) & (kpos <= t4)
    p_sel = _masked_softmax(_dot(qb, r[2 * HD:3 * HD, :].astype(BF16)), smask)
    o_sel = _dot_nt(p_sel.astype(BF16), r[3 * HD:4 * HD, :].astype(BF16))

    w = win_sc[...]
    wpos = past - w_buf + lax.broadcasted_iota(jnp.int32, (nrow, w_buf + PAGE_SIZE), 1)
    wmask = (wpos <= t4) & (wpos > t4 - WINDOW) & (wpos >= 0)
    p_win = _masked_softmax(_dot(qb, w[0:HD, :].astype(BF16)), wmask)
    o_win = _dot_nt(p_win.astype(BF16), w[HD:2 * HD, :].astype(BF16))
    o_ref[...] = _combine(o_cmp, o_sel, o_win, gate_ref[...], SUBLANES)
    wks_ref[...] = pltpu.roll(w, w_buf + PAGE_SIZE - dec_seq, 1)[:, 0:w_buf]


def _nsa_sample(page_table, cache, layer, qc8, new8, win_buf, winnew8, gates8, cmp_w, dec_seq):
    nb, n_pages = page_table.shape
    past = n_pages * PAGE_SIZE
    w_buf = win_buf.shape[3]
    lk = past + PAGE_SIZE
    assert past % CMP_BLOCK == 0 and dec_seq < CMP_BLOCK and dec_seq <= SUBLANES
    nc = past // CMP_BLOCK
    assert nc <= LANES
    blocks = np.arange(LANES)[None, :]
    group = (np.arange(past)[:, None] // CMP_BLOCK == blocks).astype(np.float32)
    pair = ((np.arange(LANES)[:, None] // (SEL_BLOCK // CMP_BLOCK) == blocks)
            & (np.arange(LANES)[:, None] < nc)).astype(np.float32)
    expand = (np.arange(LANES)[:, None] == np.arange(lk)[None, :] // SEL_BLOCK).astype(np.float32)
    wtile = jnp.tile(cmp_w, (1, past // CMP_BLOCK))
    kern = functools.partial(_nsa_sample_kernel, n_pages=n_pages, past=past, w_buf=w_buf, dec_seq=dec_seq)

    def page_spec(j):
        return pl.BlockSpec((None, None, 4 * HD, PAGE_SIZE), lambda b, pt: (layer, pt[b, j], 0, 0))

    def per_b(width):
        return pl.BlockSpec((None, SUBLANES, width), lambda b, pt: (b, 0, 0))

    const = lambda b, pt: (0, 0)
    grid_spec = pltpu.PrefetchScalarGridSpec(
        num_scalar_prefetch=1,
        grid=(nb,),
        in_specs=[per_b(C_WIDTH), per_b(4 * HD),
                  pl.BlockSpec((None, None, 2 * HD, w_buf), lambda b, pt: (layer, b, 0, 0)),
                  per_b(2 * HD), per_b(LANES),
                  pl.BlockSpec((2, past), const), pl.BlockSpec((past, LANES), const),
                  pl.BlockSpec((LANES, LANES), const), pl.BlockSpec((LANES, lk), const)]
                 + [page_spec(j) for j in range(n_pages)],
        out_specs=(per_b(C_WIDTH), pl.BlockSpec((None, 2 * HD, w_buf), lambda b, pt: (b, 0, 0))),
        scratch_shapes=[pltpu.VMEM((4 * HD, lk), F32), pltpu.VMEM((2 * HD, w_buf + PAGE_SIZE), F32)],
    )
    return pl.pallas_call(
        kern,
        grid_spec=grid_spec,
        out_shape=(jax.ShapeDtypeStruct((nb, SUBLANES, C_WIDTH), F32), jax.ShapeDtypeStruct((nb, 2 * HD, w_buf), F32)),
        compiler_params=_params(("parallel",)),
        name="nsa_sample",
    )(page_table, qc8, new8, win_buf, winnew8, gates8, wtile, jnp.asarray(group, dtype=BF16), jnp.asarray(pair),
      jnp.asarray(expand, dtype=BF16), *([cache] * n_pages))


def _merge_kernel(x_ref, oa_ref, za_ref, os_ref, zs_ref, oc_ref, zc_ref, w_ref, gf_ref, y_ref, *, final):
    mix = jnp.concatenate([oa_ref[...] * jax.nn.silu(za_ref[...]),
                           os_ref[...] * jax.nn.silu(zs_ref[...]),
                           oc_ref[...] * jax.nn.silu(zc_ref[...])], axis=1)
    y = x_ref[...] + _dot(mix.astype(BF16), w_ref[...])
    if final:
        ms = jnp.mean(y * y, axis=-1, keepdims=True)
        y = y * lax.rsqrt(ms + RMS_EPS) * gf_ref[...]
    y_ref[...] = y


def _merge(x, oa, za, os_, zs, oc, zc, w_out, final_g, tm, seq_len, os_time_major, final):
    n = x.shape[0]
    nt = seq_len // tm
    row = lambda i: (i, 0)
    os_spec = pl.BlockSpec((tm, S_WIDTH), (lambda i: (i % nt, i // nt)) if os_time_major else row)
    return pl.pallas_call(
        functools.partial(_merge_kernel, final=final),
        grid=(n // tm,),
        in_specs=[pl.BlockSpec((tm, D_MODEL), row), pl.BlockSpec((tm, A_WIDTH), row), pl.BlockSpec((tm, A_WIDTH), row),
                  os_spec, pl.BlockSpec((tm, S_WIDTH), row),
                  pl.BlockSpec((tm, C_WIDTH), row), pl.BlockSpec((tm, C_WIDTH), row),
                  pl.BlockSpec((D_MODEL, D_MODEL), lambda i: (0, 0)), pl.BlockSpec((1, D_MODEL), lambda i: (0, 0))],
        out_specs=pl.BlockSpec((tm, D_MODEL), row),
        out_shape=jax.ShapeDtypeStruct((n, D_MODEL), F32),
        compiler_params=_params(("parallel",)),
        name="merge",
    )(x, oa, za, os_, zs, oc, zc, w_out, final_g)


def _rope_tables(pos):
    half = HD // 2
    inv = ROPE_THETA ** (-jnp.arange(half, dtype=F32) / half)
    ang = pos.astype(F32)[:, None] * inv[None, :]
    cos = jnp.tile(jnp.cos(ang), (1, LANES // half))
    sin = jnp.tile(jnp.sin(ang), (1, LANES // half))
    lane = jnp.arange(LANES)
    sin = jnp.where((lane % HD) < half, -sin, sin)
    is_k = lane < HD
    return jnp.concatenate([cos, sin, jnp.where(is_k, cos, 1.0), jnp.where(is_k, sin, 0.0)], axis=1)


def _layer_weights(layer, norm_g, w_in, w_out, diff_lambda, diff_norm_g, ssm_A_re, ssm_A_im, ssm_log_dt,
                   ssm_B_re, ssm_B_im, ssm_C_re, ssm_C_im, ssm_D, ssm_w_glu, ssm_b_glu, nsa_cmp_w):
    w = w_in[layer]
    old_gc = 4 * A_WIDTH + 2 * S_WIDTH + C_WIDTH + 6 * HD
    w_pad = jnp.concatenate([w[:, :old_gc], w[:, old_gc + N_GATES:], w[:, old_gc:old_gc + N_GATES],
                             jnp.zeros((D_MODEL, LANES - N_GATES), F32)], axis=1).astype(BF16)
    eye = jnp.eye(S_GROUPS, dtype=F32)
    b_re = jnp.einsum("gpc,gh->gchp", ssm_B_re[layer], eye).reshape(S_WIDTH, GP)
    b_im = jnp.einsum("gpc,gh->gchp", ssm_B_im[layer], eye).reshape(S_WIDTH, GP)
    c_re = jnp.einsum("gcp,gh->gphc", ssm_C_re[layer], eye).reshape(GP, S_WIDTH)
    c_im = jnp.einsum("gcp,gh->gphc", ssm_C_im[layer], eye).reshape(GP, S_WIDTH)
    s5 = {
        "bbd": jnp.concatenate([b_re, b_im], axis=1).astype(BF16),
        "cbd": jnp.concatenate([c_re, -c_im], axis=0).astype(BF16),
        "d": ssm_D[layer].reshape(1, S_WIDTH),
        "a_re": ssm_A_re[layer].reshape(1, GP),
        "a_im": ssm_A_im[layer].reshape(1, GP),
        "log_dt": jnp.broadcast_to(ssm_log_dt[layer][:, None], (S_GROUPS, S_STATE)).reshape(1, GP),
        "w_glu": ssm_w_glu[layer].astype(BF16),
        "b_glu": ssm_b_glu[layer].reshape(1, S_WIDTH),
    }
    return {
        "norm_g": norm_g[layer].reshape(1, D_MODEL),
        "w_in": w_pad,
        "w_out": w_out[layer].astype(BF16),
        "diff_lambda": diff_lambda[layer],
        "diff_norm_g": diff_norm_g[layer].reshape(1, 2 * HD),
        "s5": s5,
        "wcol": jnp.broadcast_to(nsa_cmp_w[layer][:, :, None], (2, CMP_BLOCK, HD)),
        "cmp_w": nsa_cmp_w[layer],
        "lam_init": 0.8 - 0.6 * math.exp(-0.3 * layer),
    }


def _prompt_layer(x, lw, tab, nb, seq_len, final_g, final):
    tm = min(256, seq_len)
    qa, dr, kvb, za, us, zs, qc, nsa, win, zc, gates, nsa_t, win_t = _inproj(
        x, lw["norm_g"], lw["w_in"], tab, seq_len, tm, True)
    oa = _diff_prompt(qa, kvb, lw["diff_lambda"], lw["diff_norm_g"], nb, seq_len, lw["lam_init"])
    zero = jnp.zeros((nb, GP), F32)
    os_, h_re, h_im = _s5(us.reshape(seq_len * nb, S_WIDTH), lw["s5"], zero, zero, nb, min(128, seq_len))
    os_ = os_.reshape(seq_len, nb * S_WIDTH)
    oc = _nsa_prompt(qc, nsa, win, gates, lw["wcol"], nb, seq_len)
    y = _merge(x, oa, za, os_, zs, oc, zc, lw["w_out"], final_g, tm, seq_len, True, final)
    return y, dr, nsa_t, win_t, h_re, h_im


def _pad_rows(a, nb, t):
    a = a.reshape(nb, t, a.shape[-1])
    return jnp.pad(a, ((0, 0), (0, SUBLANES - t), (0, 0)))


def _sample_layer(x, lw, tab, layer, nb, t, page_table, cache_diff, cache_nsa, cache_win, h0_re, h0_im, final_g, final):
    n = nb * t
    qa, dr, kvb, za, us, zs, qc, nsa, win, zc, gates = _inproj(x, lw["norm_g"], lw["w_in"], tab, t, n, False)
    oa8 = _diff_sample(page_table, cache_diff, layer, _pad_rows(qa, nb, t), _pad_rows(kvb, nb, t),
                       lw["diff_lambda"], lw["diff_norm_g"], lw["lam_init"])
    oa = oa8[:, :t].reshape(n, A_WIDTH)
    u_tb = us.reshape(nb, t, S_WIDTH).transpose(1, 0, 2).reshape(n, S_WIDTH)
    os_tb, h_re, h_im = _s5(u_tb, lw["s5"], h0_re, h0_im, nb, t)
    os_ = os_tb.reshape(t, nb, S_WIDTH).transpose(1, 0, 2).reshape(n, S_WIDTH)
    oc8, wks_t = _nsa_sample(page_table, cache_nsa, layer, _pad_rows(qc, nb, t), _pad_rows(nsa, nb, t), cache_win,
                             _pad_rows(win, nb, t), _pad_rows(gates, nb, t), lw["cmp_w"], t)
    oc = oc8[:, :t].reshape(n, C_WIDTH)
    y = _merge(x, oa, za, os_, zs, oc, zc, lw["w_out"], final_g, n, t, False, final)
    return y, dr, nsa, wks_t, h_re, h_im


def kernel(x_prompt, x_sample, cache_diff_kv, cache_nsa_kv, cache_win_kv, state_ssm_re, state_ssm_im, page_table, norm_g, w_in, w_out, diff_lambda, diff_norm_g, ssm_A_re, ssm_A_im, ssm_log_dt, ssm_B_re, ssm_B_im, ssm_C_re, ssm_C_im, ssm_D, ssm_w_glu, ssm_b_glu, nsa_cmp_w, final_norm_g):
    nbp, seq_len, _ = x_prompt.shape
    nbs, dec_seq, _ = x_sample.shape
    depth = w_in.shape[0]
    n_pool = cache_diff_kv.shape[1]
    past = page_table.shape[1] * PAGE_SIZE
    w_buf = cache_win_kv.shape[2]
    cache_diff = cache_diff_kv.reshape(depth, n_pool, PAGE_SIZE * 2 * A_HEADS, 2 * HD)
    cache_nsa = cache_nsa_kv.transpose(0, 1, 3, 4, 5, 2).reshape(depth, n_pool, 4 * HD, PAGE_SIZE)
    cache_win = cache_win_kv.transpose(0, 1, 3, 4, 5, 2).reshape(depth, nbs, 2 * HD, w_buf)
    tab_p = _rope_tables(jnp.arange(seq_len, dtype=jnp.int32))
    tab_s = jnp.tile(_rope_tables(past + jnp.arange(dec_seq, dtype=jnp.int32)), (nbs, 1))
    final_g = final_norm_g.reshape(1, D_MODEL)

    xp = x_prompt.reshape(nbp * seq_len, D_MODEL)
    xs = x_sample.reshape(nbs * dec_seq, D_MODEL)
    outs = [[] for _ in range(10)]
    w_keep = min(WINDOW, seq_len)
    for layer in range(depth):
        lw = _layer_weights(layer, norm_g, w_in, w_out, diff_lambda, diff_norm_g, ssm_A_re, ssm_A_im, ssm_log_dt,
                            ssm_B_re, ssm_B_im, ssm_C_re, ssm_C_im, ssm_D, ssm_w_glu, ssm_b_glu, nsa_cmp_w)
        final = layer == depth - 1
        xp, dr, nsa_t, win_t, h_re, h_im = _prompt_layer(xp, lw, tab_p, nbp, seq_len, final_g, final)
        outs[0].append(dr.reshape(nbp, seq_len, 2, A_HEADS, 2 * HD))
        outs[2].append(nsa_t)
        outs[4].append(win_t[:, :, seq_len - w_keep:])
        outs[6].append(h_re.reshape(nbp, S_GROUPS, S_STATE))
        outs[7].append(h_im.reshape(nbp, S_GROUPS, S_STATE))
        xs, dr, nsa, wks_t, h_re, h_im = _sample_layer(
            xs, lw, tab_s, layer, nbs, dec_seq, page_table, cache_diff, cache_nsa, cache_win,
            state_ssm_re[layer].reshape(nbs, GP), state_ssm_im[layer].reshape(nbs, GP), final_g, final)
        outs[1].append(dr.reshape(nbs, dec_seq, 2, A_HEADS, 2 * HD))
        outs[3].append(nsa.reshape(nbs, dec_seq, 4, 1, HD))
        outs[5].append(wks_t)
        outs[8].append(h_re.reshape(nbs, S_GROUPS, S_STATE))
        outs[9].append(h_im.reshape(nbs, S_GROUPS, S_STATE))
    stacked = [jnp.stack(o) for o in outs]

    def rows_first(a, n_rows):
        return a.reshape(a.shape[0], a.shape[1], n_rows, 1, HD, a.shape[3]).transpose(0, 1, 5, 2, 3, 4)

    stacked[2] = rows_first(stacked[2], 4)
    stacked[4] = rows_first(stacked[4], 2)
    stacked[5] = rows_first(stacked[5], 2)
    return (xp.reshape(nbp, seq_len, D_MODEL), xs.reshape(nbs, dec_seq, D_MODEL), *stacked)
```

```python
import functools
import math

import jax
import jax.numpy as jnp
import numpy as np
from jax import lax
from jax.experimental import pallas as pl
from jax.experimental.pallas import tpu as pltpu

F32 = jnp.float32
BF16 = jnp.bfloat16

D_MODEL = 1024
PAGE_SIZE = 128
HD = 64
A_WIDTH = D_MODEL // 2
A_HEADS = A_WIDTH // (2 * HD)
S_WIDTH = D_MODEL // 4
S_CH = 16
S_GROUPS = S_WIDTH // S_CH
S_STATE = 64
C_WIDTH = D_MODEL - A_WIDTH - S_WIDTH
C_HEADS = C_WIDTH // HD
CMP_BLOCK = 32
SEL_BLOCK = 64
TOP_N = 8
WINDOW = 512
ROPE_THETA = 10000.0
RMS_EPS = 1e-6
MASK_VALUE = -1e30
FORCE_SCORE = 1e4
SCALE = HD ** -0.5
GP = S_GROUPS * S_STATE

LANES = 128
SUBLANES = 8
VMEM_LIMIT_BYTES = 56 * 1024 * 1024

COL_QA = 0
COL_KA = COL_QA + A_WIDTH
COL_VA = COL_KA + A_WIDTH
COL_ZA = COL_VA + A_WIDTH
COL_US = COL_ZA + A_WIDTH
COL_ZS = COL_US + S_WIDTH
COL_QC = COL_ZS + S_WIDTH
COL_KVC = COL_QC + C_WIDTH
COL_GC = COL_KVC + 6 * HD
N_GATES = 3 * C_HEADS
COL_ZC = COL_GC + N_GATES
IN_WIDTH = COL_ZC + C_WIDTH
IN_PAD = -(-IN_WIDTH // LANES) * LANES
W_PREP_TILE = 256

Q_TILE = 128
SEL_TILE = 256
WIN_TILE = 128
HIGHEST = lax.Precision.HIGHEST


def _params(semantics):
    return pltpu.CompilerParams(dimension_semantics=semantics, vmem_limit_bytes=VMEM_LIMIT_BYTES)


def _dot_nt(a, b, precision=None):
    return lax.dot_general(a, b, (((1,), (1,)), ((), ())), preferred_element_type=F32, precision=precision)


def _dot(a, b, precision=None):
    return jnp.dot(a, b, preferred_element_type=F32, precision=precision)


def _dot_split(a, b):
    hi = a.astype(BF16)
    lo = (a - hi.astype(F32)).astype(BF16)
    return _dot(hi, b) + _dot(lo, b)


def _inproj_kernel(x_ref, g_ref, w_ref, tab_ref,
                   qa_ref, dr_ref, kvb_ref, za_ref, us_ref, zs_ref, qc_ref, nsa_ref, win_ref, zc_ref, gate_ref,
                   *transposed_refs):
    x = x_ref[...]
    ms = jnp.mean(x * x, axis=-1, keepdims=True)
    xn = (x * lax.rsqrt(ms + RMS_EPS) * g_ref[...]).astype(BF16)
    tm = x.shape[0]
    cos = tab_ref[:, 0:LANES]
    sin = tab_ref[:, LANES:2 * LANES]
    cos_kv = tab_ref[:, 2 * LANES:3 * LANES]
    sin_kv = tab_ref[:, 3 * LANES:4 * LANES]
    lane = lax.broadcasted_iota(jnp.int32, (tm, LANES), 1)
    first_half = (lane % HD) < (HD // 2)

    def rope(v, c, s):
        partner = jnp.where(first_half, pltpu.roll(v, LANES - HD // 2, 1), pltpu.roll(v, HD // 2, 1))
        return v * c + partner * s

    def proj(c0, width):
        return _dot(xn, w_ref[:, c0:c0 + width])

    def chunks(y):
        return [y[:, j * LANES:(j + 1) * LANES] for j in range(y.shape[1] // LANES)]

    y = proj(COL_QA, A_WIDTH)
    for j, c in enumerate(chunks(y)):
        qa_ref[:, j * LANES:(j + 1) * LANES] = (rope(c, cos, sin) * SCALE).astype(qa_ref.dtype)
    y = proj(COL_KA, A_WIDTH)
    for j, c in enumerate(chunks(y)):
        r = rope(c, cos, sin)
        dr_ref[pl.ds(j, tm, stride=2 * A_HEADS), :] = r
        kvb_ref[:, j * LANES:(j + 1) * LANES] = r.astype(BF16)
    y_va = proj(COL_VA, A_WIDTH)
    for j, c in enumerate(chunks(y_va)):
        dr_ref[pl.ds(A_HEADS + j, tm, stride=2 * A_HEADS), :] = c
    kvb_ref[:, A_WIDTH:2 * A_WIDTH] = y_va.astype(BF16)
    za_ref[...] = proj(COL_ZA, A_WIDTH)
    y = proj(COL_US, 2 * S_WIDTH)
    us_ref[...] = y[:, :S_WIDTH]
    zs_ref[...] = y[:, S_WIDTH:]
    y = proj(COL_QC, C_WIDTH)
    for j, c in enumerate(chunks(y)):
        qc_ref[:, j * LANES:(j + 1) * LANES] = rope(c, cos, sin) * SCALE
    y = proj(COL_KVC, 6 * HD)
    kv = [rope(c, cos_kv, sin_kv) for c in chunks(y)]
    nsa_ref[:, 0:LANES] = kv[0]
    nsa_ref[:, LANES:2 * LANES] = kv[1]
    win_ref[...] = kv[2]
    if transposed_refs:
        nsa_t_ref, win_t_ref, va_t_ref = transposed_refs
        nsa_t_ref[0:LANES, :] = kv[0].T
        nsa_t_ref[LANES:2 * LANES, :] = kv[1].T
        win_t_ref[...] = kv[2].T
        for j, c in enumerate(chunks(y_va)):
            va_t_ref[j * LANES:(j + 1) * LANES, :] = c.T.astype(BF16)
    y = proj(COL_GC, IN_PAD - COL_GC)
    gate_ref[...] = jax.nn.sigmoid(y[:, 0:LANES])
    zc_ref[...] = y[:, N_GATES:N_GATES + C_WIDTH]


def _prep_w_in_kernel(wt_ref, o_ref):
    for layer in range(o_ref.shape[0]):
        o_ref[layer] = wt_ref[:, layer, :].T.astype(BF16)


def _prep_w_in(w_in):
    depth = w_in.shape[0]
    wt = jnp.pad(w_in.transpose(2, 0, 1), ((0, IN_PAD - IN_WIDTH), (0, 0), (0, 0)))
    return pl.pallas_call(
        _prep_w_in_kernel,
        grid=(IN_PAD // W_PREP_TILE,),
        in_specs=[pl.BlockSpec((W_PREP_TILE, depth, D_MODEL), lambda i: (i, 0, 0))],
        out_specs=pl.BlockSpec((depth, D_MODEL, W_PREP_TILE), lambda i: (0, 0, i)),
        out_shape=jax.ShapeDtypeStruct((depth, D_MODEL, IN_PAD), BF16),
        compiler_params=_params(("parallel",)),
        name="prep_w_in",
    )(wt)


def _inproj(x, norm_g, w, layer, tab, seq_len, tm, prompt):
    n = x.shape[0]
    nt = tab.shape[0] // tm
    grid = (n // tm,)
    row = lambda i: (i, 0)
    nb = n // seq_len
    if prompt:
        us_shape = jax.ShapeDtypeStruct((seq_len, nb * S_WIDTH), F32)
        us_spec = pl.BlockSpec((tm, S_WIDTH), lambda i: (i % nt, i // nt))
    else:
        us_shape = jax.ShapeDtypeStruct((n, S_WIDTH), F32)
        us_spec = pl.BlockSpec((tm, S_WIDTH), row)
    rows_per_token = 2 * A_HEADS
    out_shape = (
        jax.ShapeDtypeStruct((n, A_WIDTH), F32 if prompt else BF16),
        jax.ShapeDtypeStruct((n * rows_per_token, 2 * HD), F32),
        jax.ShapeDtypeStruct((n, 2 * A_WIDTH), BF16),
        jax.ShapeDtypeStruct((n, A_WIDTH), F32),
        us_shape,
        jax.ShapeDtypeStruct((n, S_WIDTH), F32),
        jax.ShapeDtypeStruct((n, C_WIDTH), F32),
        jax.ShapeDtypeStruct((n, 4 * HD), F32),
        jax.ShapeDtypeStruct((n, 2 * HD), F32),
        jax.ShapeDtypeStruct((n, C_WIDTH), F32),
        jax.ShapeDtypeStruct((n, LANES), F32),
    )
    out_specs = (
        pl.BlockSpec((tm, A_WIDTH), row), pl.BlockSpec((tm * rows_per_token, 2 * HD), row),
        pl.BlockSpec((tm, 2 * A_WIDTH), row),
        pl.BlockSpec((tm, A_WIDTH), row), us_spec, pl.BlockSpec((tm, S_WIDTH), row),
        pl.BlockSpec((tm, C_WIDTH), row), pl.BlockSpec((tm, 4 * HD), row), pl.BlockSpec((tm, 2 * HD), row),
        pl.BlockSpec((tm, C_WIDTH), row), pl.BlockSpec((tm, LANES), row),
    )
    if prompt:
        out_shape += (jax.ShapeDtypeStruct((nb, 4 * HD, seq_len), F32), jax.ShapeDtypeStruct((nb, 2 * HD, seq_len), F32),
                      jax.ShapeDtypeStruct((nb, A_WIDTH, seq_len), BF16))
        out_specs += (pl.BlockSpec((None, 4 * HD, tm), lambda i: (i // nt, 0, i % nt)),
                      pl.BlockSpec((None, 2 * HD, tm), lambda i: (i // nt, 0, i % nt)),
                      pl.BlockSpec((None, A_WIDTH, tm), lambda i: (i // nt, 0, i % nt)))
    return pl.pallas_call(
        _inproj_kernel,
        grid=grid,
        in_specs=[pl.BlockSpec((tm, D_MODEL), row),
                  pl.BlockSpec((1, D_MODEL), lambda i: (0, 0)),
                  pl.BlockSpec((None, D_MODEL, IN_PAD), lambda i: (layer, 0, 0)),
                  pl.BlockSpec((tm, 4 * LANES), lambda i: (i % nt, 0))],
        out_specs=out_specs,
        out_shape=out_shape,
        compiler_params=_params(("parallel",)),
        name="inproj",
    )(x, norm_g, w, tab)


def _flash_t_init(m_sc, l_sc, acc_sc):
    m_sc[...] = jnp.full(m_sc.shape, MASK_VALUE, F32)
    l_sc[...] = jnp.zeros(l_sc.shape, F32)
    acc_sc[...] = jnp.zeros(acc_sc.shape, F32)


def _flash_t_update(s, v_t, m_sc, l_sc, acc_sc):
    m_prev = m_sc[...]
    m_new = jnp.maximum(m_prev, jnp.max(s, axis=0, keepdims=True))
    alpha = jnp.exp(m_prev - m_new)
    p = jnp.exp(s - m_new)
    l_sc[...] = alpha * l_sc[...] + jnp.sum(p, axis=0, keepdims=True)
    acc_sc[...] = alpha * acc_sc[...] + _dot(v_t, p.astype(BF16))
    m_sc[...] = m_new


def _flash_t(n_tiles, scores, values_t, mask_mid, mask_last, s_sc, m_sc, l_sc, acc_sc):
    _flash_t_init(m_sc, l_sc, acc_sc)
    s_sc[0] = scores(0)

    def body(j, carry):
        s_cur = s_sc[j % 2]
        s_sc[(j + 1) % 2] = scores(j + 1)
        if mask_mid is not None:
            s_cur = mask_mid(s_cur, j)
        _flash_t_update(s_cur, values_t(j), m_sc, l_sc, acc_sc)
        return carry

    last = n_tiles - 1
    lax.fori_loop(0, last, body, 0)
    _flash_t_update(mask_last(s_sc[last % 2], last), values_t(last), m_sc, l_sc, acc_sc)


def _lambda(dl_ref, lam_init):
    dl = dl_ref[...]
    a = jnp.sum(dl[0:1] * dl[1:2], axis=-1, keepdims=True)
    b = jnp.sum(dl[2:3] * dl[3:4], axis=-1, keepdims=True)
    return jnp.exp(a) - jnp.exp(b) + lam_init


def _split_components(q):
    lane = lax.broadcasted_iota(jnp.int32, q.shape, 1)
    zero = jnp.zeros_like(q)
    return jnp.concatenate([jnp.where(lane < HD, q, zero), jnp.where(lane >= HD, q, zero)], axis=0)


def _head_norm(o, g, lam_init):
    ms = jnp.mean(o * o, axis=-1, keepdims=True)
    return o * lax.rsqrt(ms + RMS_EPS) * g * (1.0 - lam_init)


def _diff_prompt_kernel(dl_ref, g_ref, q_ref, k_ref, vt_ref, o_ref, s_sc, m_sc, l_sc, acc_sc, *, tq, lam_init):
    qi = pl.program_id(2)
    q_t = q_ref[...].T
    row = lax.broadcasted_iota(jnp.int32, q_t.shape, 0)
    zero = jnp.zeros_like(q_t)
    q2_t = jnp.concatenate([jnp.where(row < HD, q_t, zero), jnp.where(row >= HD, q_t, zero)], axis=1).astype(BF16)

    def causal(s, j):
        key = lax.broadcasted_iota(jnp.int32, s.shape, 0)
        qry = lax.broadcasted_iota(jnp.int32, s.shape, 1)
        qry = jnp.where(qry >= tq, qry - tq, qry)
        return jnp.where(key <= qry, s, MASK_VALUE)

    def scores(j):
        return _dot(k_ref[pl.ds(pl.multiple_of(j * tq, tq), tq), :], q2_t)

    def values_t(j):
        return vt_ref[:, pl.ds(pl.multiple_of(j * tq, tq), tq)]

    _flash_t(qi + 1, scores, values_t, None, causal, s_sc, m_sc, l_sc, acc_sc)
    o_all = acc_sc[...] / l_sc[...]
    lam = _lambda(dl_ref, lam_init)
    o = (o_all[:, :tq] - lam * o_all[:, tq:]).T
    o_ref[...] = _head_norm(o, g_ref[...], lam_init)


def _diff_prompt(qa, kvb, va_t, diff_lambda, diff_norm_g, nb, seq_len, lam_init):
    tq = min(256, seq_len)
    nq = seq_len // tq
    kern = functools.partial(_diff_prompt_kernel, tq=tq, lam_init=lam_init)
    return pl.pallas_call(
        kern,
        grid=(nb, A_HEADS, nq),
        in_specs=[pl.BlockSpec((4, HD), lambda b, h, i: (0, 0)),
                  pl.BlockSpec((1, 2 * HD), lambda b, h, i: (0, 0)),
                  pl.BlockSpec((tq, 2 * HD), lambda b, h, i: (b * nq + i, h)),
                  pl.BlockSpec((seq_len, 2 * HD), lambda b, h, i: (b, h)),
                  pl.BlockSpec((None, 2 * HD, seq_len), lambda b, h, i: (b, h, 0))],
        out_specs=pl.BlockSpec((tq, 2 * HD), lambda b, h, i: (b * nq + i, h)),
        out_shape=jax.ShapeDtypeStruct((nb * seq_len, A_WIDTH), F32),
        scratch_shapes=[pltpu.VMEM((2, tq, 2 * tq), F32), pltpu.VMEM((1, 2 * tq), F32), pltpu.VMEM((1, 2 * tq), F32),
                        pltpu.VMEM((2 * HD, 2 * tq), F32)],
        compiler_params=_params(("parallel", "parallel", "arbitrary")),
        name="diff_prompt",
    )(diff_lambda, diff_norm_g, qa, kvb, va_t)


def _diff_sample_kernel(pt_ref, dl_ref, g_ref, q_ref, new_ref, *rest, n_pages, past, lam_init):
    del pt_ref
    pages = rest[:n_pages]
    o_ref = rest[n_pages]
    kv_sc = rest[n_pages + 1]
    rows_per_token = 2 * A_HEADS
    for j in range(n_pages):
        for c in range(rows_per_token):
            blk = pages[j][pl.ds(c, PAGE_SIZE, stride=rows_per_token), :]
            kv_sc[j * PAGE_SIZE:(j + 1) * PAGE_SIZE, c * 2 * HD:(c + 1) * 2 * HD] = blk.astype(BF16)
    new8 = new_ref[...].astype(F32)
    pad = jnp.zeros((PAGE_SIZE - SUBLANES, 2 * A_WIDTH), F32)
    kv_sc[past:past + PAGE_SIZE, :] = jnp.concatenate([new8, pad], axis=0).astype(BF16)
    lk = past + PAGE_SIZE
    q = q_ref[...]
    lam = _lambda(dl_ref, lam_init)
    kpos = lax.broadcasted_iota(jnp.int32, (2 * SUBLANES, lk), 1)
    r = lax.broadcasted_iota(jnp.int32, (2 * SUBLANES, lk), 0)
    tok = jnp.where(r >= SUBLANES, r - SUBLANES, r)
    mask = kpos <= past + tok
    for h in range(A_HEADS):
        q2 = _split_components(q[:, h * 2 * HD:(h + 1) * 2 * HD])
        k = kv_sc[:, h * 2 * HD:(h + 1) * 2 * HD]
        v = kv_sc[:, A_WIDTH + h * 2 * HD:A_WIDTH + (h + 1) * 2 * HD]
        s = jnp.where(mask, _dot_nt(q2, k), MASK_VALUE)
        m = jnp.max(s, axis=1, keepdims=True)
        e = jnp.where(mask, jnp.exp(s - m), 0.0)
        p = e / jnp.maximum(jnp.sum(e, axis=1, keepdims=True), 1e-30)
        pv = _dot(p.astype(BF16), v)
        o = pv[:SUBLANES] - lam * pv[SUBLANES:]
        o_ref[:, h * 2 * HD:(h + 1) * 2 * HD] = _head_norm(o, g_ref[...], lam_init)


def _diff_sample(page_table, cache, layer, qa8, new8, diff_lambda, diff_norm_g, lam_init):
    nb, n_pages = page_table.shape
    past = n_pages * PAGE_SIZE
    kern = functools.partial(_diff_sample_kernel, n_pages=n_pages, past=past, lam_init=lam_init)

    def page_spec(j):
        return pl.BlockSpec((None, None, PAGE_SIZE * 2 * A_HEADS, 2 * HD), lambda b, pt: (layer, pt[b, j], 0, 0))

    grid_spec = pltpu.PrefetchScalarGridSpec(
        num_scalar_prefetch=1,
        grid=(nb,),
        in_specs=[pl.BlockSpec((4, HD), lambda b, pt: (0, 0)),
                  pl.BlockSpec((1, 2 * HD), lambda b, pt: (0, 0)),
                  pl.BlockSpec((None, SUBLANES, A_WIDTH), lambda b, pt: (b, 0, 0)),
                  pl.BlockSpec((None, SUBLANES, 2 * A_WIDTH), lambda b, pt: (b, 0, 0))]
                 + [page_spec(j) for j in range(n_pages)],
        out_specs=pl.BlockSpec((None, SUBLANES, A_WIDTH), lambda b, pt: (b, 0, 0)),
        scratch_shapes=[pltpu.VMEM((past + PAGE_SIZE, 2 * A_WIDTH), BF16)],
    )
    return pl.pallas_call(
        kern,
        grid_spec=grid_spec,
        out_shape=jax.ShapeDtypeStruct((nb, SUBLANES, A_WIDTH), F32),
        compiler_params=_params(("parallel",)),
        name="diff_sample",
    )(page_table, diff_lambda, diff_norm_g, qa8, new8, *([cache] * n_pages))


def _s5_kernel(u_ref, bbd_ref, cbd_ref, d_ref, are_ref, aim_ref, ldt_ref, wglu_ref, bglu_ref, h0re_ref, h0im_ref,
               y_ref, hre_ref, him_ref, buf, hs_re, hs_im, *, nb, tc):
    @pl.when(pl.program_id(0) == 0)
    def _():
        hs_re[...] = h0re_ref[...]
        hs_im[...] = h0im_ref[...]

    a_re = are_ref[...]
    a_im = aim_ref[...]
    dt = jnp.exp(ldt_ref[...])
    mag = jnp.exp(dt * a_re)
    ab_re = mag * jnp.cos(dt * a_im)
    ab_im = mag * jnp.sin(dt * a_im)
    den = a_re * a_re + a_im * a_im
    num_re = ab_re - 1.0
    c_re = (num_re * a_re + ab_im * a_im) / den
    c_im = (ab_im * a_re - num_re * a_im) / den
    full = (SUBLANES, GP)
    ab_re, ab_im, c_re, c_im = [jnp.broadcast_to(v, full) for v in (ab_re, ab_im, c_re, c_im)]

    u = u_ref[...]
    buf[...] = _dot(u.astype(BF16), bbd_ref[...])

    def group(rg, carry):
        r0 = pl.multiple_of(rg * SUBLANES, SUBLANES)

        def step(t, h):
            hr, hi = h
            row = pl.multiple_of(t * nb + r0, SUBLANES)
            br = buf[pl.ds(row, SUBLANES), 0:GP]
            bi = buf[pl.ds(row, SUBLANES), GP:2 * GP]
            nr = ab_re * hr - ab_im * hi + (c_re * br - c_im * bi)
            ni = ab_re * hi + ab_im * hr + (c_re * bi + c_im * br)
            buf[pl.ds(row, SUBLANES), 0:GP] = nr
            buf[pl.ds(row, SUBLANES), GP:2 * GP] = ni
            return nr, ni

        hr, hi = lax.fori_loop(0, tc, step, (hs_re[pl.ds(r0, SUBLANES), :], hs_im[pl.ds(r0, SUBLANES), :]))
        hs_re[pl.ds(r0, SUBLANES), :] = hr
        hs_im[pl.ds(r0, SUBLANES), :] = hi
        return carry

    lax.fori_loop(0, nb // SUBLANES, group, 0)
    y = _dot(buf[...].astype(BF16), cbd_ref[...]) + d_ref[...] * u
    y = jax.nn.gelu(y)
    z = _dot(y.astype(BF16), wglu_ref[...]) + bglu_ref[...]
    y_ref[...] = y * jax.nn.sigmoid(z)
    hre_ref[...] = hs_re[...]
    him_ref[...] = hs_im[...]


def _s5(u, sp, h0_re, h0_im, nb, tc):
    rows = u.shape[0]
    rc = tc * nb
    const = lambda i: (0, 0)
    kern = functools.partial(_s5_kernel, nb=nb, tc=tc)
    return pl.pallas_call(
        kern,
        grid=(rows // rc,),
        in_specs=[pl.BlockSpec((rc, S_WIDTH), lambda i: (i, 0)),
                  pl.BlockSpec((S_WIDTH, 2 * GP), const), pl.BlockSpec((2 * GP, S_WIDTH), const),
                  pl.BlockSpec((1, S_WIDTH), const),
                  pl.BlockSpec((1, GP), const), pl.BlockSpec((1, GP), const), pl.BlockSpec((1, GP), const),
                  pl.BlockSpec((S_WIDTH, S_WIDTH), const), pl.BlockSpec((1, S_WIDTH), const),
                  pl.BlockSpec((nb, GP), const), pl.BlockSpec((nb, GP), const)],
        out_specs=(pl.BlockSpec((rc, S_WIDTH), lambda i: (i, 0)),
                   pl.BlockSpec((nb, GP), const), pl.BlockSpec((nb, GP), const)),
        out_shape=(jax.ShapeDtypeStruct((rows, S_WIDTH), F32),
                   jax.ShapeDtypeStruct((nb, GP), F32), jax.ShapeDtypeStruct((nb, GP), F32)),
        scratch_shapes=[pltpu.VMEM((rc, 2 * GP), F32), pltpu.VMEM((nb, GP), F32), pltpu.VMEM((nb, GP), F32)],
        compiler_params=_params(("arbitrary",)),
        name="s5",
    )(u, sp["bbd"], sp["cbd"], sp["d"], sp["a_re"], sp["a_im"], sp["log_dt"], sp["w_glu"], sp["b_glu"], h0_re, h0_im)


def _masked_softmax(s, mask):
    s = jnp.where(mask, s, MASK_VALUE)
    m = jnp.max(s, axis=-1, keepdims=True)
    e = jnp.where(mask, jnp.exp(s - m), 0.0)
    return e / jnp.maximum(jnp.sum(e, axis=-1, keepdims=True), 1e-30)


def _stack_heads(q):
    return jnp.concatenate([q[:, g * HD:(g + 1) * HD] for g in range(C_HEADS)], axis=0)


def _compress(rows, wcol):
    n = rows.shape[0] // CMP_BLOCK
    return jnp.sum(rows.reshape(n, CMP_BLOCK, HD) * wcol[None], axis=1)


def _select_blocks(imp, t, ns, axis):
    blk = lax.broadcasted_iota(jnp.int32, imp.shape, axis)
    cur = t // SEL_BLOCK
    valid = blk <= cur
    forced = jnp.where(blk == 0, 1.0, jnp.where(blk == cur, 1.0, jnp.where(blk == cur - 1, 1.0, 0.0)))
    score = jnp.where(valid, imp + FORCE_SCORE * forced, -1.0)
    score = jnp.where(blk < ns, score, -2.0)
    rank = jnp.zeros(imp.shape, F32)
    for s2 in range(ns):
        other = score[s2:s2 + 1, :] if axis == 0 else score[:, s2:s2 + 1]
        ahead = jnp.where(other > score, 1.0, jnp.where(other == score, jnp.where(blk > s2, 1.0, 0.0), 0.0))
        rank = rank + ahead
    chosen = jnp.where(rank < float(min(TOP_N, ns)), 1.0, 0.0)
    return jnp.where(blk < ns, chosen, 0.0)


def _combine(o_cmp, o_sel, o_win, gates, r):
    outs = []
    for g in range(C_HEADS):
        sl = slice(g * r, (g + 1) * r)
        outs.append(gates[:, 3 * g:3 * g + 1] * o_cmp[sl] + gates[:, 3 * g + 1:3 * g + 2] * o_sel[sl]
                    + gates[:, 3 * g + 2:3 * g + 3] * o_win[sl])
    return jnp.concatenate(outs, axis=1)


def _nsa_prompt_kernel(q_ref, nsa_ref, nsat_ref, win_ref, wint_ref, gate_ref, wcol_ref, wtile_ref, group_ref,
                       pairt_ref, et_ref, o_ref,
                       kc_sc, vct_sc, selk_sc, selvt_sc, wink_sc, winvt_sc, s_sel_sc, s_win_sc,
                       m_sc, l_sc, acc_sc, m2_sc, l2_sc, acc2_sc, *, seq_len):
    i = pl.program_id(1)
    nc = seq_len // CMP_BLOCK
    ns = seq_len // SEL_BLOCK
    nsp = min(LANES, -(-ns // SUBLANES) * SUBLANES)

    @pl.when(i == 0)
    def _():
        r = nsa_ref[...]
        kc_sc[...] = jnp.zeros(kc_sc.shape, F32)
        kc_sc[0:nc, :] = _compress(r[:, 0:HD], wcol_ref[0])
        vct_sc[...] = _dot_split(nsat_ref[HD:2 * HD, :] * wtile_ref[1:2, :], group_ref[...])
        selk_sc[...] = r[:, 2 * HD:3 * HD].astype(BF16)
        selvt_sc[...] = nsat_ref[3 * HD:4 * HD, :].astype(BF16)
        wink_sc[...] = win_ref[:, 0:HD].astype(BF16)
        winvt_sc[...] = wint_ref[HD:2 * HD, :].astype(BF16)

    q_t = q_ref[...].T
    qs_t = jnp.concatenate([q_t[g * HD:(g + 1) * HD] for g in range(C_HEADS)], axis=1)
    qb_t = qs_t.astype(BF16)
    t_q = i * Q_TILE + lax.broadcasted_iota(jnp.int32, (1, Q_TILE), 1)
    t_row = jnp.concatenate([t_q] * C_HEADS, axis=1)

    sc = _dot(kc_sc[...], qs_t, precision=HIGHEST)
    n_idx = lax.broadcasted_iota(jnp.int32, sc.shape, 0)
    cmask = ((n_idx + 1) * CMP_BLOCK - 1 <= t_row) & (n_idx < nc)
    sc = jnp.where(cmask, sc, MASK_VALUE)
    e = jnp.where(cmask, jnp.exp(sc - jnp.max(sc, axis=0, keepdims=True)), 0.0)
    p_t = e / jnp.maximum(jnp.sum(e, axis=0, keepdims=True), 1e-30)
    o_cmp_t = _dot(vct_sc[...], p_t, precision=HIGHEST)
    psum_t = p_t[:, 0:Q_TILE]
    for g in range(1, C_HEADS):
        psum_t = psum_t + p_t[:, g * Q_TILE:(g + 1) * Q_TILE]
    imp_t = _dot(pairt_ref[...], psum_t, precision=HIGHEST)
    chosen_t = _select_blocks(imp_t[0:nsp], t_q, ns, 0)
    if nsp < LANES:
        chosen_t = jnp.concatenate([chosen_t, jnp.zeros((LANES - nsp, Q_TILE), F32)], axis=0)
    chosen_b = chosen_t.astype(BF16)

    def sel_off(j):
        return pl.multiple_of(j * SEL_TILE, SEL_TILE)

    def sel_bias(sv, j):
        x = _dot(et_ref[pl.ds(sel_off(j), SEL_TILE), :], chosen_b)
        bias = (x - 1.0) * (-MASK_VALUE)
        return sv + jnp.concatenate([bias] * C_HEADS, axis=1)

    def sel_last(sv, j):
        key = j * SEL_TILE + lax.broadcasted_iota(jnp.int32, sv.shape, 0)
        return jnp.where(key <= t_row, sel_bias(sv, j), MASK_VALUE)

    _flash_t((i * Q_TILE) // SEL_TILE + 1,
             lambda j: _dot(selk_sc[pl.ds(sel_off(j), SEL_TILE), :], qb_t),
             lambda j: selvt_sc[:, pl.ds(sel_off(j), SEL_TILE)],
             sel_bias, sel_last, s_sel_sc, m_sc, l_sc, acc_sc)
    o_sel_t = acc_sc[...] / jnp.maximum(l_sc[...], 1e-30)

    first = jnp.maximum(i - WINDOW // WIN_TILE, 0)

    def win_off(j):
        return pl.multiple_of((first + j) * WIN_TILE, WIN_TILE)

    def win_mask(sv, j):
        key = (first + j) * WIN_TILE + lax.broadcasted_iota(jnp.int32, sv.shape, 0)
        return jnp.where((key <= t_row) & (key > t_row - WINDOW), sv, MASK_VALUE)

    _flash_t(i - first + 1,
             lambda j: _dot(wink_sc[pl.ds(win_off(j), WIN_TILE), :], qb_t),
             lambda j: winvt_sc[:, pl.ds(win_off(j), WIN_TILE)],
             win_mask, win_mask, s_win_sc, m2_sc, l2_sc, acc2_sc)
    o_win_t = acc2_sc[...] / jnp.maximum(l2_sc[...], 1e-30)

    g_t = gate_ref[...].T
    outs = []
    for g in range(C_HEADS):
        cs = slice(g * Q_TILE, (g + 1) * Q_TILE)
        outs.append(g_t[3 * g:3 * g + 1] * o_cmp_t[:, cs] + g_t[3 * g + 1:3 * g + 2] * o_sel_t[:, cs]
                    + g_t[3 * g + 2:3 * g + 3] * o_win_t[:, cs])
    o_ref[...] = jnp.concatenate(outs, axis=0).T


def _nsa_prompt(qc, nsa, nsa_t, win, win_t, gates, wcol, cmp_w, nb, seq_len):
    nqb = seq_len // Q_TILE
    nc = seq_len // CMP_BLOCK
    assert nc <= LANES and seq_len % SEL_TILE == 0
    blocks = np.arange(LANES)
    pair_t = ((blocks[None, :] // (SEL_BLOCK // CMP_BLOCK) == blocks[:, None]) & (blocks[None, :] < nc)).astype(np.float32)
    expand_t = (np.arange(seq_len)[:, None] // SEL_BLOCK == blocks[None, :]).astype(np.float32)
    group = (np.arange(seq_len)[:, None] // CMP_BLOCK == blocks[None, :]).astype(np.float32)
    wtile = jnp.tile(cmp_w, (1, nc))
    kern = functools.partial(_nsa_prompt_kernel, seq_len=seq_len)
    n = C_HEADS * Q_TILE
    per_b = lambda b, i: (b, 0)
    per_b3 = lambda b, i: (b, 0, 0)
    const = lambda b, i: (0, 0)
    return pl.pallas_call(
        kern,
        grid=(nb, nqb),
        in_specs=[pl.BlockSpec((Q_TILE, C_WIDTH), lambda b, i: (b * nqb + i, 0)),
                  pl.BlockSpec((seq_len, 4 * HD), per_b), pl.BlockSpec((None, 4 * HD, seq_len), per_b3),
                  pl.BlockSpec((seq_len, 2 * HD), per_b), pl.BlockSpec((None, 2 * HD, seq_len), per_b3),
                  pl.BlockSpec((Q_TILE, LANES), lambda b, i: (b * nqb + i, 0)),
                  pl.BlockSpec((2, CMP_BLOCK, HD), lambda b, i: (0, 0, 0)),
                  pl.BlockSpec((2, seq_len), const), pl.BlockSpec((seq_len, LANES), const),
                  pl.BlockSpec((LANES, LANES), const), pl.BlockSpec((seq_len, LANES), const)],
        out_specs=pl.BlockSpec((Q_TILE, C_WIDTH), lambda b, i: (b * nqb + i, 0)),
        out_shape=jax.ShapeDtypeStruct((nb * seq_len, C_WIDTH), F32),
        scratch_shapes=[pltpu.VMEM((LANES, HD), F32), pltpu.VMEM((HD, LANES), F32),
                        pltpu.VMEM((seq_len, HD), BF16), pltpu.VMEM((HD, seq_len), BF16),
                        pltpu.VMEM((seq_len, HD), BF16), pltpu.VMEM((HD, seq_len), BF16),
                        pltpu.VMEM((2, SEL_TILE, n), F32), pltpu.VMEM((2, WIN_TILE, n), F32),
                        pltpu.VMEM((1, n), F32), pltpu.VMEM((1, n), F32), pltpu.VMEM((HD, n), F32),
                        pltpu.VMEM((1, n), F32), pltpu.VMEM((1, n), F32), pltpu.VMEM((HD, n), F32)],
        compiler_params=_params(("parallel", "arbitrary")),
        name="nsa_prompt",
    )(qc, nsa, nsa_t, win, win_t, gates, wcol, wtile, jnp.asarray(group, dtype=BF16), jnp.asarray(pair_t),
      jnp.asarray(expand_t, dtype=BF16))


def _nsa_sample_kernel(pt_ref, q_ref, new_ref, wint_ref, winnew_ref, gate_ref, wtile_ref, group_ref, pair_ref, e_ref,
                       *rest, n_pages, past, w_buf, dec_seq):
    del pt_ref
    pages = rest[:n_pages]
    o_ref, wks_ref = rest[n_pages:n_pages + 2]
    rows_sc, win_sc = rest[n_pages + 2:]
    lk = past + PAGE_SIZE
    nc = past // CMP_BLOCK
    ns = -(-(past + dec_seq) // SEL_BLOCK)
    for j in range(n_pages):
        rows_sc[:, j * PAGE_SIZE:(j + 1) * PAGE_SIZE] = pages[j][...]
    pad4 = jnp.zeros((PAGE_SIZE - SUBLANES, 4 * HD), F32)
    rows_sc[:, past:lk] = jnp.concatenate([new_ref[...], pad4], axis=0).T
    pad2 = jnp.zeros((PAGE_SIZE - SUBLANES, 2 * HD), F32)
    win_sc[:, 0:w_buf] = wint_ref[...]
    win_sc[:, w_buf:w_buf + PAGE_SIZE] = jnp.concatenate([winnew_ref[...], pad2], axis=0).T

    r = rows_sc[...]
    kc_t = _dot_split(r[0:HD, 0:past] * wtile_ref[0:1, :], group_ref[...])
    vc_t = _dot_split(r[HD:2 * HD, 0:past] * wtile_ref[1:2, :], group_ref[...])
    qs = _stack_heads(q_ref[...])
    qb = qs.astype(BF16)
    nrow = C_HEADS * SUBLANES
    t1 = past + lax.broadcasted_iota(jnp.int32, (SUBLANES, 1), 0)
    t4 = jnp.concatenate([t1] * C_HEADS, axis=0)

    sc = _dot(qs, kc_t, precision=HIGHEST)
    n_idx = lax.broadcasted_iota(jnp.int32, sc.shape, 1)
    p_cmp = _masked_softmax(sc, ((n_idx + 1) * CMP_BLOCK - 1 <= t4) & (n_idx < nc))
    o_cmp = _dot_nt(p_cmp, vc_t, precision=HIGHEST)
    psum = p_cmp[0:SUBLANES]
    for g in range(1, C_HEADS):
        psum = psum + p_cmp[g * SUBLANES:(g + 1) * SUBLANES]
    imp = _dot(psum, pair_ref[...], precision=HIGHEST)
    chosen = _select_blocks(imp, t1, ns, 1)
    sx = _dot(chosen.astype(BF16), e_ref[...])
    sx = jnp.concatenate([sx] * C_HEADS, axis=0)
    kpos = lax.broadcasted_iota(jnp.int32, (nrow, lk), 1)
    smask = (sx > 0.5) & (kpos <= t4)
    p_sel = _masked_softmax(_dot(qb, r[2 * HD:3 * HD, :].astype(BF16)), smask)
    o_sel = _dot_nt(p_sel.astype(BF16), r[3 * HD:4 * HD, :].astype(BF16))

    w = win_sc[...]
    wpos = past - w_buf + lax.broadcasted_iota(jnp.int32, (nrow, w_buf + PAGE_SIZE), 1)
    wmask = (wpos <= t4) & (wpos > t4 - WINDOW) & (wpos >= 0)
    p_win = _masked_softmax(_dot(qb, w[0:HD, :].astype(BF16)), wmask)
    o_win = _dot_nt(p_win.astype(BF16), w[HD:2 * HD, :].astype(BF16))
    o_ref[...] = _combine(o_cmp, o_sel, o_win, gate_ref[...], SUBLANES)
    wks_ref[...] = pltpu.roll(w, w_buf + PAGE_SIZE - dec_seq, 1)[:, 0:w_buf]


def _nsa_sample(page_table, cache, layer, qc8, new8, win_buf, winnew8, gates8, cmp_w, dec_seq):
    nb, n_pages = page_table.shape
    past = n_pages * PAGE_SIZE
    w_buf = win_buf.shape[3]
    lk = past + PAGE_SIZE
    assert past % CMP_BLOCK == 0 and dec_seq < CMP_BLOCK and dec_seq <= SUBLANES
    nc = past // CMP_BLOCK
    assert nc <= LANES
    blocks = np.arange(LANES)[None, :]
    group = (np.arange(past)[:, None] // CMP_BLOCK == blocks).astype(np.float32)
    pair = ((np.arange(LANES)[:, None] // (SEL_BLOCK // CMP_BLOCK) == blocks)
            & (np.arange(LANES)[:, None] < nc)).astype(np.float32)
    expand = (np.arange(LANES)[:, None] == np.arange(lk)[None, :] // SEL_BLOCK).astype(np.float32)
    wtile = jnp.tile(cmp_w, (1, past // CMP_BLOCK))
    kern = functools.partial(_nsa_sample_kernel, n_pages=n_pages, past=past, w_buf=w_buf, dec_seq=dec_seq)

    def page_spec(j):
        return pl.BlockSpec((None, None, 4 * HD, PAGE_SIZE), lambda b, pt: (layer, pt[b, j], 0, 0))

    def per_b(width):
        return pl.BlockSpec((None, SUBLANES, width), lambda b, pt: (b, 0, 0))

    const = lambda b, pt: (0, 0)
    grid_spec = pltpu.PrefetchScalarGridSpec(
        num_scalar_prefetch=1,
        grid=(nb,),
        in_specs=[per_b(C_WIDTH), per_b(4 * HD),
                  pl.BlockSpec((None, None, 2 * HD, w_buf), lambda b, pt: (layer, b, 0, 0)),
                  per_b(2 * HD), per_b(LANES),
                  pl.BlockSpec((2, past), const), pl.BlockSpec((past, LANES), const),
                  pl.BlockSpec((LANES, LANES), const), pl.BlockSpec((LANES, lk), const)]
                 + [page_spec(j) for j in range(n_pages)],
        out_specs=(per_b(C_WIDTH), pl.BlockSpec((None, 2 * HD, w_buf), lambda b, pt: (b, 0, 0))),
        scratch_shapes=[pltpu.VMEM((4 * HD, lk), F32), pltpu.VMEM((2 * HD, w_buf + PAGE_SIZE), F32)],
    )
    return pl.pallas_call(
        kern,
        grid_spec=grid_spec,
        out_shape=(jax.ShapeDtypeStruct((nb, SUBLANES, C_WIDTH), F32), jax.ShapeDtypeStruct((nb, 2 * HD, w_buf), F32)),
        compiler_params=_params(("parallel",)),
        name="nsa_sample",
    )(page_table, qc8, new8, win_buf, winnew8, gates8, wtile, jnp.asarray(group, dtype=BF16), jnp.asarray(pair),
      jnp.asarray(expand, dtype=BF16), *([cache] * n_pages))


def _merge_kernel(x_ref, oa_ref, za_ref, os_ref, zs_ref, oc_ref, zc_ref, w_ref, gf_ref, y_ref, *, final):
    mix = jnp.concatenate([oa_ref[...] * jax.nn.silu(za_ref[...]),
                           os_ref[...] * jax.nn.silu(zs_ref[...]),
                           oc_ref[...] * jax.nn.silu(zc_ref[...])], axis=1)
    y = x_ref[...] + _dot(mix.astype(BF16), w_ref[...])
    if final:
        ms = jnp.mean(y * y, axis=-1, keepdims=True)
        y = y * lax.rsqrt(ms + RMS_EPS) * gf_ref[...]
    y_ref[...] = y


def _merge(x, oa, za, os_, zs, oc, zc, w_out, final_g, tm, seq_len, os_time_major, final):
    n = x.shape[0]
    nt = seq_len // tm
    row = lambda i: (i, 0)
    os_spec = pl.BlockSpec((tm, S_WIDTH), (lambda i: (i % nt, i // nt)) if os_time_major else row)
    return pl.pallas_call(
        functools.partial(_merge_kernel, final=final),
        grid=(n // tm,),
        in_specs=[pl.BlockSpec((tm, D_MODEL), row), pl.BlockSpec((tm, A_WIDTH), row), pl.BlockSpec((tm, A_WIDTH), row),
                  os_spec, pl.BlockSpec((tm, S_WIDTH), row),
                  pl.BlockSpec((tm, C_WIDTH), row), pl.BlockSpec((tm, C_WIDTH), row),
                  pl.BlockSpec((D_MODEL, D_MODEL), lambda i: (0, 0)), pl.BlockSpec((1, D_MODEL), lambda i: (0, 0))],
        out_specs=pl.BlockSpec((tm, D_MODEL), row),
        out_shape=jax.ShapeDtypeStruct((n, D_MODEL), F32),
        compiler_params=_params(("parallel",)),
        name="merge",
    )(x, oa, za, os_, zs, oc, zc, w_out, final_g)


def _rope_tables(pos):
    half = HD // 2
    inv = ROPE_THETA ** (-jnp.arange(half, dtype=F32) / half)
    ang = pos.astype(F32)[:, None] * inv[None, :]
    cos = jnp.tile(jnp.cos(ang), (1, LANES // half))
    sin = jnp.tile(jnp.sin(ang), (1, LANES // half))
    lane = jnp.arange(LANES)
    sin = jnp.where((lane % HD) < half, -sin, sin)
    is_k = lane < HD
    return jnp.concatenate([cos, sin, jnp.where(is_k, cos, 1.0), jnp.where(is_k, sin, 0.0)], axis=1)


def _layer_weights(layer, norm_g, w_in, w_out, diff_lambda, diff_norm_g, ssm_A_re, ssm_A_im, ssm_log_dt,
                   ssm_B_re, ssm_B_im, ssm_C_re, ssm_C_im, ssm_D, ssm_w_glu, ssm_b_glu, nsa_cmp_w):
    eye = jnp.eye(S_GROUPS, dtype=F32)
    b_re = jnp.einsum("gpc,gh->gchp", ssm_B_re[layer], eye).reshape(S_WIDTH, GP)
    b_im = jnp.einsum("gpc,gh->gchp", ssm_B_im[layer], eye).reshape(S_WIDTH, GP)
    c_re = jnp.einsum("gcp,gh->gphc", ssm_C_re[layer], eye).reshape(GP, S_WIDTH)
    c_im = jnp.einsum("gcp,gh->gphc", ssm_C_im[layer], eye).reshape(GP, S_WIDTH)
    s5 = {
        "bbd": jnp.concatenate([b_re, b_im], axis=1).astype(BF16),
        "cbd": jnp.concatenate([c_re, -c_im], axis=0).astype(BF16),
        "d": ssm_D[layer].reshape(1, S_WIDTH),
        "a_re": ssm_A_re[layer].reshape(1, GP),
        "a_im": ssm_A_im[layer].reshape(1, GP),
        "log_dt": jnp.broadcast_to(ssm_log_dt[layer][:, None], (S_GROUPS, S_STATE)).reshape(1, GP),
        "w_glu": ssm_w_glu[layer].astype(BF16),
        "b_glu": ssm_b_glu[layer].reshape(1, S_WIDTH),
    }
    return {
        "norm_g": norm_g[layer].reshape(1, D_MODEL),
        "w_in": w_in,
        "layer": layer,
        "w_out": w_out[layer].astype(BF16),
        "diff_lambda": diff_lambda[layer],
        "diff_norm_g": diff_norm_g[layer].reshape(1, 2 * HD),
        "s5": s5,
        "wcol": jnp.broadcast_to(nsa_cmp_w[layer][:, :, None], (2, CMP_BLOCK, HD)),
        "cmp_w": nsa_cmp_w[layer],
        "lam_init": 0.8 - 0.6 * math.exp(-0.3 * layer),
    }


def _prompt_layer(x, lw, tab, nb, seq_len, final_g, final):
    tm = min(256, seq_len)
    qa, dr, kvb, za, us, zs, qc, nsa, win, zc, gates, nsa_t, win_t, va_t = _inproj(
        x, lw["norm_g"], lw["w_in"], lw["layer"], tab, seq_len, tm, True)
    oa = _diff_prompt(qa, kvb, va_t, lw["diff_lambda"], lw["diff_norm_g"], nb, seq_len, lw["lam_init"])
    zero = jnp.zeros((nb, GP), F32)
    os_, h_re, h_im = _s5(us.reshape(seq_len * nb, S_WIDTH), lw["s5"], zero, zero, nb, min(128, seq_len))
    os_ = os_.reshape(seq_len, nb * S_WIDTH)
    oc = _nsa_prompt(qc, nsa, nsa_t, win, win_t, gates, lw["wcol"], lw["cmp_w"], nb, seq_len)
    y = _merge(x, oa, za, os_, zs, oc, zc, lw["w_out"], final_g, tm, seq_len, True, final)
    return y, dr, nsa_t, win_t, h_re, h_im


def _pad_rows(a, nb, t):
    a = a.reshape(nb, t, a.shape[-1])
    return jnp.pad(a, ((0, 0), (0, SUBLANES - t), (0, 0)))


def _sample_layer(x, lw, tab, layer, nb, t, page_table, cache_diff, cache_nsa, cache_win, h0_re, h0_im, final_g, final):
    n = nb * t
    qa, dr, kvb, za, us, zs, qc, nsa, win, zc, gates = _inproj(
        x, lw["norm_g"], lw["w_in"], lw["layer"], tab, t, n, False)
    oa8 = _diff_sample(page_table, cache_diff, layer, _pad_rows(qa, nb, t), _pad_rows(kvb, nb, t),
                       lw["diff_lambda"], lw["diff_norm_g"], lw["lam_init"])
    oa = oa8[:, :t].reshape(n, A_WIDTH)
    u_tb = us.reshape(nb, t, S_WIDTH).transpose(1, 0, 2).reshape(n, S_WIDTH)
    os_tb, h_re, h_im = _s5(u_tb, lw["s5"], h0_re, h0_im, nb, t)
    os_ = os_tb.reshape(t, nb, S_WIDTH).transpose(1, 0, 2).reshape(n, S_WIDTH)
    oc8, wks_t = _nsa_sample(page_table, cache_nsa, layer, _pad_rows(qc, nb, t), _pad_rows(nsa, nb, t), cache_win,
                             _pad_rows(win, nb, t), _pad_rows(gates, nb, t), lw["cmp_w"], t)
    oc = oc8[:, :t].reshape(n, C_WIDTH)
    y = _merge(x, oa, za, os_, zs, oc, zc, lw["w_out"], final_g, n, t, False, final)
    return y, dr, nsa, wks_t, h_re, h_im


def kernel(x_prompt, x_sample, cache_diff_kv, cache_nsa_kv, cache_win_kv, state_ssm_re, state_ssm_im, page_table, norm_g, w_in, w_out, diff_lambda, diff_norm_g, ssm_A_re, ssm_A_im, ssm_log_dt, ssm_B_re, ssm_B_im, ssm_C_re, ssm_C_im, ssm_D, ssm_w_glu, ssm_b_glu, nsa_cmp_w, final_norm_g):
    nbp, seq_len, _ = x_prompt.shape
    nbs, dec_seq, _ = x_sample.shape
    depth = w_in.shape[0]
    n_pool = cache_diff_kv.shape[1]
    past = page_table.shape[1] * PAGE_SIZE
    w_buf = cache_win_kv.shape[2]
    cache_diff = cache_diff_kv.reshape(depth, n_pool, PAGE_SIZE * 2 * A_HEADS, 2 * HD)
    cache_nsa = cache_nsa_kv.transpose(0, 1, 3, 4, 5, 2).reshape(depth, n_pool, 4 * HD, PAGE_SIZE)
    cache_win = cache_win_kv.transpose(0, 1, 3, 4, 5, 2).reshape(depth, nbs, 2 * HD, w_buf)
    tab_p = _rope_tables(jnp.arange(seq_len, dtype=jnp.int32))
    tab_s = jnp.tile(_rope_tables(past + jnp.arange(dec_seq, dtype=jnp.int32)), (nbs, 1))
    final_g = final_norm_g.reshape(1, D_MODEL)

    xp = x_prompt.reshape(nbp * seq_len, D_MODEL)
    xs = x_sample.reshape(nbs * dec_seq, D_MODEL)
    outs = [[] for _ in range(10)]
    w_keep = min(WINDOW, seq_len)
    w_in_b = _prep_w_in(w_in)
    for layer in range(depth):
        lw = _layer_weights(layer, norm_g, w_in_b, w_out, diff_lambda, diff_norm_g, ssm_A_re, ssm_A_im, ssm_log_dt,
                            ssm_B_re, ssm_B_im, ssm_C_re, ssm_C_im, ssm_D, ssm_w_glu, ssm_b_glu, nsa_cmp_w)
        final = layer == depth - 1
        xp, dr, nsa_t, win_t, h_re, h_im = _prompt_layer(xp, lw, tab_p, nbp, seq_len, final_g, final)
        outs[0].append(dr.reshape(nbp, seq_len, 2, A_HEADS, 2 * HD))
        outs[2].append(nsa_t)
        outs[4].append(win_t[:, :, seq_len - w_keep:])
        outs[6].append(h_re.reshape(nbp, S_GROUPS, S_STATE))
        outs[7].append(h_im.reshape(nbp, S_GROUPS, S_STATE))
        xs, dr, nsa, wks_t, h_re, h_im = _sample_layer(
            xs, lw, tab_s, layer, nbs, dec_seq, page_table, cache_diff, cache_nsa, cache_win,
            state_ssm_re[layer].reshape(nbs, GP), state_ssm_im[layer].reshape(nbs, GP), final_g, final)
        outs[1].append(dr.reshape(nbs, dec_seq, 2, A_HEADS, 2 * HD))
        outs[3].append(nsa.reshape(nbs, dec_seq, 4, 1, HD))
        outs[5].append(wks_t)
        outs[8].append(h_re.reshape(nbs, S_GROUPS, S_STATE))
        outs[9].append(h_im.reshape(nbs, S_GROUPS, S_STATE))
    stacked = [jnp.stack(o) for o in outs]

    def rows_first(a, n_rows):
        return a.reshape(a.shape[0], a.shape[1], n_rows, 1, HD, a.shape[3]).transpose(0, 1, 5, 2, 3, 4)

    stacked[2] = rows_first(stacked[2], 4)
    stacked[4] = rows_first(stacked[4], 2)
    stacked[5] = rows_first(stacked[5], 2)
    return (xp.reshape(nbp, seq_len, D_MODEL), xs.reshape(nbs, dec_seq, D_MODEL), *stacked)
```

```python
import functools
import math

import jax
import jax.numpy as jnp
import numpy as np
from jax import lax
from jax.experimental import pallas as pl
from jax.experimental.pallas import tpu as pltpu

F32 = jnp.float32
BF16 = jnp.bfloat16

D_MODEL = 1024
PAGE_SIZE = 128
HD = 64
A_WIDTH = D_MODEL // 2
A_HEADS = A_WIDTH // (2 * HD)
S_WIDTH = D_MODEL // 4
S_CH = 16
S_GROUPS = S_WIDTH // S_CH
S_STATE = 64
C_WIDTH = D_MODEL - A_WIDTH - S_WIDTH
C_HEADS = C_WIDTH // HD
CMP_BLOCK = 32
SEL_BLOCK = 64
TOP_N = 8
WINDOW = 512
ROPE_THETA = 10000.0
RMS_EPS = 1e-6
MASK_VALUE = -1e30
FORCE_SCORE = 1e4
SCALE = HD ** -0.5
GP = S_GROUPS * S_STATE

LANES = 128
SUBLANES = 8
VMEM_LIMIT_BYTES = 56 * 1024 * 1024

COL_QA = 0
COL_KA = COL_QA + A_WIDTH
COL_VA = COL_KA + A_WIDTH
COL_ZA = COL_VA + A_WIDTH
COL_US = COL_ZA + A_WIDTH
COL_ZS = COL_US + S_WIDTH
COL_QC = COL_ZS + S_WIDTH
COL_KVC = COL_QC + C_WIDTH
COL_GC = COL_KVC + 6 * HD
N_GATES = 3 * C_HEADS
COL_ZC = COL_GC + N_GATES
IN_WIDTH = COL_ZC + C_WIDTH
IN_PAD = -(-IN_WIDTH // LANES) * LANES
W_PREP_TILE = 256

DIFF_Q_TILE = 512
DIFF_K_TILE = 256
Q_TILE = 256
SEL_TILE = 256
WIN_TILE = 128
HIGHEST = lax.Precision.HIGHEST


def _params(semantics):
    return pltpu.CompilerParams(dimension_semantics=semantics, vmem_limit_bytes=VMEM_LIMIT_BYTES)


def _dot_nt(a, b, precision=None):
    return lax.dot_general(a, b, (((1,), (1,)), ((), ())), preferred_element_type=F32, precision=precision)


def _dot(a, b, precision=None):
    return jnp.dot(a, b, preferred_element_type=F32, precision=precision)


def _dot_split(a, b):
    hi = a.astype(BF16)
    lo = (a - hi.astype(F32)).astype(BF16)
    return _dot(hi, b) + _dot(lo, b)


def _inproj_kernel(x_ref, g_ref, w_ref, tab_ref,
                   qa_ref, dr_ref, kvb_ref, za_ref, us_ref, zs_ref, qc_ref, nsa_ref, win_ref, zc_ref, gate_ref,
                   *transposed_refs):
    x = x_ref[...]
    ms = jnp.mean(x * x, axis=-1, keepdims=True)
    xn = (x * lax.rsqrt(ms + RMS_EPS) * g_ref[...]).astype(BF16)
    tm = x.shape[0]
    cos = tab_ref[:, 0:LANES]
    sin = tab_ref[:, LANES:2 * LANES]
    cos_kv = tab_ref[:, 2 * LANES:3 * LANES]
    sin_kv = tab_ref[:, 3 * LANES:4 * LANES]
    lane = lax.broadcasted_iota(jnp.int32, (tm, LANES), 1)
    first_half = (lane % HD) < (HD // 2)

    def rope(v, c, s):
        partner = jnp.where(first_half, pltpu.roll(v, LANES - HD // 2, 1), pltpu.roll(v, HD // 2, 1))
        return v * c + partner * s

    def proj(c0, width):
        return _dot(xn, w_ref[:, c0:c0 + width])

    def chunks(y):
        return [y[:, j * LANES:(j + 1) * LANES] for j in range(y.shape[1] // LANES)]

    y = proj(COL_QA, A_WIDTH)
    for j, c in enumerate(chunks(y)):
        qa_ref[:, j * LANES:(j + 1) * LANES] = (rope(c, cos, sin) * SCALE).astype(qa_ref.dtype)
    y = proj(COL_KA, A_WIDTH)
    for j, c in enumerate(chunks(y)):
        r = rope(c, cos, sin)
        dr_ref[pl.ds(j, tm, stride=2 * A_HEADS), :] = r
        kvb_ref[:, j * LANES:(j + 1) * LANES] = r.astype(BF16)
    y_va = proj(COL_VA, A_WIDTH)
    for j, c in enumerate(chunks(y_va)):
        dr_ref[pl.ds(A_HEADS + j, tm, stride=2 * A_HEADS), :] = c
    kvb_ref[:, A_WIDTH:2 * A_WIDTH] = y_va.astype(BF16)
    za_ref[...] = proj(COL_ZA, A_WIDTH)
    y = proj(COL_US, 2 * S_WIDTH)
    us_ref[...] = y[:, :S_WIDTH]
    zs_ref[...] = y[:, S_WIDTH:]
    y = proj(COL_QC, C_WIDTH)
    for j, c in enumerate(chunks(y)):
        qc_ref[:, j * LANES:(j + 1) * LANES] = rope(c, cos, sin) * SCALE
    y = proj(COL_KVC, 6 * HD)
    kv = [rope(c, cos_kv, sin_kv) for c in chunks(y)]
    nsa_ref[:, 0:LANES] = kv[0]
    nsa_ref[:, LANES:2 * LANES] = kv[1]
    win_ref[...] = kv[2]
    if transposed_refs:
        nsa_t_ref, win_t_ref, va_t_ref = transposed_refs
        nsa_t_ref[0:LANES, :] = kv[0].T
        nsa_t_ref[LANES:2 * LANES, :] = kv[1].T
        win_t_ref[...] = kv[2].T
        for j, c in enumerate(chunks(y_va)):
            va_t_ref[j * LANES:(j + 1) * LANES, :] = c.T.astype(BF16)
    y = proj(COL_GC, IN_PAD - COL_GC)
    gate_ref[...] = jax.nn.sigmoid(y[:, 0:LANES])
    zc_ref[...] = y[:, N_GATES:N_GATES + C_WIDTH]


def _prep_w_in_kernel(wt_ref, o_ref):
    for layer in range(o_ref.shape[0]):
        o_ref[layer] = wt_ref[:, layer, :].T.astype(BF16)


def _prep_w_in(w_in):
    depth = w_in.shape[0]
    wt = jnp.pad(w_in.transpose(2, 0, 1), ((0, IN_PAD - IN_WIDTH), (0, 0), (0, 0)))
    return pl.pallas_call(
        _prep_w_in_kernel,
        grid=(IN_PAD // W_PREP_TILE,),
        in_specs=[pl.BlockSpec((W_PREP_TILE, depth, D_MODEL), lambda i: (i, 0, 0))],
        out_specs=pl.BlockSpec((depth, D_MODEL, W_PREP_TILE), lambda i: (0, 0, i)),
        out_shape=jax.ShapeDtypeStruct((depth, D_MODEL, IN_PAD), BF16),
        compiler_params=_params(("parallel",)),
        name="prep_w_in",
    )(wt)


def _inproj(x, norm_g, w, layer, tab, seq_len, tm, prompt):
    n = x.shape[0]
    nt = tab.shape[0] // tm
    grid = (n // tm,)
    row = lambda i: (i, 0)
    nb = n // seq_len
    if prompt:
        us_shape = jax.ShapeDtypeStruct((seq_len, nb * S_WIDTH), F32)
        us_spec = pl.BlockSpec((tm, S_WIDTH), lambda i: (i % nt, i // nt))
    else:
        us_shape = jax.ShapeDtypeStruct((n, S_WIDTH), F32)
        us_spec = pl.BlockSpec((tm, S_WIDTH), row)
    rows_per_token = 2 * A_HEADS
    out_shape = (
        jax.ShapeDtypeStruct((n, A_WIDTH), F32 if prompt else BF16),
        jax.ShapeDtypeStruct((n * rows_per_token, 2 * HD), F32),
        jax.ShapeDtypeStruct((n, 2 * A_WIDTH), BF16),
        jax.ShapeDtypeStruct((n, A_WIDTH), F32),
        us_shape,
        jax.ShapeDtypeStruct((n, S_WIDTH), F32),
        jax.ShapeDtypeStruct((n, C_WIDTH), F32),
        jax.ShapeDtypeStruct((n, 4 * HD), F32),
        jax.ShapeDtypeStruct((n, 2 * HD), F32),
        jax.ShapeDtypeStruct((n, C_WIDTH), F32),
        jax.ShapeDtypeStruct((n, LANES), F32),
    )
    out_specs = (
        pl.BlockSpec((tm, A_WIDTH), row), pl.BlockSpec((tm * rows_per_token, 2 * HD), row),
        pl.BlockSpec((tm, 2 * A_WIDTH), row),
        pl.BlockSpec((tm, A_WIDTH), row), us_spec, pl.BlockSpec((tm, S_WIDTH), row),
        pl.BlockSpec((tm, C_WIDTH), row), pl.BlockSpec((tm, 4 * HD), row), pl.BlockSpec((tm, 2 * HD), row),
        pl.BlockSpec((tm, C_WIDTH), row), pl.BlockSpec((tm, LANES), row),
    )
    if prompt:
        out_shape += (jax.ShapeDtypeStruct((nb, 4 * HD, seq_len), F32), jax.ShapeDtypeStruct((nb, 2 * HD, seq_len), F32),
                      jax.ShapeDtypeStruct((nb, A_WIDTH, seq_len), BF16))
        out_specs += (pl.BlockSpec((None, 4 * HD, tm), lambda i: (i // nt, 0, i % nt)),
                      pl.BlockSpec((None, 2 * HD, tm), lambda i: (i // nt, 0, i % nt)),
                      pl.BlockSpec((None, A_WIDTH, tm), lambda i: (i // nt, 0, i % nt)))
    return pl.pallas_call(
        _inproj_kernel,
        grid=grid,
        in_specs=[pl.BlockSpec((tm, D_MODEL), row),
                  pl.BlockSpec((1, D_MODEL), lambda i: (0, 0)),
                  pl.BlockSpec((None, D_MODEL, IN_PAD), lambda i: (layer, 0, 0)),
                  pl.BlockSpec((tm, 4 * LANES), lambda i: (i % nt, 0))],
        out_specs=out_specs,
        out_shape=out_shape,
        compiler_params=_params(("parallel",)),
        name="inproj",
    )(x, norm_g, w, tab)


def _flash_t_init(m_sc, l_sc, acc_sc):
    m_sc[...] = jnp.full(m_sc.shape, MASK_VALUE, F32)
    l_sc[...] = jnp.zeros(l_sc.shape, F32)
    acc_sc[...] = jnp.zeros(acc_sc.shape, F32)


def _flash_t_update(s, v_t, m_sc, l_sc, acc_sc):
    m_prev = m_sc[...]
    m_new = jnp.maximum(m_prev, jnp.max(s, axis=0, keepdims=True))
    alpha = jnp.exp(m_prev - m_new)
    p = jnp.exp(s - m_new)
    l_sc[...] = alpha * l_sc[...] + jnp.sum(p, axis=0, keepdims=True)
    acc_sc[...] = alpha * acc_sc[...] + _dot(v_t, p.astype(BF16))
    m_sc[...] = m_new


def _flash_t(n_plain, n_tail, scores, values_t, mask_plain, mask_tail, s_sc, m_sc, l_sc, acc_sc, unroll=1):
    _flash_t_init(m_sc, l_sc, acc_sc)
    s_sc[0] = scores(0)

    def body(jj, carry):
        for u in range(unroll):
            j = jj * unroll + u
            s_cur = s_sc[j % 2]
            s_sc[(j + 1) % 2] = scores(j + 1)
            if mask_plain is not None:
                s_cur = mask_plain(s_cur, j)
            _flash_t_update(s_cur, values_t(j), m_sc, l_sc, acc_sc)
        return carry

    lax.fori_loop(0, n_plain // unroll, body, 0)
    for t in range(n_tail):
        j = n_plain + t
        s_cur = s_sc[j % 2]
        if t + 1 < n_tail:
            s_sc[(j + 1) % 2] = scores(j + 1)
        _flash_t_update(mask_tail(s_cur, j), values_t(j), m_sc, l_sc, acc_sc)


def _lambda(dl_ref, lam_init):
    dl = dl_ref[...]
    a = jnp.sum(dl[0:1] * dl[1:2], axis=-1, keepdims=True)
    b = jnp.sum(dl[2:3] * dl[3:4], axis=-1, keepdims=True)
    return jnp.exp(a) - jnp.exp(b) + lam_init


def _split_components(q):
    lane = lax.broadcasted_iota(jnp.int32, q.shape, 1)
    zero = jnp.zeros_like(q)
    return jnp.concatenate([jnp.where(lane < HD, q, zero), jnp.where(lane >= HD, q, zero)], axis=0)


def _head_norm(o, g, lam_init):
    ms = jnp.mean(o * o, axis=-1, keepdims=True)
    return o * lax.rsqrt(ms + RMS_EPS) * g * (1.0 - lam_init)


def _diff_prompt_kernel(dl_ref, g_ref, q_ref, k_ref, vt_ref, o_ref, s_sc, m_sc, l_sc, acc_sc, *, tq, tk, lam_init):
    qi = pl.program_id(2)
    q_t = q_ref[...].T
    row = lax.broadcasted_iota(jnp.int32, q_t.shape, 0)
    zero = jnp.zeros_like(q_t)
    q2_t = jnp.concatenate([jnp.where(row < HD, q_t, zero), jnp.where(row >= HD, q_t, zero)], axis=1).astype(BF16)

    def causal(s, j):
        key = j * tk + lax.broadcasted_iota(jnp.int32, s.shape, 0)
        qry = lax.broadcasted_iota(jnp.int32, s.shape, 1)
        qry = qi * tq + jnp.where(qry >= tq, qry - tq, qry)
        return jnp.where(key <= qry, s, MASK_VALUE)

    def scores(j):
        return _dot(k_ref[pl.ds(pl.multiple_of(j * tk, tk), tk), :], q2_t)

    def values_t(j):
        return vt_ref[:, pl.ds(pl.multiple_of(j * tk, tk), tk)]

    _flash_t(qi * (tq // tk), tq // tk, scores, values_t, None, causal, s_sc, m_sc, l_sc, acc_sc, unroll=tq // tk)
    o_all = acc_sc[...] / l_sc[...]
    lam = _lambda(dl_ref, lam_init)
    o = (o_all[:, :tq] - lam * o_all[:, tq:]).T
    o_ref[...] = _head_norm(o, g_ref[...], lam_init)


def _diff_prompt(qa, kvb, va_t, diff_lambda, diff_norm_g, nb, seq_len, lam_init):
    tq = min(DIFF_Q_TILE, seq_len)
    tk = min(DIFF_K_TILE, seq_len)
    nq = seq_len // tq
    kern = functools.partial(_diff_prompt_kernel, tq=tq, tk=tk, lam_init=lam_init)
    return pl.pallas_call(
        kern,
        grid=(nb, A_HEADS, nq),
        in_specs=[pl.BlockSpec((4, HD), lambda b, h, i: (0, 0)),
                  pl.BlockSpec((1, 2 * HD), lambda b, h, i: (0, 0)),
                  pl.BlockSpec((tq, 2 * HD), lambda b, h, i: (b * nq + i, h)),
                  pl.BlockSpec((seq_len, 2 * HD), lambda b, h, i: (b, h)),
                  pl.BlockSpec((None, 2 * HD, seq_len), lambda b, h, i: (b, h, 0))],
        out_specs=pl.BlockSpec((tq, 2 * HD), lambda b, h, i: (b * nq + i, h)),
        out_shape=jax.ShapeDtypeStruct((nb * seq_len, A_WIDTH), F32),
        scratch_shapes=[pltpu.VMEM((2, tk, 2 * tq), F32), pltpu.VMEM((1, 2 * tq), F32), pltpu.VMEM((1, 2 * tq), F32),
                        pltpu.VMEM((2 * HD, 2 * tq), F32)],
        compiler_params=_params(("parallel", "parallel", "arbitrary")),
        name="diff_prompt",
    )(diff_lambda, diff_norm_g, qa, kvb, va_t)


def _diff_sample_kernel(pt_ref, dl_ref, g_ref, q_ref, new_ref, *rest, n_pages, past, lam_init):
    del pt_ref
    pages = rest[:n_pages]
    o_ref = rest[n_pages]
    kv_sc = rest[n_pages + 1]
    rows_per_token = 2 * A_HEADS
    for j in range(n_pages):
        for c in range(rows_per_token):
            blk = pages[j][pl.ds(c, PAGE_SIZE, stride=rows_per_token), :]
            kv_sc[j * PAGE_SIZE:(j + 1) * PAGE_SIZE, c * 2 * HD:(c + 1) * 2 * HD] = blk.astype(BF16)
    new8 = new_ref[...].astype(F32)
    pad = jnp.zeros((PAGE_SIZE - SUBLANES, 2 * A_WIDTH), F32)
    kv_sc[past:past + PAGE_SIZE, :] = jnp.concatenate([new8, pad], axis=0).astype(BF16)
    lk = past + PAGE_SIZE
    q = q_ref[...]
    lam = _lambda(dl_ref, lam_init)
    zero = jnp.zeros((2 * SUBLANES, 2 * HD), BF16)
    q_rows = []
    for h in range(A_HEADS):
        q2 = _split_components(q[:, h * 2 * HD:(h + 1) * 2 * HD])
        q_rows.append(jnp.concatenate([zero] * h + [q2] + [zero] * (A_HEADS - 1 - h), axis=1))
    q_bd = jnp.concatenate(q_rows, axis=0)
    nrow = 2 * SUBLANES * A_HEADS
    kpos = lax.broadcasted_iota(jnp.int32, (nrow, lk), 1)
    tok = lax.broadcasted_iota(jnp.int32, (nrow, lk), 0) % SUBLANES
    mask = kpos <= past + tok
    s = jnp.where(mask, _dot_nt(q_bd, kv_sc[:, 0:A_WIDTH]), MASK_VALUE)
    m = jnp.max(s, axis=1, keepdims=True)
    e = jnp.where(mask, jnp.exp(s - m), 0.0)
    p = e / jnp.maximum(jnp.sum(e, axis=1, keepdims=True), 1e-30)
    pv = _dot(p.astype(BF16), kv_sc[:, A_WIDTH:2 * A_WIDTH])
    for h in range(A_HEADS):
        blk = pv[h * 2 * SUBLANES:(h + 1) * 2 * SUBLANES, h * 2 * HD:(h + 1) * 2 * HD]
        o = blk[:SUBLANES] - lam * blk[SUBLANES:]
        o_ref[:, h * 2 * HD:(h + 1) * 2 * HD] = _head_norm(o, g_ref[...], lam_init)


def _diff_sample(page_table, cache, layer, qa8, new8, diff_lambda, diff_norm_g, lam_init):
    nb, n_pages = page_table.shape
    past = n_pages * PAGE_SIZE
    kern = functools.partial(_diff_sample_kernel, n_pages=n_pages, past=past, lam_init=lam_init)

    def page_spec(j):
        return pl.BlockSpec((None, None, PAGE_SIZE * 2 * A_HEADS, 2 * HD), lambda b, pt: (layer, pt[b, j], 0, 0))

    grid_spec = pltpu.PrefetchScalarGridSpec(
        num_scalar_prefetch=1,
        grid=(nb,),
        in_specs=[pl.BlockSpec((4, HD), lambda b, pt: (0, 0)),
                  pl.BlockSpec((1, 2 * HD), lambda b, pt: (0, 0)),
                  pl.BlockSpec((None, SUBLANES, A_WIDTH), lambda b, pt: (b, 0, 0)),
                  pl.BlockSpec((None, SUBLANES, 2 * A_WIDTH), lambda b, pt: (b, 0, 0))]
                 + [page_spec(j) for j in range(n_pages)],
        out_specs=pl.BlockSpec((None, SUBLANES, A_WIDTH), lambda b, pt: (b, 0, 0)),
        scratch_shapes=[pltpu.VMEM((past + PAGE_SIZE, 2 * A_WIDTH), BF16)],
    )
    return pl.pallas_call(
        kern,
        grid_spec=grid_spec,
        out_shape=jax.ShapeDtypeStruct((nb, SUBLANES, A_WIDTH), F32),
        compiler_params=_params(("parallel",)),
        name="diff_sample",
    )(page_table, diff_lambda, diff_norm_g, qa8, new8, *([cache] * n_pages))


def _s5_kernel(u_ref, bbd_ref, cbd_ref, d_ref, are_ref, aim_ref, ldt_ref, wglu_ref, bglu_ref, h0re_ref, h0im_ref,
               y_ref, hre_ref, him_ref, buf, hs_re, hs_im, *, nb, tc):
    @pl.when(pl.program_id(0) == 0)
    def _():
        hs_re[...] = h0re_ref[...]
        hs_im[...] = h0im_ref[...]

    a_re = are_ref[...]
    a_im = aim_ref[...]
    dt = jnp.exp(ldt_ref[...])
    mag = jnp.exp(dt * a_re)
    ab_re = mag * jnp.cos(dt * a_im)
    ab_im = mag * jnp.sin(dt * a_im)
    den = a_re * a_re + a_im * a_im
    num_re = ab_re - 1.0
    c_re = (num_re * a_re + ab_im * a_im) / den
    c_im = (ab_im * a_re - num_re * a_im) / den
    full = (SUBLANES, GP)
    ab_re, ab_im, c_re, c_im = [jnp.broadcast_to(v, full) for v in (ab_re, ab_im, c_re, c_im)]

    u = u_ref[...]
    buf[...] = _dot(u.astype(BF16), bbd_ref[...])

    def group(rg, carry):
        r0 = pl.multiple_of(rg * SUBLANES, SUBLANES)

        def step(t, h):
            hr, hi = h
            row = pl.multiple_of(t * nb + r0, SUBLANES)
            br = buf[pl.ds(row, SUBLANES), 0:GP]
            bi = buf[pl.ds(row, SUBLANES), GP:2 * GP]
            nr = ab_re * hr - ab_im * hi + (c_re * br - c_im * bi)
            ni = ab_re * hi + ab_im * hr + (c_re * bi + c_im * br)
            buf[pl.ds(row, SUBLANES), 0:GP] = nr
            buf[pl.ds(row, SUBLANES), GP:2 * GP] = ni
            return nr, ni

        hr, hi = lax.fori_loop(0, tc, step, (hs_re[pl.ds(r0, SUBLANES), :], hs_im[pl.ds(r0, SUBLANES), :]))
        hs_re[pl.ds(r0, SUBLANES), :] = hr
        hs_im[pl.ds(r0, SUBLANES), :] = hi
        return carry

    lax.fori_loop(0, nb // SUBLANES, group, 0)
    y = _dot(buf[...].astype(BF16), cbd_ref[...]) + d_ref[...] * u
    y = jax.nn.gelu(y)
    z = _dot(y.astype(BF16), wglu_ref[...]) + bglu_ref[...]
    y_ref[...] = y * jax.nn.sigmoid(z)
    hre_ref[...] = hs_re[...]
    him_ref[...] = hs_im[...]


def _s5(u, sp, h0_re, h0_im, nb, tc):
    rows = u.shape[0]
    rc = tc * nb
    const = lambda i: (0, 0)
    kern = functools.partial(_s5_kernel, nb=nb, tc=tc)
    return pl.pallas_call(
        kern,
        grid=(rows // rc,),
        in_specs=[pl.BlockSpec((rc, S_WIDTH), lambda i: (i, 0)),
                  pl.BlockSpec((S_WIDTH, 2 * GP), const), pl.BlockSpec((2 * GP, S_WIDTH), const),
                  pl.BlockSpec((1, S_WIDTH), const),
                  pl.BlockSpec((1, GP), const), pl.BlockSpec((1, GP), const), pl.BlockSpec((1, GP), const),
                  pl.BlockSpec((S_WIDTH, S_WIDTH), const), pl.BlockSpec((1, S_WIDTH), const),
                  pl.BlockSpec((nb, GP), const), pl.BlockSpec((nb, GP), const)],
        out_specs=(pl.BlockSpec((rc, S_WIDTH), lambda i: (i, 0)),
                   pl.BlockSpec((nb, GP), const), pl.BlockSpec((nb, GP), const)),
        out_shape=(jax.ShapeDtypeStruct((rows, S_WIDTH), F32),
                   jax.ShapeDtypeStruct((nb, GP), F32), jax.ShapeDtypeStruct((nb, GP), F32)),
        scratch_shapes=[pltpu.VMEM((rc, 2 * GP), F32), pltpu.VMEM((nb, GP), F32), pltpu.VMEM((nb, GP), F32)],
        compiler_params=_params(("arbitrary",)),
        name="s5",
    )(u, sp["bbd"], sp["cbd"], sp["d"], sp["a_re"], sp["a_im"], sp["log_dt"], sp["w_glu"], sp["b_glu"], h0_re, h0_im)


def _masked_softmax(s, mask):
    s = jnp.where(mask, s, MASK_VALUE)
    m = jnp.max(s, axis=-1, keepdims=True)
    e = jnp.where(mask, jnp.exp(s - m), 0.0)
    return e / jnp.maximum(jnp.sum(e, axis=-1, keepdims=True), 1e-30)


def _stack_heads(q):
    return jnp.concatenate([q[:, g * HD:(g + 1) * HD] for g in range(C_HEADS)], axis=0)


def _compress(rows, wcol):
    n = rows.shape[0] // CMP_BLOCK
    return jnp.sum(rows.reshape(n, CMP_BLOCK, HD) * wcol[None], axis=1)


def _select_blocks(imp, t, ns, axis):
    blk = lax.broadcasted_iota(jnp.int32, imp.shape, axis)
    cur = t // SEL_BLOCK
    valid = blk <= cur
    forced = jnp.where(blk == 0, 1.0, jnp.where(blk == cur, 1.0, jnp.where(blk == cur - 1, 1.0, 0.0)))
    score = jnp.where(valid, imp + FORCE_SCORE * forced, -1.0)
    score = jnp.where(blk < ns, score, -2.0)
    rank = jnp.zeros(imp.shape, F32)
    for s2 in range(ns):
        other = score[s2:s2 + 1, :] if axis == 0 else score[:, s2:s2 + 1]
        ahead = jnp.where(other > score, 1.0, jnp.where(other == score, jnp.where(blk > s2, 1.0, 0.0), 0.0))
        rank = rank + ahead
    chosen = jnp.where(rank < float(min(TOP_N, ns)), 1.0, 0.0)
    return jnp.where(blk < ns, chosen, 0.0)


def _combine(o_cmp, o_sel, o_win, gates, r):
    outs = []
    for g in range(C_HEADS):
        sl = slice(g * r, (g + 1) * r)
        outs.append(gates[:, 3 * g:3 * g + 1] * o_cmp[sl] + gates[:, 3 * g + 1:3 * g + 2] * o_sel[sl]
                    + gates[:, 3 * g + 2:3 * g + 3] * o_win[sl])
    return jnp.concatenate(outs, axis=1)


def _nsa_prompt_kernel(q_ref, nsa_ref, nsat_ref, win_ref, wint_ref, gate_ref, wcol_ref, wtile_ref, group_ref,
                       pairt_ref, et_ref, o_ref,
                       kc_sc, vct_sc, selk_sc, selvt_sc, wink_sc, winvt_sc, s_sel_sc, s_win_sc,
                       m_sc, l_sc, acc_sc, m2_sc, l2_sc, acc2_sc, *, seq_len):
    i = pl.program_id(1)
    nc = seq_len // CMP_BLOCK
    ns = seq_len // SEL_BLOCK
    nsp = min(LANES, -(-ns // SUBLANES) * SUBLANES)

    @pl.when(i == 0)
    def _():
        r = nsa_ref[...]
        kc_sc[...] = jnp.zeros(kc_sc.shape, F32)
        kc_sc[0:nc, :] = _compress(r[:, 0:HD], wcol_ref[0])
        vct_sc[...] = _dot_split(nsat_ref[HD:2 * HD, :] * wtile_ref[1:2, :], group_ref[...])
        selk_sc[...] = r[:, 2 * HD:3 * HD].astype(BF16)
        selvt_sc[...] = nsat_ref[3 * HD:4 * HD, :].astype(BF16)
        wink_sc[...] = win_ref[:, 0:HD].astype(BF16)
        winvt_sc[...] = wint_ref[HD:2 * HD, :].astype(BF16)

    q_t = q_ref[...].T
    qs_t = jnp.concatenate([q_t[g * HD:(g + 1) * HD] for g in range(C_HEADS)], axis=1)
    qb_t = qs_t.astype(BF16)
    t_q = i * Q_TILE + lax.broadcasted_iota(jnp.int32, (1, Q_TILE), 1)
    t_row = jnp.concatenate([t_q] * C_HEADS, axis=1)

    sc = _dot(kc_sc[...], qs_t, precision=HIGHEST)
    n_idx = lax.broadcasted_iota(jnp.int32, sc.shape, 0)
    cmask = ((n_idx + 1) * CMP_BLOCK - 1 <= t_row) & (n_idx < nc)
    sc = jnp.where(cmask, sc, MASK_VALUE)
    e = jnp.where(cmask, jnp.exp(sc - jnp.max(sc, axis=0, keepdims=True)), 0.0)
    p_t = e / jnp.maximum(jnp.sum(e, axis=0, keepdims=True), 1e-30)
    o_cmp_t = _dot(vct_sc[...], p_t, precision=HIGHEST)
    psum_t = p_t[:, 0:Q_TILE]
    for g in range(1, C_HEADS):
        psum_t = psum_t + p_t[:, g * Q_TILE:(g + 1) * Q_TILE]
    imp_t = _dot(pairt_ref[...], psum_t, precision=HIGHEST)
    chosen_t = _select_blocks(imp_t[0:nsp], t_q, ns, 0)
    if nsp < LANES:
        chosen_t = jnp.concatenate([chosen_t, jnp.zeros((LANES - nsp, Q_TILE), F32)], axis=0)
    chosen_b = chosen_t.astype(BF16)

    def sel_off(j):
        return pl.multiple_of(j * SEL_TILE, SEL_TILE)

    def sel_bias(sv, j):
        x = _dot(et_ref[pl.ds(sel_off(j), SEL_TILE), :], chosen_b)
        bias = (x - 1.0) * (-MASK_VALUE)
        return sv + jnp.concatenate([bias] * C_HEADS, axis=1)

    def sel_last(sv, j):
        key = j * SEL_TILE + lax.broadcasted_iota(jnp.int32, sv.shape, 0)
        return jnp.where(key <= t_row, sel_bias(sv, j), MASK_VALUE)

    _flash_t(((i + 1) * Q_TILE - 1) // SEL_TILE, 1,
             lambda j: _dot(selk_sc[pl.ds(sel_off(j), SEL_TILE), :], qb_t),
             lambda j: selvt_sc[:, pl.ds(sel_off(j), SEL_TILE)],
             sel_bias, sel_last, s_sel_sc, m_sc, l_sc, acc_sc)
    o_sel_t = acc_sc[...] / jnp.maximum(l_sc[...], 1e-30)

    q_tiles = Q_TILE // WIN_TILE
    first = jnp.maximum(i * q_tiles - WINDOW // WIN_TILE, 0)

    def win_off(j):
        return pl.multiple_of((first + j) * WIN_TILE, WIN_TILE)

    def win_mask(sv, j):
        key = (first + j) * WIN_TILE + lax.broadcasted_iota(jnp.int32, sv.shape, 0)
        return jnp.where((key <= t_row) & (key > t_row - WINDOW), sv, MASK_VALUE)

    _flash_t((i + 1) * q_tiles - first - 1, 1,
             lambda j: _dot(wink_sc[pl.ds(win_off(j), WIN_TILE), :], qb_t),
             lambda j: winvt_sc[:, pl.ds(win_off(j), WIN_TILE)],
             win_mask, win_mask, s_win_sc, m2_sc, l2_sc, acc2_sc)
    o_win_t = acc2_sc[...] / jnp.maximum(l2_sc[...], 1e-30)

    g_t = gate_ref[...].T
    outs = []
    for g in range(C_HEADS):
        cs = slice(g * Q_TILE, (g + 1) * Q_TILE)
        outs.append(g_t[3 * g:3 * g + 1] * o_cmp_t[:, cs] + g_t[3 * g + 1:3 * g + 2] * o_sel_t[:, cs]
                    + g_t[3 * g + 2:3 * g + 3] * o_win_t[:, cs])
    o_ref[...] = jnp.concatenate(outs, axis=0).T


def _nsa_prompt(qc, nsa, nsa_t, win, win_t, gates, wcol, cmp_w, nb, seq_len):
    nqb = seq_len // Q_TILE
    nc = seq_len // CMP_BLOCK
    assert nc <= LANES and seq_len % SEL_TILE == 0 and Q_TILE <= SEL_TILE and seq_len % Q_TILE == 0
    blocks = np.arange(LANES)
    pair_t = ((blocks[None, :] // (SEL_BLOCK // CMP_BLOCK) == blocks[:, None]) & (blocks[None, :] < nc)).astype(np.float32)
    expand_t = (np.arange(seq_len)[:, None] // SEL_BLOCK == blocks[None, :]).astype(np.float32)
    group = (np.arange(seq_len)[:, None] // CMP_BLOCK == blocks[None, :]).astype(np.float32)
    wtile = jnp.tile(cmp_w, (1, nc))
    kern = functools.partial(_nsa_prompt_kernel, seq_len=seq_len)
    n = C_HEADS * Q_TILE
    per_b = lambda b, i: (b, 0)
    per_b3 = lambda b, i: (b, 0, 0)
    const = lambda b, i: (0, 0)
    return pl.pallas_call(
        kern,
        grid=(nb, nqb),
        in_specs=[pl.BlockSpec((Q_TILE, C_WIDTH), lambda b, i: (b * nqb + i, 0)),
                  pl.BlockSpec((seq_len, 4 * HD), per_b), pl.BlockSpec((None, 4 * HD, seq_len), per_b3),
                  pl.BlockSpec((seq_len, 2 * HD), per_b), pl.BlockSpec((None, 2 * HD, seq_len), per_b3),
                  pl.BlockSpec((Q_TILE, LANES), lambda b, i: (b * nqb + i, 0)),
                  pl.BlockSpec((2, CMP_BLOCK, HD), lambda b, i: (0, 0, 0)),
                  pl.BlockSpec((2, seq_len), const), pl.BlockSpec((seq_len, LANES), const),
                  pl.BlockSpec((LANES, LANES), const), pl.BlockSpec((seq_len, LANES), const)],
        out_specs=pl.BlockSpec((Q_TILE, C_WIDTH), lambda b, i: (b * nqb + i, 0)),
        out_shape=jax.ShapeDtypeStruct((nb * seq_len, C_WIDTH), F32),
        scratch_shapes=[pltpu.VMEM((LANES, HD), F32), pltpu.VMEM((HD, LANES), F32),
                        pltpu.VMEM((seq_len, HD), BF16), pltpu.VMEM((HD, seq_len), BF16),
                        pltpu.VMEM((seq_len, HD), BF16), pltpu.VMEM((HD, seq_len), BF16),
                        pltpu.VMEM((2, SEL_TILE, n), F32), pltpu.VMEM((2, WIN_TILE, n), F32),
                        pltpu.VMEM((1, n), F32), pltpu.VMEM((1, n), F32), pltpu.VMEM((HD, n), F32),
                        pltpu.VMEM((1, n), F32), pltpu.VMEM((1, n), F32), pltpu.VMEM((HD, n), F32)],
        compiler_params=_params(("parallel", "arbitrary")),
        name="nsa_prompt",
    )(qc, nsa, nsa_t, win, win_t, gates, wcol, wtile, jnp.asarray(group, dtype=BF16), jnp.asarray(pair_t),
      jnp.asarray(expand_t, dtype=BF16))


def _nsa_sample_kernel(pt_ref, q_ref, new_ref, wint_ref, winnew_ref, gate_ref, wtile_ref, group_ref, pair_ref, e_ref,
                       *rest, n_pages, past, w_buf, dec_seq):
    del pt_ref
    pages = rest[:n_pages]
    o_ref, wks_ref = rest[n_pages:n_pages + 2]
    rows_sc, win_sc = rest[n_pages + 2:]
    lk = past + PAGE_SIZE
    nc = past // CMP_BLOCK
    ns = -(-(past + dec_seq) // SEL_BLOCK)
    for j in range(n_pages):
        rows_sc[:, j * PAGE_SIZE:(j + 1) * PAGE_SIZE] = pages[j][...]
    pad4 = jnp.zeros((PAGE_SIZE - SUBLANES, 4 * HD), F32)
    rows_sc[:, past:lk] = jnp.concatenate([new_ref[...], pad4], axis=0).T
    pad2 = jnp.zeros((PAGE_SIZE - SUBLANES, 2 * HD), F32)
    win_sc[:, 0:w_buf] = wint_ref[...]
    win_sc[:, w_buf:w_buf + PAGE_SIZE] = jnp.concatenate([winnew_ref[...], pad2], axis=0).T

    r = rows_sc[...]
    kc_t = _dot_split(r[0:HD, 0:past] * wtile_ref[0:1, :], group_ref[...])
    vc_t = _dot_split(r[HD:2 * HD, 0:past] * wtile_ref[1:2, :], group_ref[...])
    qs = _stack_heads(q_ref[...])
    qb = qs.astype(BF16)
    nrow = C_HEADS * SUBLANES
    t1 = past + lax.broadcasted_iota(jnp.int32, (SUBLANES, 1), 0)
    t4 = jnp.concatenate([t1] * C_HEADS, axis=0)

    sc = _dot(qs, kc_t, precision=HIGHEST)
    n_idx = lax.broadcasted_iota(jnp.int32, sc.shape, 1)
    p_cmp = _masked_softmax(sc, ((n_idx + 1) * CMP_BLOCK - 1 <= t4) & (n_idx < nc))
    o_cmp = _dot_nt(p_cmp, vc_t, precision=HIGHEST)
    psum = p_cmp[0:SUBLANES]
    for g in range(1, C_HEADS):
        psum = psum + p_cmp[g * SUBLANES:(g + 1) * SUBLANES]
    imp = _dot(psum, pair_ref[...], precision=HIGHEST)
    chosen = _select_blocks(imp, t1, ns, 1)
    sx = _dot(chosen.astype(BF16), e_ref[...])
    sx = jnp.concatenate([sx] * C_HEADS, axis=0)
    kpos = lax.broadcasted_iota(jnp.int32, (nrow, lk), 1)
    smask = (sx > 0.5) & (kpos <= t4)
    p_sel = _masked_softmax(_dot(qb, r[2 * HD:3 * HD, :].astype(BF16)), smask)
    o_sel = _dot_nt(p_sel.astype(BF16), r[3 * HD:4 * HD, :].astype(BF16))

    w = win_sc[...]
    wpos = past - w_buf + lax.broadcasted_iota(jnp.int32, (nrow, w_buf + PAGE_SIZE), 1)
    wmask = (wpos <= t4) & (wpos > t4 - WINDOW) & (wpos >= 0)
    p_win = _masked_softmax(_dot(qb, w[0:HD, :].astype(BF16)), wmask)
    o_win = _dot_nt(p_win.astype(BF16), w[HD:2 * HD, :].astype(BF16))
    o_ref[...] = _combine(o_cmp, o_sel, o_win, gate_ref[...], SUBLANES)
    wks_ref[...] = pltpu.roll(w, w_buf + PAGE_SIZE - dec_seq, 1)[:, 0:w_buf]


def _nsa_sample(page_table, cache, layer, qc8, new8, win_buf, winnew8, gates8, cmp_w, dec_seq):
    nb, n_pages = page_table.shape
    past = n_pages * PAGE_SIZE
    w_buf = win_buf.shape[3]
    lk = past + PAGE_SIZE
    assert past % CMP_BLOCK == 0 and dec_seq < CMP_BLOCK and dec_seq <= SUBLANES
    nc = past // CMP_BLOCK
    assert nc <= LANES
    blocks = np.arange(LANES)[None, :]
    group = (np.arange(past)[:, None] // CMP_BLOCK == blocks).astype(np.float32)
    pair = ((np.arange(LANES)[:, None] // (SEL_BLOCK // CMP_BLOCK) == blocks)
            & (np.arange(LANES)[:, None] < nc)).astype(np.float32)
    expand = (np.arange(LANES)[:, None] == np.arange(lk)[None, :] // SEL_BLOCK).astype(np.float32)
    wtile = jnp.tile(cmp_w, (1, past // CMP_BLOCK))
    kern = functools.partial(_nsa_sample_kernel, n_pages=n_pages, past=past, w_buf=w_buf, dec_seq=dec_seq)

    def page_spec(j):
        return pl.BlockSpec((None, None, 4 * HD, PAGE_SIZE), lambda b, pt: (layer, pt[b, j], 0, 0))

    def per_b(width):
        return pl.BlockSpec((None, SUBLANES, width), lambda b, pt: (b, 0, 0))

    const = lambda b, pt: (0, 0)
    grid_spec = pltpu.PrefetchScalarGridSpec(
        num_scalar_prefetch=1,
        grid=(nb,),
        in_specs=[per_b(C_WIDTH), per_b(4 * HD),
                  pl.BlockSpec((None, None, 2 * HD, w_buf), lambda b, pt: (layer, b, 0, 0)),
                  per_b(2 * HD), per_b(LANES),
                  pl.BlockSpec((2, past), const), pl.BlockSpec((past, LANES), const),
                  pl.BlockSpec((LANES, LANES), const), pl.BlockSpec((LANES, lk), const)]
                 + [page_spec(j) for j in range(n_pages)],
        out_specs=(per_b(C_WIDTH), pl.BlockSpec((None, 2 * HD, w_buf), lambda b, pt: (b, 0, 0))),
        scratch_shapes=[pltpu.VMEM((4 * HD, lk), F32), pltpu.VMEM((2 * HD, w_buf + PAGE_SIZE), F32)],
    )
    return pl.pallas_call(
        kern,
        grid_spec=grid_spec,
        out_shape=(jax.ShapeDtypeStruct((nb, SUBLANES, C_WIDTH), F32), jax.ShapeDtypeStruct((nb, 2 * HD, w_buf), F32)),
        compiler_params=_params(("parallel",)),
        name="nsa_sample",
    )(page_table, qc8, new8, win_buf, winnew8, gates8, wtile, jnp.asarray(group, dtype=BF16), jnp.asarray(pair),
      jnp.asarray(expand, dtype=BF16), *([cache] * n_pages))


def _merge_kernel(x_ref, oa_ref, za_ref, os_ref, zs_ref, oc_ref, zc_ref, w_ref, gf_ref, y_ref, *, final):
    mix = jnp.concatenate([oa_ref[...] * jax.nn.silu(za_ref[...]),
                           os_ref[...] * jax.nn.silu(zs_ref[...]),
                           oc_ref[...] * jax.nn.silu(zc_ref[...])], axis=1)
    y = x_ref[...] + _dot(mix.astype(BF16), w_ref[...])
    if final:
        ms = jnp.mean(y * y, axis=-1, keepdims=True)
        y = y * lax.rsqrt(ms + RMS_EPS) * gf_ref[...]
    y_ref[...] = y


def _merge(x, oa, za, os_, zs, oc, zc, w_out, final_g, tm, seq_len, os_time_major, final):
    n = x.shape[0]
    nt = seq_len // tm
    row = lambda i: (i, 0)
    os_spec = pl.BlockSpec((tm, S_WIDTH), (lambda i: (i % nt, i // nt)) if os_time_major else row)
    return pl.pallas_call(
        functools.partial(_merge_kernel, final=final),
        grid=(n // tm,),
        in_specs=[pl.BlockSpec((tm, D_MODEL), row), pl.BlockSpec((tm, A_WIDTH), row), pl.BlockSpec((tm, A_WIDTH), row),
                  os_spec, pl.BlockSpec((tm, S_WIDTH), row),
                  pl.BlockSpec((tm, C_WIDTH), row), pl.BlockSpec((tm, C_WIDTH), row),
                  pl.BlockSpec((D_MODEL, D_MODEL), lambda i: (0, 0)), pl.BlockSpec((1, D_MODEL), lambda i: (0, 0))],
        out_specs=pl.BlockSpec((tm, D_MODEL), row),
        out_shape=jax.ShapeDtypeStruct((n, D_MODEL), F32),
        compiler_params=_params(("parallel",)),
        name="merge",
    )(x, oa, za, os_, zs, oc, zc, w_out, final_g)


def _rope_tables(pos):
    half = HD // 2
    inv = ROPE_THETA ** (-jnp.arange(half, dtype=F32) / half)
    ang = pos.astype(F32)[:, None] * inv[None, :]
    cos = jnp.tile(jnp.cos(ang), (1, LANES // half))
    sin = jnp.tile(jnp.sin(ang), (1, LANES // half))
    lane = jnp.arange(LANES)
    sin = jnp.where((lane % HD) < half, -sin, sin)
    is_k = lane < HD
    return jnp.concatenate([cos, sin, jnp.where(is_k, cos, 1.0), jnp.where(is_k, sin, 0.0)], axis=1)


def _layer_weights(layer, norm_g, w_in, w_out, diff_lambda, diff_norm_g, ssm_A_re, ssm_A_im, ssm_log_dt,
                   ssm_B_re, ssm_B_im, ssm_C_re, ssm_C_im, ssm_D, ssm_w_glu, ssm_b_glu, nsa_cmp_w):
    eye = jnp.eye(S_GROUPS, dtype=F32)
    b_re = jnp.einsum("gpc,gh->gchp", ssm_B_re[layer], eye).reshape(S_WIDTH, GP)
    b_im = jnp.einsum("gpc,gh->gchp", ssm_B_im[layer], eye).reshape(S_WIDTH, GP)
    c_re = jnp.einsum("gcp,gh->gphc", ssm_C_re[layer], eye).reshape(GP, S_WIDTH)
    c_im = jnp.einsum("gcp,gh->gphc", ssm_C_im[layer], eye).reshape(GP, S_WIDTH)
    s5 = {
        "bbd": jnp.concatenate([b_re, b_im], axis=1).astype(BF16),
        "cbd": jnp.concatenate([c_re, -c_im], axis=0).astype(BF16),
        "d": ssm_D[layer].reshape(1, S_WIDTH),
        "a_re": ssm_A_re[layer].reshape(1, GP),
        "a_im": ssm_A_im[layer].reshape(1, GP),
        "log_dt": jnp.broadcast_to(ssm_log_dt[layer][:, None], (S_GROUPS, S_STATE)).reshape(1, GP),
        "w_glu": ssm_w_glu[layer].astype(BF16),
        "b_glu": ssm_b_glu[layer].reshape(1, S_WIDTH),
    }
    return {
        "norm_g": norm_g[layer].reshape(1, D_MODEL),
        "w_in": w_in,
        "layer": layer,
        "w_out": w_out[layer].astype(BF16),
        "diff_lambda": diff_lambda[layer],
        "diff_norm_g": diff_norm_g[layer].reshape(1, 2 * HD),
        "s5": s5,
        "wcol": jnp.broadcast_to(nsa_cmp_w[layer][:, :, None], (2, CMP_BLOCK, HD)),
        "cmp_w": nsa_cmp_w[layer],
        "lam_init": 0.8 - 0.6 * math.exp(-0.3 * layer),
    }


def _prompt_layer(x, lw, tab, nb, seq_len, final_g, final):
    tm = min(256, seq_len)
    qa, dr, kvb, za, us, zs, qc, nsa, win, zc, gates, nsa_t, win_t, va_t = _inproj(
        x, lw["norm_g"], lw["w_in"], lw["layer"], tab, seq_len, tm, True)
    oa = _diff_prompt(qa, kvb, va_t, lw["diff_lambda"], lw["diff_norm_g"], nb, seq_len, lw["lam_init"])
    zero = jnp.zeros((nb, GP), F32)
    os_, h_re, h_im = _s5(us.reshape(seq_len * nb, S_WIDTH), lw["s5"], zero, zero, nb, min(128, seq_len))
    os_ = os_.reshape(seq_len, nb * S_WIDTH)
    oc = _nsa_prompt(qc, nsa, nsa_t, win, win_t, gates, lw["wcol"], lw["cmp_w"], nb, seq_len)
    y = _merge(x, oa, za, os_, zs, oc, zc, lw["w_out"], final_g, tm, seq_len, True, final)
    return y, dr, nsa_t, win_t, h_re, h_im


def _pad_rows(a, nb, t):
    a = a.reshape(nb, t, a.shape[-1])
    return jnp.pad(a, ((0, 0), (0, SUBLANES - t), (0, 0)))


def _sample_layer(x, lw, tab, layer, nb, t, page_table, cache_diff, cache_nsa, cache_win, h0_re, h0_im, final_g, final):
    n = nb * t
    qa, dr, kvb, za, us, zs, qc, nsa, win, zc, gates = _inproj(
        x, lw["norm_g"], lw["w_in"], lw["layer"], tab, t, n, False)
    oa8 = _diff_sample(page_table, cache_diff, layer, _pad_rows(qa, nb, t), _pad_rows(kvb, nb, t),
                       lw["diff_lambda"], lw["diff_norm_g"], lw["lam_init"])
    oa = oa8[:, :t].reshape(n, A_WIDTH)
    u_tb = us.reshape(nb, t, S_WIDTH).transpose(1, 0, 2).reshape(n, S_WIDTH)
    os_tb, h_re, h_im = _s5(u_tb, lw["s5"], h0_re, h0_im, nb, t)
    os_ = os_tb.reshape(t, nb, S_WIDTH).transpose(1, 0, 2).reshape(n, S_WIDTH)
    oc8, wks_t = _nsa_sample(page_table, cache_nsa, layer, _pad_rows(qc, nb, t), _pad_rows(nsa, nb, t), cache_win,
                             _pad_rows(win, nb, t), _pad_rows(gates, nb, t), lw["cmp_w"], t)
    oc = oc8[:, :t].reshape(n, C_WIDTH)
    y = _merge(x, oa, za, os_, zs, oc, zc, lw["w_out"], final_g, n, t, False, final)
    return y, dr, nsa, wks_t, h_re, h_im


def kernel(x_prompt, x_sample, cache_diff_kv, cache_nsa_kv, cache_win_kv, state_ssm_re, state_ssm_im, page_table, norm_g, w_in, w_out, diff_lambda, diff_norm_g, ssm_A_re, ssm_A_im, ssm_log_dt, ssm_B_re, ssm_B_im, ssm_C_re, ssm_C_im, ssm_D, ssm_w_glu, ssm_b_glu, nsa_cmp_w, final_norm_g):
    nbp, seq_len, _ = x_prompt.shape
    nbs, dec_seq, _ = x_sample.shape
    depth = w_in.shape[0]
    n_pool = cache_diff_kv.shape[1]
    past = page_table.shape[1] * PAGE_SIZE
    w_buf = cache_win_kv.shape[2]
    cache_diff = cache_diff_kv.reshape(depth, n_pool, PAGE_SIZE * 2 * A_HEADS, 2 * HD)
    cache_nsa = cache_nsa_kv.transpose(0, 1, 3, 4, 5, 2).reshape(depth, n_pool, 4 * HD, PAGE_SIZE)
    cache_win = cache_win_kv.transpose(0, 1, 3, 4, 5, 2).reshape(depth, nbs, 2 * HD, w_buf)
    tab_p = _rope_tables(jnp.arange(seq_len, dtype=jnp.int32))
    tab_s = jnp.tile(_rope_tables(past + jnp.arange(dec_seq, dtype=jnp.int32)), (nbs, 1))
    final_g = final_norm_g.reshape(1, D_MODEL)

    xp = x_prompt.reshape(nbp * seq_len, D_MODEL)
    xs = x_sample.reshape(nbs * dec_seq, D_MODEL)
    outs = [[] for _ in range(10)]
    w_keep = min(WINDOW, seq_len)
    w_in_b = _prep_w_in(w_in)
    for layer in range(depth):
        lw = _layer_weights(layer, norm_g, w_in_b, w_out, diff_lambda, diff_norm_g, ssm_A_re, ssm_A_im, ssm_log_dt,
                            ssm_B_re, ssm_B_im, ssm_C_re, ssm_C_im, ssm_D, ssm_w_glu, ssm_b_glu, nsa_cmp_w)
        final = layer == depth - 1
        xp, dr, nsa_t, win_t, h_re, h_im = _prompt_layer(xp, lw, tab_p, nbp, seq_len, final_g, final)
        outs[0].append(dr.reshape(nbp, seq_len, 2, A_HEADS, 2 * HD))
        outs[2].append(nsa_t)
        outs[4].append(win_t[:, :, seq_len - w_keep:])
        outs[6].append(h_re.reshape(nbp, S_GROUPS, S_STATE))
        outs[7].append(h_im.reshape(nbp, S_GROUPS, S_STATE))
        xs, dr, nsa, wks_t, h_re, h_im = _sample_layer(
            xs, lw, tab_s, layer, nbs, dec_seq, page_table, cache_diff, cache_nsa, cache_win,
            state_ssm_re[layer].reshape(nbs, GP), state_ssm_im[layer].reshape(nbs, GP), final_g, final)
        outs[1].append(dr.reshape(nbs, dec_seq, 2, A_HEADS, 2 * HD))
        outs[3].append(nsa.reshape(nbs, dec_seq, 4, 1, HD))
        outs[5].append(wks_t)
        outs[8].append(h_re.reshape(nbs, S_GROUPS, S_STATE))
        outs[9].append(h_im.reshape(nbs, S_GROUPS, S_STATE))
    stacked = [jnp.stack(o) for o in outs]

    def rows_first(a, n_rows):
        return a.reshape(a.shape[0], a.shape[1], n_rows, 1, HD, a.shape[3]).transpose(0, 1, 5, 2, 3, 4)

    stacked[2] = rows_first(stacked[2], 4)
    stacked[4] = rows_first(stacked[4], 2)
    stacked[5] = rows_first(stacked[5], 2)
    return (xp.reshape(nbp, seq_len, D_MODEL), xs.reshape(nbs, dec_seq, D_MODEL), *stacked)
```

```python
import functools
import math

import jax
import jax.numpy as jnp
import numpy as np
from jax import lax
from jax.experimental import pallas as pl
from jax.experimental.pallas import tpu as pltpu

F32 = jnp.float32
BF16 = jnp.bfloat16

D_MODEL = 1024
PAGE_SIZE = 128
HD = 64
A_WIDTH = D_MODEL // 2
A_HEADS = A_WIDTH // (2 * HD)
S_WIDTH = D_MODEL // 4
S_CH = 16
S_GROUPS = S_WIDTH // S_CH
S_STATE = 64
C_WIDTH = D_MODEL - A_WIDTH - S_WIDTH
C_HEADS = C_WIDTH // HD
CMP_BLOCK = 32
SEL_BLOCK = 64
TOP_N = 8
WINDOW = 512
ROPE_THETA = 10000.0
RMS_EPS = 1e-6
MASK_VALUE = -1e30
FORCE_SCORE = 1e4
SCALE = HD ** -0.5
GP = S_GROUPS * S_STATE

LANES = 128
SUBLANES = 8
VMEM_LIMIT_BYTES = 56 * 1024 * 1024

COL_QA = 0
COL_KA = COL_QA + A_WIDTH
COL_VA = COL_KA + A_WIDTH
COL_ZA = COL_VA + A_WIDTH
COL_US = COL_ZA + A_WIDTH
COL_ZS = COL_US + S_WIDTH
COL_QC = COL_ZS + S_WIDTH
COL_KVC = COL_QC + C_WIDTH
COL_GC = COL_KVC + 6 * HD
N_GATES = 3 * C_HEADS
COL_ZC = COL_GC + N_GATES
IN_WIDTH = COL_ZC + C_WIDTH
IN_PAD = -(-IN_WIDTH // LANES) * LANES
W_PREP_TILE = 256

DIFF_Q_TILE = 512
DIFF_K_TILE = 512
DIFF_HEADS_PER_STEP = 2
Q_TILE = 256
SEL_TILE = 512
WIN_TILE = 256
HIGHEST = lax.Precision.HIGHEST


def _params(semantics):
    return pltpu.CompilerParams(dimension_semantics=semantics, vmem_limit_bytes=VMEM_LIMIT_BYTES)


def _dot_nt(a, b, precision=None):
    return lax.dot_general(a, b, (((1,), (1,)), ((), ())), preferred_element_type=F32, precision=precision)


def _dot(a, b, precision=None):
    return jnp.dot(a, b, preferred_element_type=F32, precision=precision)


def _dot_split(a, b):
    hi = a.astype(BF16)
    lo = (a - hi.astype(F32)).astype(BF16)
    return _dot(hi, b) + _dot(lo, b)


def _inproj_kernel(x_ref, g_ref, w_ref, tab_ref,
                   qa_ref, dr_ref, kvb_ref, za_ref, us_ref, zs_ref, qc_ref, nsa_ref, win_ref, zc_ref, gate_ref,
                   *transposed_refs):
    x = x_ref[...]
    ms = jnp.mean(x * x, axis=-1, keepdims=True)
    xn = (x * lax.rsqrt(ms + RMS_EPS) * g_ref[...]).astype(BF16)
    tm = x.shape[0]
    cos = tab_ref[:, 0:LANES]
    sin = tab_ref[:, LANES:2 * LANES]
    cos_kv = tab_ref[:, 2 * LANES:3 * LANES]
    sin_kv = tab_ref[:, 3 * LANES:4 * LANES]
    lane = lax.broadcasted_iota(jnp.int32, (tm, LANES), 1)
    first_half = (lane % HD) < (HD // 2)

    def rope(v, c, s):
        partner = jnp.where(first_half, pltpu.roll(v, LANES - HD // 2, 1), pltpu.roll(v, HD // 2, 1))
        return v * c + partner * s

    def proj(c0, width):
        return _dot(xn, w_ref[:, c0:c0 + width])

    def chunks(y):
        return [y[:, j * LANES:(j + 1) * LANES] for j in range(y.shape[1] // LANES)]

    y = proj(COL_QA, A_WIDTH)
    for j, c in enumerate(chunks(y)):
        qa_ref[:, j * LANES:(j + 1) * LANES] = (rope(c, cos, sin) * SCALE).astype(qa_ref.dtype)
    y = proj(COL_KA, A_WIDTH)
    for j, c in enumerate(chunks(y)):
        r = rope(c, cos, sin)
        dr_ref[pl.ds(j, tm, stride=2 * A_HEADS), :] = r
        kvb_ref[:, j * LANES:(j + 1) * LANES] = r.astype(BF16)
    y_va = proj(COL_VA, A_WIDTH)
    for j, c in enumerate(chunks(y_va)):
        dr_ref[pl.ds(A_HEADS + j, tm, stride=2 * A_HEADS), :] = c
    kvb_ref[:, A_WIDTH:2 * A_WIDTH] = y_va.astype(BF16)
    za_ref[...] = proj(COL_ZA, A_WIDTH)
    y = proj(COL_US, 2 * S_WIDTH)
    us_ref[...] = y[:, :S_WIDTH]
    zs_ref[...] = y[:, S_WIDTH:]
    y = proj(COL_QC, C_WIDTH)
    for j, c in enumerate(chunks(y)):
        qc_ref[:, j * LANES:(j + 1) * LANES] = rope(c, cos, sin) * SCALE
    y = proj(COL_KVC, 6 * HD)
    kv = [rope(c, cos_kv, sin_kv) for c in chunks(y)]
    nsa_ref[:, 0:LANES] = kv[0]
    nsa_ref[:, LANES:2 * LANES] = kv[1]
    win_ref[...] = kv[2]
    if transposed_refs:
        nsa_t_ref, win_t_ref, va_t_ref = transposed_refs
        nsa_t_ref[0:LANES, :] = kv[0].T
        nsa_t_ref[LANES:2 * LANES, :] = kv[1].T
        win_t_ref[...] = kv[2].T
        for j, c in enumerate(chunks(y_va)):
            va_t_ref[j * LANES:(j + 1) * LANES, :] = c.T.astype(BF16)
    y = proj(COL_GC, IN_PAD - COL_GC)
    gate_ref[...] = jax.nn.sigmoid(y[:, 0:LANES])
    zc_ref[...] = y[:, N_GATES:N_GATES + C_WIDTH]


def _prep_w_in_kernel(wt_ref, o_ref):
    for layer in range(o_ref.shape[0]):
        o_ref[layer] = wt_ref[:, layer, :].T.astype(BF16)


def _prep_w_in(w_in):
    depth = w_in.shape[0]
    wt = jnp.pad(w_in.transpose(2, 0, 1), ((0, IN_PAD - IN_WIDTH), (0, 0), (0, 0)))
    return pl.pallas_call(
        _prep_w_in_kernel,
        grid=(IN_PAD // W_PREP_TILE,),
        in_specs=[pl.BlockSpec((W_PREP_TILE, depth, D_MODEL), lambda i: (i, 0, 0))],
        out_specs=pl.BlockSpec((depth, D_MODEL, W_PREP_TILE), lambda i: (0, 0, i)),
        out_shape=jax.ShapeDtypeStruct((depth, D_MODEL, IN_PAD), BF16),
        compiler_params=_params(("parallel",)),
        name="prep_w_in",
    )(wt)


def _inproj(x, norm_g, w, layer, tab, seq_len, tm, prompt):
    n = x.shape[0]
    nt = tab.shape[0] // tm
    grid = (n // tm,)
    row = lambda i: (i, 0)
    nb = n // seq_len
    if prompt:
        us_shape = jax.ShapeDtypeStruct((seq_len, nb * S_WIDTH), F32)
        us_spec = pl.BlockSpec((tm, S_WIDTH), lambda i: (i % nt, i // nt))
    else:
        us_shape = jax.ShapeDtypeStruct((n, S_WIDTH), F32)
        us_spec = pl.BlockSpec((tm, S_WIDTH), row)
    rows_per_token = 2 * A_HEADS
    out_shape = (
        jax.ShapeDtypeStruct((n, A_WIDTH), F32 if prompt else BF16),
        jax.ShapeDtypeStruct((n * rows_per_token, 2 * HD), F32),
        jax.ShapeDtypeStruct((n, 2 * A_WIDTH), BF16),
        jax.ShapeDtypeStruct((n, A_WIDTH), F32),
        us_shape,
        jax.ShapeDtypeStruct((n, S_WIDTH), F32),
        jax.ShapeDtypeStruct((n, C_WIDTH), F32),
        jax.ShapeDtypeStruct((n, 4 * HD), F32),
        jax.ShapeDtypeStruct((n, 2 * HD), F32),
        jax.ShapeDtypeStruct((n, C_WIDTH), F32),
        jax.ShapeDtypeStruct((n, LANES), F32),
    )
    out_specs = (
        pl.BlockSpec((tm, A_WIDTH), row), pl.BlockSpec((tm * rows_per_token, 2 * HD), row),
        pl.BlockSpec((tm, 2 * A_WIDTH), row),
        pl.BlockSpec((tm, A_WIDTH), row), us_spec, pl.BlockSpec((tm, S_WIDTH), row),
        pl.BlockSpec((tm, C_WIDTH), row), pl.BlockSpec((tm, 4 * HD), row), pl.BlockSpec((tm, 2 * HD), row),
        pl.BlockSpec((tm, C_WIDTH), row), pl.BlockSpec((tm, LANES), row),
    )
    if prompt:
        out_shape += (jax.ShapeDtypeStruct((nb, 4 * HD, seq_len), F32), jax.ShapeDtypeStruct((nb, 2 * HD, seq_len), F32),
                      jax.ShapeDtypeStruct((nb, A_WIDTH, seq_len), BF16))
        out_specs += (pl.BlockSpec((None, 4 * HD, tm), lambda i: (i // nt, 0, i % nt)),
                      pl.BlockSpec((None, 2 * HD, tm), lambda i: (i // nt, 0, i % nt)),
                      pl.BlockSpec((None, A_WIDTH, tm), lambda i: (i // nt, 0, i % nt)))
    return pl.pallas_call(
        _inproj_kernel,
        grid=grid,
        in_specs=[pl.BlockSpec((tm, D_MODEL), row),
                  pl.BlockSpec((1, D_MODEL), lambda i: (0, 0)),
                  pl.BlockSpec((None, D_MODEL, IN_PAD), lambda i: (layer, 0, 0)),
                  pl.BlockSpec((tm, 4 * LANES), lambda i: (i % nt, 0))],
        out_specs=out_specs,
        out_shape=out_shape,
        compiler_params=_params(("parallel",)),
        name="inproj",
    )(x, norm_g, w, tab)


def _flash_t_init(m_sc, l_sc, acc_sc):
    m_sc[...] = jnp.full(m_sc.shape, MASK_VALUE, F32)
    l_sc[...] = jnp.zeros(l_sc.shape, F32)
    acc_sc[...] = jnp.zeros(acc_sc.shape, F32)


def _flash_t_update(s, v_t, m_sc, l_sc, acc_sc):
    m_prev = m_sc[...]
    m_new = jnp.maximum(m_prev, jnp.max(s, axis=0, keepdims=True))
    alpha = jnp.exp(m_prev - m_new)
    p = jnp.exp(s - m_new)
    l_sc[...] = alpha * l_sc[...] + jnp.sum(p, axis=0, keepdims=True)
    acc_sc[...] = alpha * acc_sc[...] + _dot(v_t, p.astype(BF16))
    m_sc[...] = m_new


def _flash_t(n_plain, n_tail, chains, mask_plain, mask_tail, unroll=1):
    for scores, _, s_sc, m_sc, l_sc, acc_sc in chains:
        _flash_t_init(m_sc, l_sc, acc_sc)
        s_sc[0] = scores(0)

    def step(j, mask, prefetch):
        cur = [c[2][j % 2] for c in chains]
        if prefetch:
            for scores, _, s_sc, _, _, _ in chains:
                s_sc[(j + 1) % 2] = scores(j + 1)
        for s_cur, (_, values_t, _, m_sc, l_sc, acc_sc) in zip(cur, chains):
            if mask is not None:
                s_cur = mask(s_cur, j)
            _flash_t_update(s_cur, values_t(j), m_sc, l_sc, acc_sc)

    def body(jj, carry):
        for u in range(unroll):
            step(jj * unroll + u, mask_plain, True)
        return carry

    lax.fori_loop(0, n_plain // unroll, body, 0)
    for t in range(n_tail):
        step(n_plain + t, mask_tail, t + 1 < n_tail)


def _lambda(dl_ref, lam_init):
    dl = dl_ref[...]
    a = jnp.sum(dl[0:1] * dl[1:2], axis=-1, keepdims=True)
    b = jnp.sum(dl[2:3] * dl[3:4], axis=-1, keepdims=True)
    return jnp.exp(a) - jnp.exp(b) + lam_init


def _split_components(q):
    lane = lax.broadcasted_iota(jnp.int32, q.shape, 1)
    zero = jnp.zeros_like(q)
    return jnp.concatenate([jnp.where(lane < HD, q, zero), jnp.where(lane >= HD, q, zero)], axis=0)


def _head_norm(o, g, lam_init):
    ms = jnp.mean(o * o, axis=-1, keepdims=True)
    return o * lax.rsqrt(ms + RMS_EPS) * g * (1.0 - lam_init)


def _diff_prompt_kernel(dl_ref, g_ref, q_ref, k_ref, vt_ref, o_ref, s_sc, m_sc, l_sc, acc_sc, *, tq, tk, lam_init):
    qi = pl.program_id(2)
    heads = q_ref.shape[1] // (2 * HD)

    def causal(s, j):
        key = j * tk + lax.broadcasted_iota(jnp.int32, s.shape, 0)
        qry = lax.broadcasted_iota(jnp.int32, s.shape, 1)
        qry = qi * tq + jnp.where(qry >= tq, qry - tq, qry)
        return jnp.where(key <= qry, s, MASK_VALUE)

    def chain(h):
        cols = slice(h * 2 * HD, (h + 1) * 2 * HD)
        q_t = q_ref[:, cols].T
        row = lax.broadcasted_iota(jnp.int32, q_t.shape, 0)
        zero = jnp.zeros_like(q_t)
        q2_t = jnp.concatenate([jnp.where(row < HD, q_t, zero), jnp.where(row >= HD, q_t, zero)], axis=1).astype(BF16)

        def scores(j):
            return _dot(k_ref[pl.ds(pl.multiple_of(j * tk, tk), tk), cols], q2_t)

        def values_t(j):
            return vt_ref[h * 2 * HD:(h + 1) * 2 * HD, pl.ds(pl.multiple_of(j * tk, tk), tk)]

        return scores, values_t, s_sc.at[h], m_sc.at[h], l_sc.at[h], acc_sc.at[h]

    _flash_t(qi * (tq // tk), tq // tk, [chain(h) for h in range(heads)], None, causal, unroll=tq // tk)
    lam = _lambda(dl_ref, lam_init)
    for h in range(heads):
        o_all = acc_sc[h] / l_sc[h]
        o = (o_all[:, :tq] - lam * o_all[:, tq:]).T
        o_ref[:, h * 2 * HD:(h + 1) * 2 * HD] = _head_norm(o, g_ref[...], lam_init)


def _diff_prompt(qa, kvb, va_t, diff_lambda, diff_norm_g, nb, seq_len, lam_init):
    tq = min(DIFF_Q_TILE, seq_len)
    tk = min(DIFF_K_TILE, seq_len)
    nq = seq_len // tq
    hp = DIFF_HEADS_PER_STEP
    width = hp * 2 * HD
    kern = functools.partial(_diff_prompt_kernel, tq=tq, tk=tk, lam_init=lam_init)
    return pl.pallas_call(
        kern,
        grid=(nb, A_HEADS // hp, nq),
        in_specs=[pl.BlockSpec((4, HD), lambda b, h, i: (0, 0)),
                  pl.BlockSpec((1, 2 * HD), lambda b, h, i: (0, 0)),
                  pl.BlockSpec((tq, width), lambda b, h, i: (b * nq + i, h)),
                  pl.BlockSpec((seq_len, width), lambda b, h, i: (b, h)),
                  pl.BlockSpec((None, width, seq_len), lambda b, h, i: (b, h, 0))],
        out_specs=pl.BlockSpec((tq, width), lambda b, h, i: (b * nq + i, h)),
        out_shape=jax.ShapeDtypeStruct((nb * seq_len, A_WIDTH), F32),
        scratch_shapes=[pltpu.VMEM((hp, 2, tk, 2 * tq), F32), pltpu.VMEM((hp, 1, 2 * tq), F32),
                        pltpu.VMEM((hp, 1, 2 * tq), F32), pltpu.VMEM((hp, 2 * HD, 2 * tq), F32)],
        compiler_params=_params(("parallel", "parallel", "arbitrary")),
        name="diff_prompt",
    )(diff_lambda, diff_norm_g, qa, kvb, va_t)


def _diff_sample_kernel(pt_ref, dl_ref, g_ref, q_ref, new_ref, *rest, n_pages, past, lam_init):
    del pt_ref
    pages = rest[:n_pages]
    o_ref = rest[n_pages]
    kv_sc = rest[n_pages + 1]
    rows_per_token = 2 * A_HEADS
    for j in range(n_pages):
        for c in range(rows_per_token):
            blk = pages[j][pl.ds(c, PAGE_SIZE, stride=rows_per_token), :]
            kv_sc[j * PAGE_SIZE:(j + 1) * PAGE_SIZE, c * 2 * HD:(c + 1) * 2 * HD] = blk.astype(BF16)
    new8 = new_ref[...].astype(F32)
    pad = jnp.zeros((PAGE_SIZE - SUBLANES, 2 * A_WIDTH), F32)
    kv_sc[past:past + PAGE_SIZE, :] = jnp.concatenate([new8, pad], axis=0).astype(BF16)
    lk = past + PAGE_SIZE
    q = q_ref[...]
    lam = _lambda(dl_ref, lam_init)
    zero = jnp.zeros((2 * SUBLANES, 2 * HD), BF16)
    q_rows = []
    for h in range(A_HEADS):
        q2 = _split_components(q[:, h * 2 * HD:(h + 1) * 2 * HD])
        q_rows.append(jnp.concatenate([zero] * h + [q2] + [zero] * (A_HEADS - 1 - h), axis=1))
    q_bd = jnp.concatenate(q_rows, axis=0)
    nrow = 2 * SUBLANES * A_HEADS
    kpos = lax.broadcasted_iota(jnp.int32, (nrow, lk), 1)
    tok = lax.broadcasted_iota(jnp.int32, (nrow, lk), 0) % SUBLANES
    mask = kpos <= past + tok
    s = jnp.where(mask, _dot_nt(q_bd, kv_sc[:, 0:A_WIDTH]), MASK_VALUE)
    m = jnp.max(s, axis=1, keepdims=True)
    e = jnp.where(mask, jnp.exp(s - m), 0.0)
    p = e / jnp.maximum(jnp.sum(e, axis=1, keepdims=True), 1e-30)
    pv = _dot(p.astype(BF16), kv_sc[:, A_WIDTH:2 * A_WIDTH])
    for h in range(A_HEADS):
        blk = pv[h * 2 * SUBLANES:(h + 1) * 2 * SUBLANES, h * 2 * HD:(h + 1) * 2 * HD]
        o = blk[:SUBLANES] - lam * blk[SUBLANES:]
        o_ref[:, h * 2 * HD:(h + 1) * 2 * HD] = _head_norm(o, g_ref[...], lam_init)


def _diff_sample(page_table, cache, layer, qa8, new8, diff_lambda, diff_norm_g, lam_init):
    nb, n_pages = page_table.shape
    past = n_pages * PAGE_SIZE
    kern = functools.partial(_diff_sample_kernel, n_pages=n_pages, past=past, lam_init=lam_init)

    def page_spec(j):
        return pl.BlockSpec((None, None, PAGE_SIZE * 2 * A_HEADS, 2 * HD), lambda b, pt: (layer, pt[b, j], 0, 0))

    grid_spec = pltpu.PrefetchScalarGridSpec(
        num_scalar_prefetch=1,
        grid=(nb,),
        in_specs=[pl.BlockSpec((4, HD), lambda b, pt: (0, 0)),
                  pl.BlockSpec((1, 2 * HD), lambda b, pt: (0, 0)),
                  pl.BlockSpec((None, SUBLANES, A_WIDTH), lambda b, pt: (b, 0, 0)),
                  pl.BlockSpec((None, SUBLANES, 2 * A_WIDTH), lambda b, pt: (b, 0, 0))]
                 + [page_spec(j) for j in range(n_pages)],
        out_specs=pl.BlockSpec((None, SUBLANES, A_WIDTH), lambda b, pt: (b, 0, 0)),
        scratch_shapes=[pltpu.VMEM((past + PAGE_SIZE, 2 * A_WIDTH), BF16)],
    )
    return pl.pallas_call(
        kern,
        grid_spec=grid_spec,
        out_shape=jax.ShapeDtypeStruct((nb, SUBLANES, A_WIDTH), F32),
        compiler_params=_params(("parallel",)),
        name="diff_sample",
    )(page_table, diff_lambda, diff_norm_g, qa8, new8, *([cache] * n_pages))


def _s5_kernel(u_ref, bbd_ref, cbd_ref, d_ref, are_ref, aim_ref, ldt_ref, wglu_ref, bglu_ref, h0re_ref, h0im_ref,
               y_ref, hre_ref, him_ref, buf, hs_re, hs_im, *, nb, tc):
    @pl.when(pl.program_id(0) == 0)
    def _():
        hs_re[...] = h0re_ref[...]
        hs_im[...] = h0im_ref[...]

    a_re = are_ref[...]
    a_im = aim_ref[...]
    dt = jnp.exp(ldt_ref[...])
    mag = jnp.exp(dt * a_re)
    ab_re = mag * jnp.cos(dt * a_im)
    ab_im = mag * jnp.sin(dt * a_im)
    den = a_re * a_re + a_im * a_im
    num_re = ab_re - 1.0
    c_re = (num_re * a_re + ab_im * a_im) / den
    c_im = (ab_im * a_re - num_re * a_im) / den
    full = (SUBLANES, GP)
    ab_re, ab_im, c_re, c_im = [jnp.broadcast_to(v, full) for v in (ab_re, ab_im, c_re, c_im)]

    u = u_ref[...]
    buf[...] = _dot(u.astype(BF16), bbd_ref[...])

    def group(rg, carry):
        r0 = pl.multiple_of(rg * SUBLANES, SUBLANES)

        def step(t, h):
            hr, hi = h
            row = pl.multiple_of(t * nb + r0, SUBLANES)
            br = buf[pl.ds(row, SUBLANES), 0:GP]
            bi = buf[pl.ds(row, SUBLANES), GP:2 * GP]
            nr = ab_re * hr - ab_im * hi + (c_re * br - c_im * bi)
            ni = ab_re * hi + ab_im * hr + (c_re * bi + c_im * br)
            buf[pl.ds(row, SUBLANES), 0:GP] = nr
            buf[pl.ds(row, SUBLANES), GP:2 * GP] = ni
            return nr, ni

        hr, hi = lax.fori_loop(0, tc, step, (hs_re[pl.ds(r0, SUBLANES), :], hs_im[pl.ds(r0, SUBLANES), :]))
        hs_re[pl.ds(r0, SUBLANES), :] = hr
        hs_im[pl.ds(r0, SUBLANES), :] = hi
        return carry

    lax.fori_loop(0, nb // SUBLANES, group, 0)
    y = _dot(buf[...].astype(BF16), cbd_ref[...]) + d_ref[...] * u
    y = jax.nn.gelu(y)
    z = _dot(y.astype(BF16), wglu_ref[...]) + bglu_ref[...]
    y_ref[...] = y * jax.nn.sigmoid(z)
    hre_ref[...] = hs_re[...]
    him_ref[...] = hs_im[...]


def _s5(u, sp, h0_re, h0_im, nb, tc):
    rows = u.shape[0]
    rc = tc * nb
    const = lambda i: (0, 0)
    kern = functools.partial(_s5_kernel, nb=nb, tc=tc)
    return pl.pallas_call(
        kern,
        grid=(rows // rc,),
        in_specs=[pl.BlockSpec((rc, S_WIDTH), lambda i: (i, 0)),
                  pl.BlockSpec((S_WIDTH, 2 * GP), const), pl.BlockSpec((2 * GP, S_WIDTH), const),
                  pl.BlockSpec((1, S_WIDTH), const),
                  pl.BlockSpec((1, GP), const), pl.BlockSpec((1, GP), const), pl.BlockSpec((1, GP), const),
                  pl.BlockSpec((S_WIDTH, S_WIDTH), const), pl.BlockSpec((1, S_WIDTH), const),
                  pl.BlockSpec((nb, GP), const), pl.BlockSpec((nb, GP), const)],
        out_specs=(pl.BlockSpec((rc, S_WIDTH), lambda i: (i, 0)),
                   pl.BlockSpec((nb, GP), const), pl.BlockSpec((nb, GP), const)),
        out_shape=(jax.ShapeDtypeStruct((rows, S_WIDTH), F32),
                   jax.ShapeDtypeStruct((nb, GP), F32), jax.ShapeDtypeStruct((nb, GP), F32)),
        scratch_shapes=[pltpu.VMEM((rc, 2 * GP), F32), pltpu.VMEM((nb, GP), F32), pltpu.VMEM((nb, GP), F32)],
        compiler_params=_params(("arbitrary",)),
        name="s5",
    )(u, sp["bbd"], sp["cbd"], sp["d"], sp["a_re"], sp["a_im"], sp["log_dt"], sp["w_glu"], sp["b_glu"], h0_re, h0_im)


def _masked_softmax(s, mask):
    s = jnp.where(mask, s, MASK_VALUE)
    m = jnp.max(s, axis=-1, keepdims=True)
    e = jnp.where(mask, jnp.exp(s - m), 0.0)
    return e / jnp.maximum(jnp.sum(e, axis=-1, keepdims=True), 1e-30)


def _stack_heads(q):
    return jnp.concatenate([q[:, g * HD:(g + 1) * HD] for g in range(C_HEADS)], axis=0)


def _compress(rows, wcol):
    n = rows.shape[0] // CMP_BLOCK
    return jnp.sum(rows.reshape(n, CMP_BLOCK, HD) * wcol[None], axis=1)


def _select_blocks(imp, t, ns, axis):
    blk = lax.broadcasted_iota(jnp.int32, imp.shape, axis)
    cur = t // SEL_BLOCK
    valid = blk <= cur
    forced = jnp.where(blk == 0, 1.0, jnp.where(blk == cur, 1.0, jnp.where(blk == cur - 1, 1.0, 0.0)))
    score = jnp.where(valid, imp + FORCE_SCORE * forced, -1.0)
    score = jnp.where(blk < ns, score, -2.0)
    rank = jnp.zeros(imp.shape, F32)
    for s2 in range(ns):
        other = score[s2:s2 + 1, :] if axis == 0 else score[:, s2:s2 + 1]
        ahead = jnp.where(other > score, 1.0, jnp.where(other == score, jnp.where(blk > s2, 1.0, 0.0), 0.0))
        rank = rank + ahead
    chosen = jnp.where(rank < float(min(TOP_N, ns)), 1.0, 0.0)
    return jnp.where(blk < ns, chosen, 0.0)


def _combine(o_cmp, o_sel, o_win, gates, r):
    outs = []
    for g in range(C_HEADS):
        sl = slice(g * r, (g + 1) * r)
        outs.append(gates[:, 3 * g:3 * g + 1] * o_cmp[sl] + gates[:, 3 * g + 1:3 * g + 2] * o_sel[sl]
                    + gates[:, 3 * g + 2:3 * g + 3] * o_win[sl])
    return jnp.concatenate(outs, axis=1)


def _nsa_prompt_kernel(q_ref, nsa_ref, nsat_ref, win_ref, wint_ref, gate_ref, wcol_ref, wtile_ref, group_ref,
                       pairt_ref, et_ref, o_ref,
                       kc_sc, vct_sc, selk_sc, selvt_sc, wink_sc, winvt_sc, s_sel_sc, s_win_sc,
                       m_sc, l_sc, acc_sc, m2_sc, l2_sc, acc2_sc, *, seq_len):
    i = pl.program_id(1)
    nc = seq_len // CMP_BLOCK
    ns = seq_len // SEL_BLOCK
    nsp = min(LANES, -(-ns // SUBLANES) * SUBLANES)

    @pl.when(i == 0)
    def _():
        r = nsa_ref[...]
        kc_sc[...] = jnp.zeros(kc_sc.shape, F32)
        kc_sc[0:nc, :] = _compress(r[:, 0:HD], wcol_ref[0])
        vct_sc[...] = _dot_split(nsat_ref[HD:2 * HD, :] * wtile_ref[1:2, :], group_ref[...])
        selk_sc[...] = r[:, 2 * HD:3 * HD].astype(BF16)
        selvt_sc[...] = nsat_ref[3 * HD:4 * HD, :].astype(BF16)
        wink_sc[...] = win_ref[:, 0:HD].astype(BF16)
        winvt_sc[...] = wint_ref[HD:2 * HD, :].astype(BF16)

    q_t = q_ref[...].T
    qs_t = jnp.concatenate([q_t[g * HD:(g + 1) * HD] for g in range(C_HEADS)], axis=1)
    qb_t = qs_t.astype(BF16)
    t_q = i * Q_TILE + lax.broadcasted_iota(jnp.int32, (1, Q_TILE), 1)
    t_row = jnp.concatenate([t_q] * C_HEADS, axis=1)

    sc = _dot(kc_sc[...], qs_t, precision=HIGHEST)
    n_idx = lax.broadcasted_iota(jnp.int32, sc.shape, 0)
    cmask = ((n_idx + 1) * CMP_BLOCK - 1 <= t_row) & (n_idx < nc)
    sc = jnp.where(cmask, sc, MASK_VALUE)
    e = jnp.where(cmask, jnp.exp(sc - jnp.max(sc, axis=0, keepdims=True)), 0.0)
    p_t = e / jnp.maximum(jnp.sum(e, axis=0, keepdims=True), 1e-30)
    o_cmp_t = _dot(vct_sc[...], p_t, precision=HIGHEST)
    psum_t = p_t[:, 0:Q_TILE]
    for g in range(1, C_HEADS):
        psum_t = psum_t + p_t[:, g * Q_TILE:(g + 1) * Q_TILE]
    imp_t = _dot(pairt_ref[...], psum_t, precision=HIGHEST)
    chosen_t = _select_blocks(imp_t[0:nsp], t_q, ns, 0)
    if nsp < LANES:
        chosen_t = jnp.concatenate([chosen_t, jnp.zeros((LANES - nsp, Q_TILE), F32)], axis=0)
    chosen_b = chosen_t.astype(BF16)

    def sel_off(j):
        return pl.multiple_of(j * SEL_TILE, SEL_TILE)

    def sel_bias(sv, j):
        x = _dot(et_ref[pl.ds(sel_off(j), SEL_TILE), :], chosen_b)
        bias = (x - 1.0) * (-MASK_VALUE)
        return sv + jnp.concatenate([bias] * C_HEADS, axis=1)

    def sel_last(sv, j):
        key = j * SEL_TILE + lax.broadcasted_iota(jnp.int32, sv.shape, 0)
        return jnp.where(key <= t_row, sel_bias(sv, j), MASK_VALUE)

    _flash_t(((i + 1) * Q_TILE - 1) // SEL_TILE, 1,
             [(lambda j: _dot(selk_sc[pl.ds(sel_off(j), SEL_TILE), :], qb_t),
               lambda j: selvt_sc[:, pl.ds(sel_off(j), SEL_TILE)], s_sel_sc, m_sc, l_sc, acc_sc)],
             sel_bias, sel_last)
    o_sel_t = acc_sc[...] / jnp.maximum(l_sc[...], 1e-30)

    q_tiles = Q_TILE // WIN_TILE
    first = jnp.maximum(i * q_tiles - WINDOW // WIN_TILE, 0)

    def win_off(j):
        return pl.multiple_of((first + j) * WIN_TILE, WIN_TILE)

    def win_mask(sv, j):
        key = (first + j) * WIN_TILE + lax.broadcasted_iota(jnp.int32, sv.shape, 0)
        return jnp.where((key <= t_row) & (key > t_row - WINDOW), sv, MASK_VALUE)

    _flash_t((i + 1) * q_tiles - first - 1, 1,
             [(lambda j: _dot(wink_sc[pl.ds(win_off(j), WIN_TILE), :], qb_t),
               lambda j: winvt_sc[:, pl.ds(win_off(j), WIN_TILE)], s_win_sc, m2_sc, l2_sc, acc2_sc)],
             win_mask, win_mask)
    o_win_t = acc2_sc[...] / jnp.maximum(l2_sc[...], 1e-30)

    g_t = gate_ref[...].T
    outs = []
    for g in range(C_HEADS):
        cs = slice(g * Q_TILE, (g + 1) * Q_TILE)
        outs.append(g_t[3 * g:3 * g + 1] * o_cmp_t[:, cs] + g_t[3 * g + 1:3 * g + 2] * o_sel_t[:, cs]
                    + g_t[3 * g + 2:3 * g + 3] * o_win_t[:, cs])
    o_ref[...] = jnp.concatenate(outs, axis=0).T


def _nsa_prompt(qc, nsa, nsa_t, win, win_t, gates, wcol, cmp_w, nb, seq_len):
    nqb = seq_len // Q_TILE
    nc = seq_len // CMP_BLOCK
    assert nc <= LANES and seq_len % SEL_TILE == 0 and Q_TILE <= SEL_TILE and seq_len % Q_TILE == 0
    blocks = np.arange(LANES)
    pair_t = ((blocks[None, :] // (SEL_BLOCK // CMP_BLOCK) == blocks[:, None]) & (blocks[None, :] < nc)).astype(np.float32)
    expand_t = (np.arange(seq_len)[:, None] // SEL_BLOCK == blocks[None, :]).astype(np.float32)
    group = (np.arange(seq_len)[:, None] // CMP_BLOCK == blocks[None, :]).astype(np.float32)
    wtile = jnp.tile(cmp_w, (1, nc))
    kern = functools.partial(_nsa_prompt_kernel, seq_len=seq_len)
    n = C_HEADS * Q_TILE
    per_b = lambda b, i: (b, 0)
    per_b3 = lambda b, i: (b, 0, 0)
    const = lambda b, i: (0, 0)
    return pl.pallas_call(
        kern,
        grid=(nb, nqb),
        in_specs=[pl.BlockSpec((Q_TILE, C_WIDTH), lambda b, i: (b * nqb + i, 0)),
                  pl.BlockSpec((seq_len, 4 * HD), per_b), pl.BlockSpec((None, 4 * HD, seq_len), per_b3),
                  pl.BlockSpec((seq_len, 2 * HD), per_b), pl.BlockSpec((None, 2 * HD, seq_len), per_b3),
                  pl.BlockSpec((Q_TILE, LANES), lambda b, i: (b * nqb + i, 0)),
                  pl.BlockSpec((2, CMP_BLOCK, HD), lambda b, i: (0, 0, 0)),
                  pl.BlockSpec((2, seq_len), const), pl.BlockSpec((seq_len, LANES), const),
                  pl.BlockSpec((LANES, LANES), const), pl.BlockSpec((seq_len, LANES), const)],
        out_specs=pl.BlockSpec((Q_TILE, C_WIDTH), lambda b, i: (b * nqb + i, 0)),
        out_shape=jax.ShapeDtypeStruct((nb * seq_len, C_WIDTH), F32),
        scratch_shapes=[pltpu.VMEM((LANES, HD), F32), pltpu.VMEM((HD, LANES), F32),
                        pltpu.VMEM((seq_len, HD), BF16), pltpu.VMEM((HD, seq_len), BF16),
                        pltpu.VMEM((seq_len, HD), BF16), pltpu.VMEM((HD, seq_len), BF16),
                        pltpu.VMEM((2, SEL_TILE, n), F32), pltpu.VMEM((2, WIN_TILE, n), F32),
                        pltpu.VMEM((1, n), F32), pltpu.VMEM((1, n), F32), pltpu.VMEM((HD, n), F32),
                        pltpu.VMEM((1, n), F32), pltpu.VMEM((1, n), F32), pltpu.VMEM((HD, n), F32)],
        compiler_params=_params(("parallel", "arbitrary")),
        name="nsa_prompt",
    )(qc, nsa, nsa_t, win, win_t, gates, wcol, wtile, jnp.asarray(group, dtype=BF16), jnp.asarray(pair_t),
      jnp.asarray(expand_t, dtype=BF16))


def _nsa_sample_kernel(pt_ref, q_ref, new_ref, wint_ref, winnew_ref, gate_ref, wtile_ref, group_ref, pair_ref, e_ref,
                       *rest, n_pages, past, w_buf, dec_seq):
    del pt_ref
    pages = rest[:n_pages]
    o_ref, wks_ref = rest[n_pages:n_pages + 2]
    rows_sc, win_sc = rest[n_pages + 2:]
    lk = past + PAGE_SIZE
    nc = past // CMP_BLOCK
    ns = -(-(past + dec_seq) // SEL_BLOCK)
    for j in range(n_pages):
        rows_sc[:, j * PAGE_SIZE:(j + 1) * PAGE_SIZE] = pages[j][...]
    pad4 = jnp.zeros((PAGE_SIZE - SUBLANES, 4 * HD), F32)
    rows_sc[:, past:lk] = jnp.concatenate([new_ref[...], pad4], axis=0).T
    pad2 = jnp.zeros((PAGE_SIZE - SUBLANES, 2 * HD), F32)
    win_sc[:, 0:w_buf] = wint_ref[...]
    win_sc[:, w_buf:w_buf + PAGE_SIZE] = jnp.concatenate([winnew_ref[...], pad2], axis=0).T

    r = rows_sc[...]
    kc_t = _dot_split(r[0:HD, 0:past] * wtile_ref[0:1, :], group_ref[...])
    vc_t = _dot_split(r[HD:2 * HD, 0:past] * wtile_ref[1:2, :], group_ref[...])
    qs = _stack_heads(q_ref[...])
    qb = qs.astype(BF16)
    nrow = C_HEADS * SUBLANES
    t1 = past + lax.broadcasted_iota(jnp.int32, (SUBLANES, 1), 0)
    t4 = jnp.concatenate([t1] * C_HEADS, axis=0)

    sc = _dot(qs, kc_t, precision=HIGHEST)
    n_idx = lax.broadcasted_iota(jnp.int32, sc.shape, 1)
    p_cmp = _masked_softmax(sc, ((n_idx + 1) * CMP_BLOCK - 1 <= t4) & (n_idx < nc))
    o_cmp = _dot_nt(p_cmp, vc_t, precision=HIGHEST)
    psum = p_cmp[0:SUBLANES]
    for g in range(1, C_HEADS):
        psum = psum + p_cmp[g * SUBLANES:(g + 1) * SUBLANES]
    imp = _dot(psum, pair_ref[...], precision=HIGHEST)
    chosen = _select_blocks(imp, t1, ns, 1)
    sx = _dot(chosen.astype(BF16), e_ref[...])
    sx = jnp.concatenate([sx] * C_HEADS, axis=0)
    kpos = lax.broadcasted_iota(jnp.int32, (nrow, lk), 1)
    smask = (sx > 0.5) & (kpos <= t4)
    p_sel = _masked_softmax(_dot(qb, r[2 * HD:3 * HD, :].astype(BF16)), smask)
    o_sel = _dot_nt(p_sel.astype(BF16), r[3 * HD:4 * HD, :].astype(BF16))

    w = win_sc[...]
    wpos = past - w_buf + lax.broadcasted_iota(jnp.int32, (nrow, w_buf + PAGE_SIZE), 1)
    wmask = (wpos <= t4) & (wpos > t4 - WINDOW) & (wpos >= 0)
    p_win = _masked_softmax(_dot(qb, w[0:HD, :].astype(BF16)), wmask)
    o_win = _dot_nt(p_win.astype(BF16), w[HD:2 * HD, :].astype(BF16))
    o_ref[...] = _combine(o_cmp, o_sel, o_win, gate_ref[...], SUBLANES)
    wks_ref[...] = pltpu.roll(w, w_buf + PAGE_SIZE - dec_seq, 1)[:, 0:w_buf]


def _nsa_sample(page_table, cache, layer, qc8, new8, win_buf, winnew8, gates8, cmp_w, dec_seq):
    nb, n_pages = page_table.shape
    past = n_pages * PAGE_SIZE
    w_buf = win_buf.shape[3]
    lk = past + PAGE_SIZE
    assert past % CMP_BLOCK == 0 and dec_seq < CMP_BLOCK and dec_seq <= SUBLANES
    nc = past // CMP_BLOCK
    assert nc <= LANES
    blocks = np.arange(LANES)[None, :]
    group = (np.arange(past)[:, None] // CMP_BLOCK == blocks).astype(np.float32)
    pair = ((np.arange(LANES)[:, None] // (SEL_BLOCK // CMP_BLOCK) == blocks)
            & (np.arange(LANES)[:, None] < nc)).astype(np.float32)
    expand = (np.arange(LANES)[:, None] == np.arange(lk)[None, :] // SEL_BLOCK).astype(np.float32)
    wtile = jnp.tile(cmp_w, (1, past // CMP_BLOCK))
    kern = functools.partial(_nsa_sample_kernel, n_pages=n_pages, past=past, w_buf=w_buf, dec_seq=dec_seq)

    def page_spec(j):
        return pl.BlockSpec((None, None, 4 * HD, PAGE_SIZE), lambda b, pt: (layer, pt[b, j], 0, 0))

    def per_b(width):
        return pl.BlockSpec((None, SUBLANES, width), lambda b, pt: (b, 0, 0))

    const = lambda b, pt: (0, 0)
    grid_spec = pltpu.PrefetchScalarGridSpec(
        num_scalar_prefetch=1,
        grid=(nb,),
        in_specs=[per_b(C_WIDTH), per_b(4 * HD),
                  pl.BlockSpec((None, None, 2 * HD, w_buf), lambda b, pt: (layer, b, 0, 0)),
                  per_b(2 * HD), per_b(LANES),
                  pl.BlockSpec((2, past), const), pl.BlockSpec((past, LANES), const),
                  pl.BlockSpec((LANES, LANES), const), pl.BlockSpec((LANES, lk), const)]
                 + [page_spec(j) for j in range(n_pages)],
        out_specs=(per_b(C_WIDTH), pl.BlockSpec((None, 2 * HD, w_buf), lambda b, pt: (b, 0, 0))),
        scratch_shapes=[pltpu.VMEM((4 * HD, lk), F32), pltpu.VMEM((2 * HD, w_buf + PAGE_SIZE), F32)],
    )
    return pl.pallas_call(
        kern,
        grid_spec=grid_spec,
        out_shape=(jax.ShapeDtypeStruct((nb, SUBLANES, C_WIDTH), F32), jax.ShapeDtypeStruct((nb, 2 * HD, w_buf), F32)),
        compiler_params=_params(("parallel",)),
        name="nsa_sample",
    )(page_table, qc8, new8, win_buf, winnew8, gates8, wtile, jnp.asarray(group, dtype=BF16), jnp.asarray(pair),
      jnp.asarray(expand, dtype=BF16), *([cache] * n_pages))


def _merge_kernel(x_ref, oa_ref, za_ref, os_ref, zs_ref, oc_ref, zc_ref, w_ref, gf_ref, y_ref, *, final):
    mix = jnp.concatenate([oa_ref[...] * jax.nn.silu(za_ref[...]),
                           os_ref[...] * jax.nn.silu(zs_ref[...]),
                           oc_ref[...] * jax.nn.silu(zc_ref[...])], axis=1)
    y = x_ref[...] + _dot(mix.astype(BF16), w_ref[...])
    if final:
        ms = jnp.mean(y * y, axis=-1, keepdims=True)
        y = y * lax.rsqrt(ms + RMS_EPS) * gf_ref[...]
    y_ref[...] = y


def _merge(x, oa, za, os_, zs, oc, zc, w_out, final_g, tm, seq_len, os_time_major, final):
    n = x.shape[0]
    nt = seq_len // tm
    row = lambda i: (i, 0)
    os_spec = pl.BlockSpec((tm, S_WIDTH), (lambda i: (i % nt, i // nt)) if os_time_major else row)
    return pl.pallas_call(
        functools.partial(_merge_kernel, final=final),
        grid=(n // tm,),
        in_specs=[pl.BlockSpec((tm, D_MODEL), row), pl.BlockSpec((tm, A_WIDTH), row), pl.BlockSpec((tm, A_WIDTH), row),
                  os_spec, pl.BlockSpec((tm, S_WIDTH), row),
                  pl.BlockSpec((tm, C_WIDTH), row), pl.BlockSpec((tm, C_WIDTH), row),
                  pl.BlockSpec((D_MODEL, D_MODEL), lambda i: (0, 0)), pl.BlockSpec((1, D_MODEL), lambda i: (0, 0))],
        out_specs=pl.BlockSpec((tm, D_MODEL), row),
        out_shape=jax.ShapeDtypeStruct((n, D_MODEL), F32),
        compiler_params=_params(("parallel",)),
        name="merge",
    )(x, oa, za, os_, zs, oc, zc, w_out, final_g)


def _rope_tables(pos):
    half = HD // 2
    inv = ROPE_THETA ** (-jnp.arange(half, dtype=F32) / half)
    ang = pos.astype(F32)[:, None] * inv[None, :]
    cos = jnp.tile(jnp.cos(ang), (1, LANES // half))
    sin = jnp.tile(jnp.sin(ang), (1, LANES // half))
    lane = jnp.arange(LANES)
    sin = jnp.where((lane % HD) < half, -sin, sin)
    is_k = lane < HD
    return jnp.concatenate([cos, sin, jnp.where(is_k, cos, 1.0), jnp.where(is_k, sin, 0.0)], axis=1)


def _layer_weights(layer, norm_g, w_in, w_out, diff_lambda, diff_norm_g, ssm_A_re, ssm_A_im, ssm_log_dt,
                   ssm_B_re, ssm_B_im, ssm_C_re, ssm_C_im, ssm_D, ssm_w_glu, ssm_b_glu, nsa_cmp_w):
    eye = jnp.eye(S_GROUPS, dtype=F32)
    b_re = jnp.einsum("gpc,gh->gchp", ssm_B_re[layer], eye).reshape(S_WIDTH, GP)
    b_im = jnp.einsum("gpc,gh->gchp", ssm_B_im[layer], eye).reshape(S_WIDTH, GP)
    c_re = jnp.einsum("gcp,gh->gphc", ssm_C_re[layer], eye).reshape(GP, S_WIDTH)
    c_im = jnp.einsum("gcp,gh->gphc", ssm_C_im[layer], eye).reshape(GP, S_WIDTH)
    s5 = {
        "bbd": jnp.concatenate([b_re, b_im], axis=1).astype(BF16),
        "cbd": jnp.concatenate([c_re, -c_im], axis=0).astype(BF16),
        "d": ssm_D[layer].reshape(1, S_WIDTH),
        "a_re": ssm_A_re[layer].reshape(1, GP),
        "a_im": ssm_A_im[layer].reshape(1, GP),
        "log_dt": jnp.broadcast_to(ssm_log_dt[layer][:, None], (S_GROUPS, S_STATE)).reshape(1, GP),
        "w_glu": ssm_w_glu[layer].astype(BF16),
        "b_glu": ssm_b_glu[layer].reshape(1, S_WIDTH),
    }
    return {
        "norm_g": norm_g[layer].reshape(1, D_MODEL),
        "w_in": w_in,
        "layer": layer,
        "w_out": w_out[layer].astype(BF16),
        "diff_lambda": diff_lambda[layer],
        "diff_norm_g": diff_norm_g[layer].reshape(1, 2 * HD),
        "s5": s5,
        "wcol": jnp.broadcast_to(nsa_cmp_w[layer][:, :, None], (2, CMP_BLOCK, HD)),
        "cmp_w": nsa_cmp_w[layer],
        "lam_init": 0.8 - 0.6 * math.exp(-0.3 * layer),
    }


def _prompt_layer(x, lw, tab, nb, seq_len, final_g, final):
    tm = min(256, seq_len)
    qa, dr, kvb, za, us, zs, qc, nsa, win, zc, gates, nsa_t, win_t, va_t = _inproj(
        x, lw["norm_g"], lw["w_in"], lw["layer"], tab, seq_len, tm, True)
    oa = _diff_prompt(qa, kvb, va_t, lw["diff_lambda"], lw["diff_norm_g"], nb, seq_len, lw["lam_init"])
    zero = jnp.zeros((nb, GP), F32)
    os_, h_re, h_im = _s5(us.reshape(seq_len * nb, S_WIDTH), lw["s5"], zero, zero, nb, min(128, seq_len))
    os_ = os_.reshape(seq_len, nb * S_WIDTH)
    oc = _nsa_prompt(qc, nsa, nsa_t, win, win_t, gates, lw["wcol"], lw["cmp_w"], nb, seq_len)
    y = _merge(x, oa, za, os_, zs, oc, zc, lw["w_out"], final_g, tm, seq_len, True, final)
    return y, dr, nsa_t, win_t, h_re, h_im


def _pad_rows(a, nb, t):
    a = a.reshape(nb, t, a.shape[-1])
    return jnp.pad(a, ((0, 0), (0, SUBLANES - t), (0, 0)))


def _sample_layer(x, lw, tab, layer, nb, t, page_table, cache_diff, cache_nsa, cache_win, h0_re, h0_im, final_g, final):
    n = nb * t
    qa, dr, kvb, za, us, zs, qc, nsa, win, zc, gates = _inproj(
        x, lw["norm_g"], lw["w_in"], lw["layer"], tab, t, n, False)
    oa8 = _diff_sample(page_table, cache_diff, layer, _pad_rows(qa, nb, t), _pad_rows(kvb, nb, t),
                       lw["diff_lambda"], lw["diff_norm_g"], lw["lam_init"])
    oa = oa8[:, :t].reshape(n, A_WIDTH)
    u_tb = us.reshape(nb, t, S_WIDTH).transpose(1, 0, 2).reshape(n, S_WIDTH)
    os_tb, h_re, h_im = _s5(u_tb, lw["s5"], h0_re, h0_im, nb, t)
    os_ = os_tb.reshape(t, nb, S_WIDTH).transpose(1, 0, 2).reshape(n, S_WIDTH)
    oc8, wks_t = _nsa_sample(page_table, cache_nsa, layer, _pad_rows(qc, nb, t), _pad_rows(nsa, nb, t), cache_win,
                             _pad_rows(win, nb, t), _pad_rows(gates, nb, t), lw["cmp_w"], t)
    oc = oc8[:, :t].reshape(n, C_WIDTH)
    y = _merge(x, oa, za, os_, zs, oc, zc, lw["w_out"], final_g, n, t, False, final)
    return y, dr, nsa, wks_t, h_re, h_im


def kernel(x_prompt, x_sample, cache_diff_kv, cache_nsa_kv, cache_win_kv, state_ssm_re, state_ssm_im, page_table, norm_g, w_in, w_out, diff_lambda, diff_norm_g, ssm_A_re, ssm_A_im, ssm_log_dt, ssm_B_re, ssm_B_im, ssm_C_re, ssm_C_im, ssm_D, ssm_w_glu, ssm_b_glu, nsa_cmp_w, final_norm_g):
    nbp, seq_len, _ = x_prompt.shape
    nbs, dec_seq, _ = x_sample.shape
    depth = w_in.shape[0]
    n_pool = cache_diff_kv.shape[1]
    past = page_table.shape[1] * PAGE_SIZE
    w_buf = cache_win_kv.shape[2]
    cache_diff = cache_diff_kv.reshape(depth, n_pool, PAGE_SIZE * 2 * A_HEADS, 2 * HD)
    cache_nsa = cache_nsa_kv.transpose(0, 1, 3, 4, 5, 2).reshape(depth, n_pool, 4 * HD, PAGE_SIZE)
    cache_win = cache_win_kv.transpose(0, 1, 3, 4, 5, 2).reshape(depth, nbs, 2 * HD, w_buf)
    tab_p = _rope_tables(jnp.arange(seq_len, dtype=jnp.int32))
    tab_s = jnp.tile(_rope_tables(past + jnp.arange(dec_seq, dtype=jnp.int32)), (nbs, 1))
    final_g = final_norm_g.reshape(1, D_MODEL)

    xp = x_prompt.reshape(nbp * seq_len, D_MODEL)
    xs = x_sample.reshape(nbs * dec_seq, D_MODEL)
    outs = [[] for _ in range(10)]
    w_keep = min(WINDOW, seq_len)
    w_in_b = _prep_w_in(w_in)
    for layer in range(depth):
        lw = _layer_weights(layer, norm_g, w_in_b, w_out, diff_lambda, diff_norm_g, ssm_A_re, ssm_A_im, ssm_log_dt,
                            ssm_B_re, ssm_B_im, ssm_C_re, ssm_C_im, ssm_D, ssm_w_glu, ssm_b_glu, nsa_cmp_w)
        final = layer == depth - 1
        xp, dr, nsa_t, win_t, h_re, h_im = _prompt_layer(xp, lw, tab_p, nbp, seq_len, final_g, final)
        outs[0].append(dr.reshape(nbp, seq_len, 2, A_HEADS, 2 * HD))
        outs[2].append(nsa_t)
        outs[4].append(win_t[:, :, seq_len - w_keep:])
        outs[6].append(h_re.reshape(nbp, S_GROUPS, S_STATE))
        outs[7].append(h_im.reshape(nbp, S_GROUPS, S_STATE))
        xs, dr, nsa, wks_t, h_re, h_im = _sample_layer(
            xs, lw, tab_s, layer, nbs, dec_seq, page_table, cache_diff, cache_nsa, cache_win,
            state_ssm_re[layer].reshape(nbs, GP), state_ssm_im[layer].reshape(nbs, GP), final_g, final)
        outs[1].append(dr.reshape(nbs, dec_seq, 2, A_HEADS, 2 * HD))
        outs[3].append(nsa.reshape(nbs, dec_seq, 4, 1, HD))
        outs[5].append(wks_t)
        outs[8].append(h_re.reshape(nbs, S_GROUPS, S_STATE))
        outs[9].append(h_im.reshape(nbs, S_GROUPS, S_STATE))
    stacked = [jnp.stack(o) for o in outs]

    def rows_first(a, n_rows):
        return a.reshape(a.shape[0], a.shape[1], n_rows, 1, HD, a.shape[3]).transpose(0, 1, 5, 2, 3, 4)

    stacked[2] = rows_first(stacked[2], 4)
    stacked[4] = rows_first(stacked[4], 2)
    stacked[5] = rows_first(stacked[5], 2)
    return (xp.reshape(nbp, seq_len, D_MODEL), xs.reshape(nbs, dec_seq, D_MODEL), *stacked)
```

```python
import functools
import math

import jax
import jax.numpy as jnp
import numpy as np
from jax import lax
from jax.experimental import pallas as pl
from jax.experimental.pallas import tpu as pltpu

F32 = jnp.float32
BF16 = jnp.bfloat16

D_MODEL = 1024
PAGE_SIZE = 128
HD = 64
A_WIDTH = D_MODEL // 2
A_HEADS = A_WIDTH // (2 * HD)
S_WIDTH = D_MODEL // 4
S_CH = 16
S_GROUPS = S_WIDTH // S_CH
S_STATE = 64
C_WIDTH = D_MODEL - A_WIDTH - S_WIDTH
C_HEADS = C_WIDTH // HD
CMP_BLOCK = 32
SEL_BLOCK = 64
TOP_N = 8
WINDOW = 512
ROPE_THETA = 10000.0
RMS_EPS = 1e-6
MASK_VALUE = -1e30
FORCE_SCORE = 1e4
SCALE = HD ** -0.5
GP = S_GROUPS * S_STATE

LANES = 128
SUBLANES = 8
VMEM_LIMIT_BYTES = 56 * 1024 * 1024

COL_QA = 0
COL_KA = COL_QA + A_WIDTH
COL_VA = COL_KA + A_WIDTH
COL_ZA = COL_VA + A_WIDTH
COL_US = COL_ZA + A_WIDTH
COL_ZS = COL_US + S_WIDTH
COL_QC = COL_ZS + S_WIDTH
COL_KVC = COL_QC + C_WIDTH
COL_GC = COL_KVC + 6 * HD
N_GATES = 3 * C_HEADS
COL_ZC = COL_GC + N_GATES
IN_WIDTH = COL_ZC + C_WIDTH
IN_PAD = -(-IN_WIDTH // LANES) * LANES
W_PREP_TILE = 256

DIFF_Q_TILE = 512
DIFF_K_TILE = 512
DIFF_HEADS_PER_STEP = 2
Q_TILE = 256
SEL_TILE = 512
WIN_TILE = 256
HIGHEST = lax.Precision.HIGHEST


def _params(semantics):
    return pltpu.CompilerParams(dimension_semantics=semantics, vmem_limit_bytes=VMEM_LIMIT_BYTES)


def _dot_nt(a, b, precision=None):
    return lax.dot_general(a, b, (((1,), (1,)), ((), ())), preferred_element_type=F32, precision=precision)


def _dot(a, b, precision=None):
    return jnp.dot(a, b, preferred_element_type=F32, precision=precision)


def _dot_split(a, b):
    hi = a.astype(BF16)
    lo = (a - hi.astype(F32)).astype(BF16)
    return _dot(hi, b) + _dot(lo, b)


def _inproj_kernel(x_ref, g_ref, w_ref, tab_ref, *refs, carried):
    (qa_ref, dr_ref, kvb_ref, za_ref, us_ref, zs_ref, qc_ref, nsa_ref, win_ref, zc_ref, gate_ref,
     *transposed_refs) = refs[carried:]
    x = x_ref[...]
    ms = jnp.mean(x * x, axis=-1, keepdims=True)
    xn = (x * lax.rsqrt(ms + RMS_EPS) * g_ref[...]).astype(BF16)
    tm = x.shape[0]
    cos = tab_ref[:, 0:LANES]
    sin = tab_ref[:, LANES:2 * LANES]
    cos_kv = tab_ref[:, 2 * LANES:3 * LANES]
    sin_kv = tab_ref[:, 3 * LANES:4 * LANES]
    lane = lax.broadcasted_iota(jnp.int32, (tm, LANES), 1)
    first_half = (lane % HD) < (HD // 2)

    def rope(v, c, s):
        partner = jnp.where(first_half, pltpu.roll(v, LANES - HD // 2, 1), pltpu.roll(v, HD // 2, 1))
        return v * c + partner * s

    def proj(c0, width):
        return _dot(xn, w_ref[:, c0:c0 + width])

    def chunks(y):
        return [y[:, j * LANES:(j + 1) * LANES] for j in range(y.shape[1] // LANES)]

    y = proj(COL_QA, A_WIDTH)
    for j, c in enumerate(chunks(y)):
        qa_ref[:, j * LANES:(j + 1) * LANES] = (rope(c, cos, sin) * SCALE).astype(qa_ref.dtype)
    y = proj(COL_KA, A_WIDTH)
    for j, c in enumerate(chunks(y)):
        r = rope(c, cos, sin)
        dr_ref[pl.ds(j, tm, stride=2 * A_HEADS), :] = r
        kvb_ref[:, j * LANES:(j + 1) * LANES] = r.astype(BF16)
    y_va = proj(COL_VA, A_WIDTH)
    for j, c in enumerate(chunks(y_va)):
        dr_ref[pl.ds(A_HEADS + j, tm, stride=2 * A_HEADS), :] = c
    kvb_ref[:, A_WIDTH:2 * A_WIDTH] = y_va.astype(BF16)
    za_ref[...] = proj(COL_ZA, A_WIDTH)
    y = proj(COL_US, 2 * S_WIDTH)
    us_ref[...] = y[:, :S_WIDTH]
    zs_ref[...] = y[:, S_WIDTH:]
    y = proj(COL_QC, C_WIDTH)
    for j, c in enumerate(chunks(y)):
        qc_ref[:, j * LANES:(j + 1) * LANES] = rope(c, cos, sin) * SCALE
    y = proj(COL_KVC, 6 * HD)
    kv = [rope(c, cos_kv, sin_kv) for c in chunks(y)]
    nsa_ref[:, 0:LANES] = kv[0]
    nsa_ref[:, LANES:2 * LANES] = kv[1]
    win_ref[...] = kv[2]
    if transposed_refs:
        nsa_t_ref, win_t_ref, va_t_ref = transposed_refs
        nsa_t_ref[0:LANES, :] = kv[0].T
        nsa_t_ref[LANES:2 * LANES, :] = kv[1].T
        win_t_ref[...] = kv[2].T
        for j, c in enumerate(chunks(y_va)):
            va_t_ref[j * LANES:(j + 1) * LANES, :] = c.T.astype(BF16)
    y = proj(COL_GC, IN_PAD - COL_GC)
    gate_ref[...] = jax.nn.sigmoid(y[:, 0:LANES])
    zc_ref[...] = y[:, N_GATES:N_GATES + C_WIDTH]


def _prep_w_in_kernel(wt_ref, o_ref):
    for layer in range(o_ref.shape[0]):
        o_ref[layer] = wt_ref[:, layer, :].T.astype(BF16)


def _prep_w_in(w_in):
    depth = w_in.shape[0]
    wt = jnp.pad(w_in.transpose(2, 0, 1), ((0, IN_PAD - IN_WIDTH), (0, 0), (0, 0)))
    return pl.pallas_call(
        _prep_w_in_kernel,
        grid=(IN_PAD // W_PREP_TILE,),
        in_specs=[pl.BlockSpec((W_PREP_TILE, depth, D_MODEL), lambda i: (i, 0, 0))],
        out_specs=pl.BlockSpec((depth, D_MODEL, W_PREP_TILE), lambda i: (0, 0, i)),
        out_shape=jax.ShapeDtypeStruct((depth, D_MODEL, IN_PAD), BF16),
        compiler_params=_params(("parallel",)),
        name="prep_w_in",
    )(wt)


def _inproj(x, norm_g, w, layer, tab, seq_len, tm, prompt, dr_all=None):
    n = x.shape[0]
    nt = tab.shape[0] // tm
    grid = (n // tm,)
    row = lambda i: (i, 0)
    nb = n // seq_len
    if prompt:
        us_shape = jax.ShapeDtypeStruct((seq_len, nb * S_WIDTH), F32)
        us_spec = pl.BlockSpec((tm, S_WIDTH), lambda i: (i % nt, i // nt))
    else:
        us_shape = jax.ShapeDtypeStruct((n, S_WIDTH), F32)
        us_spec = pl.BlockSpec((tm, S_WIDTH), row)
    rows_per_token = 2 * A_HEADS
    depth = w.shape[0]
    if prompt:
        dr_shape = jax.ShapeDtypeStruct((depth, n * rows_per_token, 2 * HD), F32)
        dr_spec = pl.BlockSpec((None, tm * rows_per_token, 2 * HD), lambda i: (layer, i, 0))
    else:
        dr_shape = jax.ShapeDtypeStruct((n * rows_per_token, 2 * HD), F32)
        dr_spec = pl.BlockSpec((tm * rows_per_token, 2 * HD), row)
    out_shape = (
        jax.ShapeDtypeStruct((n, A_WIDTH), F32 if prompt else BF16),
        dr_shape,
        jax.ShapeDtypeStruct((n, 2 * A_WIDTH), BF16),
        jax.ShapeDtypeStruct((n, A_WIDTH), F32),
        us_shape,
        jax.ShapeDtypeStruct((n, S_WIDTH), F32),
        jax.ShapeDtypeStruct((n, C_WIDTH), F32),
        jax.ShapeDtypeStruct((n, 4 * HD), F32),
        jax.ShapeDtypeStruct((n, 2 * HD), F32),
        jax.ShapeDtypeStruct((n, C_WIDTH), F32),
        jax.ShapeDtypeStruct((n, LANES), F32),
    )
    out_specs = (
        pl.BlockSpec((tm, A_WIDTH), row), dr_spec,
        pl.BlockSpec((tm, 2 * A_WIDTH), row),
        pl.BlockSpec((tm, A_WIDTH), row), us_spec, pl.BlockSpec((tm, S_WIDTH), row),
        pl.BlockSpec((tm, C_WIDTH), row), pl.BlockSpec((tm, 4 * HD), row), pl.BlockSpec((tm, 2 * HD), row),
        pl.BlockSpec((tm, C_WIDTH), row), pl.BlockSpec((tm, LANES), row),
    )
    if prompt:
        out_shape += (jax.ShapeDtypeStruct((nb, 4 * HD, seq_len), F32), jax.ShapeDtypeStruct((nb, 2 * HD, seq_len), F32),
                      jax.ShapeDtypeStruct((nb, A_WIDTH, seq_len), BF16))
        out_specs += (pl.BlockSpec((None, 4 * HD, tm), lambda i: (i // nt, 0, i % nt)),
                      pl.BlockSpec((None, 2 * HD, tm), lambda i: (i // nt, 0, i % nt)),
                      pl.BlockSpec((None, A_WIDTH, tm), lambda i: (i // nt, 0, i % nt)))
    carried = [] if dr_all is None else [dr_all]
    return pl.pallas_call(
        functools.partial(_inproj_kernel, carried=len(carried)),
        grid=grid,
        in_specs=[pl.BlockSpec((tm, D_MODEL), row),
                  pl.BlockSpec((1, D_MODEL), lambda i: (0, 0)),
                  pl.BlockSpec((None, D_MODEL, IN_PAD), lambda i: (layer, 0, 0)),
                  pl.BlockSpec((tm, 4 * LANES), lambda i: (i % nt, 0))]
                 + [pl.BlockSpec(memory_space=pl.ANY) for _ in carried],
        out_specs=out_specs,
        out_shape=out_shape,
        input_output_aliases={4: 1} if carried else {},
        compiler_params=_params(("parallel",)),
        name="inproj",
    )(x, norm_g, w, tab, *carried)


def _flash_t_init(m_sc, l_sc, acc_sc):
    m_sc[...] = jnp.full(m_sc.shape, MASK_VALUE, F32)
    l_sc[...] = jnp.zeros(l_sc.shape, F32)
    acc_sc[...] = jnp.zeros(acc_sc.shape, F32)


def _flash_t_update(s, v_t, m_sc, l_sc, acc_sc):
    m_prev = m_sc[...]
    m_new = jnp.maximum(m_prev, jnp.max(s, axis=0, keepdims=True))
    alpha = jnp.exp(m_prev - m_new)
    p = jnp.exp(s - m_new)
    l_sc[...] = alpha * l_sc[...] + jnp.sum(p, axis=0, keepdims=True)
    acc_sc[...] = alpha * acc_sc[...] + _dot(v_t, p.astype(BF16))
    m_sc[...] = m_new


def _flash_t(n_plain, n_tail, chains, mask_plain, mask_tail, unroll=1):
    for scores, _, s_sc, m_sc, l_sc, acc_sc in chains:
        _flash_t_init(m_sc, l_sc, acc_sc)
        s_sc[0] = scores(0)

    def step(j, mask, prefetch):
        cur = [c[2][j % 2] for c in chains]
        if prefetch:
            for scores, _, s_sc, _, _, _ in chains:
                s_sc[(j + 1) % 2] = scores(j + 1)
        for s_cur, (_, values_t, _, m_sc, l_sc, acc_sc) in zip(cur, chains):
            if mask is not None:
                s_cur = mask(s_cur, j)
            _flash_t_update(s_cur, values_t(j), m_sc, l_sc, acc_sc)

    def body(jj, carry):
        for u in range(unroll):
            step(jj * unroll + u, mask_plain, True)
        return carry

    lax.fori_loop(0, n_plain // unroll, body, 0)
    for t in range(n_tail):
        step(n_plain + t, mask_tail, t + 1 < n_tail)


def _lambda(dl_ref, lam_init):
    dl = dl_ref[...]
    a = jnp.sum(dl[0:1] * dl[1:2], axis=-1, keepdims=True)
    b = jnp.sum(dl[2:3] * dl[3:4], axis=-1, keepdims=True)
    return jnp.exp(a) - jnp.exp(b) + lam_init


def _split_components(q):
    lane = lax.broadcasted_iota(jnp.int32, q.shape, 1)
    zero = jnp.zeros_like(q)
    return jnp.concatenate([jnp.where(lane < HD, q, zero), jnp.where(lane >= HD, q, zero)], axis=0)


def _head_norm(o, g, lam_init):
    ms = jnp.mean(o * o, axis=-1, keepdims=True)
    return o * lax.rsqrt(ms + RMS_EPS) * g * (1.0 - lam_init)


def _diff_prompt_kernel(dl_ref, g_ref, q_ref, k_ref, vt_ref, o_ref, s_sc, m_sc, l_sc, acc_sc, *, tq, tk, lam_init):
    qi = pl.program_id(2)
    heads = q_ref.shape[1] // (2 * HD)

    def causal(s, j):
        key = j * tk + lax.broadcasted_iota(jnp.int32, s.shape, 0)
        qry = lax.broadcasted_iota(jnp.int32, s.shape, 1)
        qry = qi * tq + jnp.where(qry >= tq, qry - tq, qry)
        return jnp.where(key <= qry, s, MASK_VALUE)

    def chain(h):
        cols = slice(h * 2 * HD, (h + 1) * 2 * HD)
        q_t = q_ref[:, cols].T
        row = lax.broadcasted_iota(jnp.int32, q_t.shape, 0)
        zero = jnp.zeros_like(q_t)
        q2_t = jnp.concatenate([jnp.where(row < HD, q_t, zero), jnp.where(row >= HD, q_t, zero)], axis=1).astype(BF16)

        def scores(j):
            return _dot(k_ref[pl.ds(pl.multiple_of(j * tk, tk), tk), cols], q2_t)

        def values_t(j):
            return vt_ref[h * 2 * HD:(h + 1) * 2 * HD, pl.ds(pl.multiple_of(j * tk, tk), tk)]

        return scores, values_t, s_sc.at[h], m_sc.at[h], l_sc.at[h], acc_sc.at[h]

    _flash_t(qi * (tq // tk), tq // tk, [chain(h) for h in range(heads)], None, causal, unroll=tq // tk)
    lam = _lambda(dl_ref, lam_init)
    for h in range(heads):
        o_all = acc_sc[h] / l_sc[h]
        o = (o_all[:, :tq] - lam * o_all[:, tq:]).T
        o_ref[:, h * 2 * HD:(h + 1) * 2 * HD] = _head_norm(o, g_ref[...], lam_init)


def _diff_prompt(qa, kvb, va_t, diff_lambda, diff_norm_g, nb, seq_len, lam_init):
    tq = min(DIFF_Q_TILE, seq_len)
    tk = min(DIFF_K_TILE, seq_len)
    nq = seq_len // tq
    hp = DIFF_HEADS_PER_STEP
    width = hp * 2 * HD
    kern = functools.partial(_diff_prompt_kernel, tq=tq, tk=tk, lam_init=lam_init)
    return pl.pallas_call(
        kern,
        grid=(nb, A_HEADS // hp, nq),
        in_specs=[pl.BlockSpec((4, HD), lambda b, h, i: (0, 0)),
                  pl.BlockSpec((1, 2 * HD), lambda b, h, i: (0, 0)),
                  pl.BlockSpec((tq, width), lambda b, h, i: (b * nq + i, h)),
                  pl.BlockSpec((seq_len, width), lambda b, h, i: (b, h)),
                  pl.BlockSpec((None, width, seq_len), lambda b, h, i: (b, h, 0))],
        out_specs=pl.BlockSpec((tq, width), lambda b, h, i: (b * nq + i, h)),
        out_shape=jax.ShapeDtypeStruct((nb * seq_len, A_WIDTH), F32),
        scratch_shapes=[pltpu.VMEM((hp, 2, tk, 2 * tq), F32), pltpu.VMEM((hp, 1, 2 * tq), F32),
                        pltpu.VMEM((hp, 1, 2 * tq), F32), pltpu.VMEM((hp, 2 * HD, 2 * tq), F32)],
        compiler_params=_params(("parallel", "parallel", "arbitrary")),
        name="diff_prompt",
    )(diff_lambda, diff_norm_g, qa, kvb, va_t)


def _diff_sample_kernel(pt_ref, dl_ref, g_ref, q_ref, new_ref, *rest, n_pages, past, lam_init):
    del pt_ref
    pages = rest[:n_pages]
    o_ref = rest[n_pages]
    kv_sc = rest[n_pages + 1]
    rows_per_token = 2 * A_HEADS
    for j in range(n_pages):
        for c in range(rows_per_token):
            blk = pages[j][pl.ds(c, PAGE_SIZE, stride=rows_per_token), :]
            kv_sc[j * PAGE_SIZE:(j + 1) * PAGE_SIZE, c * 2 * HD:(c + 1) * 2 * HD] = blk.astype(BF16)
    new8 = new_ref[...].astype(F32)
    pad = jnp.zeros((PAGE_SIZE - SUBLANES, 2 * A_WIDTH), F32)
    kv_sc[past:past + PAGE_SIZE, :] = jnp.concatenate([new8, pad], axis=0).astype(BF16)
    lk = past + PAGE_SIZE
    q = q_ref[...]
    lam = _lambda(dl_ref, lam_init)
    zero = jnp.zeros((2 * SUBLANES, 2 * HD), BF16)
    q_rows = []
    for h in range(A_HEADS):
        q2 = _split_components(q[:, h * 2 * HD:(h + 1) * 2 * HD])
        q_rows.append(jnp.concatenate([zero] * h + [q2] + [zero] * (A_HEADS - 1 - h), axis=1))
    q_bd = jnp.concatenate(q_rows, axis=0)
    nrow = 2 * SUBLANES * A_HEADS
    kpos = lax.broadcasted_iota(jnp.int32, (nrow, lk), 1)
    tok = lax.broadcasted_iota(jnp.int32, (nrow, lk), 0) % SUBLANES
    mask = kpos <= past + tok
    s = jnp.where(mask, _dot_nt(q_bd, kv_sc[:, 0:A_WIDTH]), MASK_VALUE)
    m = jnp.max(s, axis=1, keepdims=True)
    e = jnp.where(mask, jnp.exp(s - m), 0.0)
    p = e / jnp.maximum(jnp.sum(e, axis=1, keepdims=True), 1e-30)
    pv = _dot(p.astype(BF16), kv_sc[:, A_WIDTH:2 * A_WIDTH])
    for h in range(A_HEADS):
        blk = pv[h * 2 * SUBLANES:(h + 1) * 2 * SUBLANES, h * 2 * HD:(h + 1) * 2 * HD]
        o = blk[:SUBLANES] - lam * blk[SUBLANES:]
        o_ref[:, h * 2 * HD:(h + 1) * 2 * HD] = _head_norm(o, g_ref[...], lam_init)


def _diff_sample(page_table, cache, layer, qa8, new8, diff_lambda, diff_norm_g, lam_init):
    nb, n_pages = page_table.shape
    past = n_pages * PAGE_SIZE
    kern = functools.partial(_diff_sample_kernel, n_pages=n_pages, past=past, lam_init=lam_init)

    def page_spec(j):
        return pl.BlockSpec((None, None, PAGE_SIZE * 2 * A_HEADS, 2 * HD), lambda b, pt: (layer, pt[b, j], 0, 0))

    grid_spec = pltpu.PrefetchScalarGridSpec(
        num_scalar_prefetch=1,
        grid=(nb,),
        in_specs=[pl.BlockSpec((4, HD), lambda b, pt: (0, 0)),
                  pl.BlockSpec((1, 2 * HD), lambda b, pt: (0, 0)),
                  pl.BlockSpec((None, SUBLANES, A_WIDTH), lambda b, pt: (b, 0, 0)),
                  pl.BlockSpec((None, SUBLANES, 2 * A_WIDTH), lambda b, pt: (b, 0, 0))]
                 + [page_spec(j) for j in range(n_pages)],
        out_specs=pl.BlockSpec((None, SUBLANES, A_WIDTH), lambda b, pt: (b, 0, 0)),
        scratch_shapes=[pltpu.VMEM((past + PAGE_SIZE, 2 * A_WIDTH), BF16)],
    )
    return pl.pallas_call(
        kern,
        grid_spec=grid_spec,
        out_shape=jax.ShapeDtypeStruct((nb, SUBLANES, A_WIDTH), F32),
        compiler_params=_params(("parallel",)),
        name="diff_sample",
    )(page_table, diff_lambda, diff_norm_g, qa8, new8, *([cache] * n_pages))


def _s5_kernel(u_ref, bbd_ref, cbd_ref, d_ref, are_ref, aim_ref, ldt_ref, wglu_ref, bglu_ref, h0re_ref, h0im_ref,
               y_ref, hre_ref, him_ref, buf, hs_re, hs_im, *row_scratch, nb, tc):
    @pl.when(pl.program_id(0) == 0)
    def _():
        hs_re[...] = h0re_ref[...]
        hs_im[...] = h0im_ref[...]

    a_re = are_ref[...]
    a_im = aim_ref[...]
    dt = jnp.exp(ldt_ref[...])
    mag = jnp.exp(dt * a_re)
    ab_re = mag * jnp.cos(dt * a_im)
    ab_im = mag * jnp.sin(dt * a_im)
    den = a_re * a_re + a_im * a_im
    num_re = ab_re - 1.0
    c_re = (num_re * a_re + ab_im * a_im) / den
    c_im = (ab_im * a_re - num_re * a_im) / den
    full = (SUBLANES, GP)
    ab_re, ab_im, c_re, c_im = [jnp.broadcast_to(v, full) for v in (ab_re, ab_im, c_re, c_im)]

    if row_scratch:
        rows_sc, = row_scratch
        halves = S_WIDTH // LANES
        for b in range(nb):
            for h in range(halves):
                c0 = b * S_WIDTH + h * LANES
                rows_sc[h, pl.ds(b, tc, stride=nb), :] = u_ref[:, c0:c0 + LANES]
        u = jnp.concatenate([rows_sc[h] for h in range(halves)], axis=1)
    else:
        u = u_ref[...]
    buf[...] = _dot(u.astype(BF16), bbd_ref[...])

    def group(rg, carry):
        r0 = pl.multiple_of(rg * SUBLANES, SUBLANES)

        def step(t, h):
            hr, hi = h
            row = pl.multiple_of(t * nb + r0, SUBLANES)
            br = buf[pl.ds(row, SUBLANES), 0:GP]
            bi = buf[pl.ds(row, SUBLANES), GP:2 * GP]
            nr = ab_re * hr - ab_im * hi + (c_re * br - c_im * bi)
            ni = ab_re * hi + ab_im * hr + (c_re * bi + c_im * br)
            buf[pl.ds(row, SUBLANES), 0:GP] = nr
            buf[pl.ds(row, SUBLANES), GP:2 * GP] = ni
            return nr, ni

        hr, hi = lax.fori_loop(0, tc, step, (hs_re[pl.ds(r0, SUBLANES), :], hs_im[pl.ds(r0, SUBLANES), :]))
        hs_re[pl.ds(r0, SUBLANES), :] = hr
        hs_im[pl.ds(r0, SUBLANES), :] = hi
        return carry

    lax.fori_loop(0, nb // SUBLANES, group, 0)
    y = _dot(buf[...].astype(BF16), cbd_ref[...]) + d_ref[...] * u
    y = jax.nn.gelu(y)
    z = _dot(y.astype(BF16), wglu_ref[...]) + bglu_ref[...]
    y = y * jax.nn.sigmoid(z)
    if row_scratch:
        for h in range(halves):
            rows_sc[h] = y[:, h * LANES:(h + 1) * LANES]
        for b in range(nb):
            for h in range(halves):
                c0 = b * S_WIDTH + h * LANES
                y_ref[:, c0:c0 + LANES] = rows_sc[h, pl.ds(b, tc, stride=nb), :]
    else:
        y_ref[...] = y
    hre_ref[...] = hs_re[...]
    him_ref[...] = hs_im[...]


def _s5(u, sp, h0_re, h0_im, nb, tc, batch_on_lanes):
    rc = tc * nb
    steps = u.shape[0] * u.shape[1] // (rc * S_WIDTH)
    const = lambda i: (0, 0)
    kern = functools.partial(_s5_kernel, nb=nb, tc=tc)
    io_block = (tc, nb * S_WIDTH) if batch_on_lanes else (rc, S_WIDTH)
    row_scratch = [pltpu.VMEM((S_WIDTH // LANES, rc, LANES), F32)] if batch_on_lanes else []
    return pl.pallas_call(
        kern,
        grid=(steps,),
        in_specs=[pl.BlockSpec(io_block, lambda i: (i, 0)),
                  pl.BlockSpec((S_WIDTH, 2 * GP), const), pl.BlockSpec((2 * GP, S_WIDTH), const),
                  pl.BlockSpec((1, S_WIDTH), const),
                  pl.BlockSpec((1, GP), const), pl.BlockSpec((1, GP), const), pl.BlockSpec((1, GP), const),
                  pl.BlockSpec((S_WIDTH, S_WIDTH), const), pl.BlockSpec((1, S_WIDTH), const),
                  pl.BlockSpec((nb, GP), const), pl.BlockSpec((nb, GP), const)],
        out_specs=(pl.BlockSpec(io_block, lambda i: (i, 0)),
                   pl.BlockSpec((nb, GP), const), pl.BlockSpec((nb, GP), const)),
        out_shape=(jax.ShapeDtypeStruct(u.shape, F32),
                   jax.ShapeDtypeStruct((nb, GP), F32), jax.ShapeDtypeStruct((nb, GP), F32)),
        scratch_shapes=[pltpu.VMEM((rc, 2 * GP), F32), pltpu.VMEM((nb, GP), F32), pltpu.VMEM((nb, GP), F32)]
                       + row_scratch,
        compiler_params=_params(("arbitrary",)),
        name="s5",
    )(u, sp["bbd"], sp["cbd"], sp["d"], sp["a_re"], sp["a_im"], sp["log_dt"], sp["w_glu"], sp["b_glu"], h0_re, h0_im)


def _masked_softmax(s, mask):
    s = jnp.where(mask, s, MASK_VALUE)
    m = jnp.max(s, axis=-1, keepdims=True)
    e = jnp.where(mask, jnp.exp(s - m), 0.0)
    return e / jnp.maximum(jnp.sum(e, axis=-1, keepdims=True), 1e-30)


def _stack_heads(q):
    return jnp.concatenate([q[:, g * HD:(g + 1) * HD] for g in range(C_HEADS)], axis=0)


def _compress(rows, wcol):
    n = rows.shape[0] // CMP_BLOCK
    return jnp.sum(rows.reshape(n, CMP_BLOCK, HD) * wcol[None], axis=1)


def _select_blocks(imp, t, ns, axis):
    blk = lax.broadcasted_iota(jnp.int32, imp.shape, axis)
    cur = t // SEL_BLOCK
    valid = blk <= cur
    forced = jnp.where(blk == 0, 1.0, jnp.where(blk == cur, 1.0, jnp.where(blk == cur - 1, 1.0, 0.0)))
    score = jnp.where(valid, imp + FORCE_SCORE * forced, -1.0)
    score = jnp.where(blk < ns, score, -2.0)
    rank = jnp.zeros(imp.shape, F32)
    for s2 in range(ns):
        other = score[s2:s2 + 1, :] if axis == 0 else score[:, s2:s2 + 1]
        ahead = jnp.where(other > score, 1.0, jnp.where(other == score, jnp.where(blk > s2, 1.0, 0.0), 0.0))
        rank = rank + ahead
    chosen = jnp.where(rank < float(min(TOP_N, ns)), 1.0, 0.0)
    return jnp.where(blk < ns, chosen, 0.0)


def _combine(o_cmp, o_sel, o_win, gates, r):
    outs = []
    for g in range(C_HEADS):
        sl = slice(g * r, (g + 1) * r)
        outs.append(gates[:, 3 * g:3 * g + 1] * o_cmp[sl] + gates[:, 3 * g + 1:3 * g + 2] * o_sel[sl]
                    + gates[:, 3 * g + 2:3 * g + 3] * o_win[sl])
    return jnp.concatenate(outs, axis=1)


def _nsa_prompt_kernel(q_ref, nsa_ref, nsat_ref, win_ref, wint_ref, gate_ref, wcol_ref, wtile_ref, group_ref,
                       pairt_ref, et_ref, o_ref,
                       kc_sc, vct_sc, selk_sc, selvt_sc, wink_sc, winvt_sc, s_sel_sc, s_win_sc,
                       m_sc, l_sc, acc_sc, m2_sc, l2_sc, acc2_sc, *, seq_len):
    i = pl.program_id(1)
    nc = seq_len // CMP_BLOCK
    ns = seq_len // SEL_BLOCK
    nsp = min(LANES, -(-ns // SUBLANES) * SUBLANES)

    @pl.when(i == 0)
    def _():
        r = nsa_ref[...]
        kc_sc[...] = jnp.zeros(kc_sc.shape, F32)
        kc_sc[0:nc, :] = _compress(r[:, 0:HD], wcol_ref[0])
        vct_sc[...] = _dot_split(nsat_ref[HD:2 * HD, :] * wtile_ref[1:2, :], group_ref[...])
        selk_sc[...] = r[:, 2 * HD:3 * HD].astype(BF16)
        selvt_sc[...] = nsat_ref[3 * HD:4 * HD, :].astype(BF16)
        wink_sc[...] = win_ref[:, 0:HD].astype(BF16)
        winvt_sc[...] = wint_ref[HD:2 * HD, :].astype(BF16)

    q_t = q_ref[...].T
    qs_t = jnp.concatenate([q_t[g * HD:(g + 1) * HD] for g in range(C_HEADS)], axis=1)
    qb_t = qs_t.astype(BF16)
    t_q = i * Q_TILE + lax.broadcasted_iota(jnp.int32, (1, Q_TILE), 1)
    t_row = jnp.concatenate([t_q] * C_HEADS, axis=1)

    sc = _dot(kc_sc[...], qs_t, precision=HIGHEST)
    n_idx = lax.broadcasted_iota(jnp.int32, sc.shape, 0)
    cmask = ((n_idx + 1) * CMP_BLOCK - 1 <= t_row) & (n_idx < nc)
    sc = jnp.where(cmask, sc, MASK_VALUE)
    e = jnp.where(cmask, jnp.exp(sc - jnp.max(sc, axis=0, keepdims=True)), 0.0)
    p_t = e / jnp.maximum(jnp.sum(e, axis=0, keepdims=True), 1e-30)
    o_cmp_t = _dot(vct_sc[...], p_t, precision=HIGHEST)
    psum_t = p_t[:, 0:Q_TILE]
    for g in range(1, C_HEADS):
        psum_t = psum_t + p_t[:, g * Q_TILE:(g + 1) * Q_TILE]
    imp_t = _dot(pairt_ref[...], psum_t, precision=HIGHEST)
    chosen_t = _select_blocks(imp_t[0:nsp], t_q, ns, 0)
    if nsp < LANES:
        chosen_t = jnp.concatenate([chosen_t, jnp.zeros((LANES - nsp, Q_TILE), F32)], axis=0)
    chosen_b = chosen_t.astype(BF16)

    def sel_off(j):
        return pl.multiple_of(j * SEL_TILE, SEL_TILE)

    def sel_bias(sv, j):
        x = _dot(et_ref[pl.ds(sel_off(j), SEL_TILE), :], chosen_b)
        bias = (x - 1.0) * (-MASK_VALUE)
        return sv + jnp.concatenate([bias] * C_HEADS, axis=1)

    def sel_last(sv, j):
        key = j * SEL_TILE + lax.broadcasted_iota(jnp.int32, sv.shape, 0)
        return jnp.where(key <= t_row, sel_bias(sv, j), MASK_VALUE)

    _flash_t(((i + 1) * Q_TILE - 1) // SEL_TILE, 1,
             [(lambda j: _dot(selk_sc[pl.ds(sel_off(j), SEL_TILE), :], qb_t),
               lambda j: selvt_sc[:, pl.ds(sel_off(j), SEL_TILE)], s_sel_sc, m_sc, l_sc, acc_sc)],
             sel_bias, sel_last)
    o_sel_t = acc_sc[...] / jnp.maximum(l_sc[...], 1e-30)

    q_tiles = Q_TILE // WIN_TILE
    first = jnp.maximum(i * q_tiles - WINDOW // WIN_TILE, 0)

    def win_off(j):
        return pl.multiple_of((first + j) * WIN_TILE, WIN_TILE)

    def win_mask(sv, j):
        key = (first + j) * WIN_TILE + lax.broadcasted_iota(jnp.int32, sv.shape, 0)
        return jnp.where((key <= t_row) & (key > t_row - WINDOW), sv, MASK_VALUE)

    _flash_t((i + 1) * q_tiles - first - 1, 1,
             [(lambda j: _dot(wink_sc[pl.ds(win_off(j), WIN_TILE), :], qb_t),
               lambda j: winvt_sc[:, pl.ds(win_off(j), WIN_TILE)], s_win_sc, m2_sc, l2_sc, acc2_sc)],
             win_mask, win_mask)
    o_win_t = acc2_sc[...] / jnp.maximum(l2_sc[...], 1e-30)

    g_t = gate_ref[...].T
    outs = []
    for g in range(C_HEADS):
        cs = slice(g * Q_TILE, (g + 1) * Q_TILE)
        outs.append(g_t[3 * g:3 * g + 1] * o_cmp_t[:, cs] + g_t[3 * g + 1:3 * g + 2] * o_sel_t[:, cs]
                    + g_t[3 * g + 2:3 * g + 3] * o_win_t[:, cs])
    o_ref[...] = jnp.concatenate(outs, axis=0).T


def _nsa_prompt(qc, nsa, nsa_t, win, win_t, gates, wcol, cmp_w, nb, seq_len):
    nqb = seq_len // Q_TILE
    nc = seq_len // CMP_BLOCK
    assert nc <= LANES and seq_len % SEL_TILE == 0 and Q_TILE <= SEL_TILE and seq_len % Q_TILE == 0
    blocks = np.arange(LANES)
    pair_t = ((blocks[None, :] // (SEL_BLOCK // CMP_BLOCK) == blocks[:, None]) & (blocks[None, :] < nc)).astype(np.float32)
    expand_t = (np.arange(seq_len)[:, None] // SEL_BLOCK == blocks[None, :]).astype(np.float32)
    group = (np.arange(seq_len)[:, None] // CMP_BLOCK == blocks[None, :]).astype(np.float32)
    wtile = jnp.tile(cmp_w, (1, nc))
    kern = functools.partial(_nsa_prompt_kernel, seq_len=seq_len)
    n = C_HEADS * Q_TILE
    per_b = lambda b, i: (b, 0)
    per_b3 = lambda b, i: (b, 0, 0)
    const = lambda b, i: (0, 0)
    return pl.pallas_call(
        kern,
        grid=(nb, nqb),
        in_specs=[pl.BlockSpec((Q_TILE, C_WIDTH), lambda b, i: (b * nqb + i, 0)),
                  pl.BlockSpec((seq_len, 4 * HD), per_b), pl.BlockSpec((None, 4 * HD, seq_len), per_b3),
                  pl.BlockSpec((seq_len, 2 * HD), per_b), pl.BlockSpec((None, 2 * HD, seq_len), per_b3),
                  pl.BlockSpec((Q_TILE, LANES), lambda b, i: (b * nqb + i, 0)),
                  pl.BlockSpec((2, CMP_BLOCK, HD), lambda b, i: (0, 0, 0)),
                  pl.BlockSpec((2, seq_len), const), pl.BlockSpec((seq_len, LANES), const),
                  pl.BlockSpec((LANES, LANES), const), pl.BlockSpec((seq_len, LANES), const)],
        out_specs=pl.BlockSpec((Q_TILE, C_WIDTH), lambda b, i: (b * nqb + i, 0)),
        out_shape=jax.ShapeDtypeStruct((nb * seq_len, C_WIDTH), F32),
        scratch_shapes=[pltpu.VMEM((LANES, HD), F32), pltpu.VMEM((HD, LANES), F32),
                        pltpu.VMEM((seq_len, HD), BF16), pltpu.VMEM((HD, seq_len), BF16),
                        pltpu.VMEM((seq_len, HD), BF16), pltpu.VMEM((HD, seq_len), BF16),
                        pltpu.VMEM((2, SEL_TILE, n), F32), pltpu.VMEM((2, WIN_TILE, n), F32),
                        pltpu.VMEM((1, n), F32), pltpu.VMEM((1, n), F32), pltpu.VMEM((HD, n), F32),
                        pltpu.VMEM((1, n), F32), pltpu.VMEM((1, n), F32), pltpu.VMEM((HD, n), F32)],
        compiler_params=_params(("parallel", "arbitrary")),
        name="nsa_prompt",
    )(qc, nsa, nsa_t, win, win_t, gates, wcol, wtile, jnp.asarray(group, dtype=BF16), jnp.asarray(pair_t),
      jnp.asarray(expand_t, dtype=BF16))


def _nsa_sample_kernel(pt_ref, q_ref, new_ref, wint_ref, winnew_ref, gate_ref, wtile_ref, group_ref, pair_ref, e_ref,
                       *rest, n_pages, past, w_buf, dec_seq):
    del pt_ref
    pages = rest[:n_pages]
    o_ref, wks_ref = rest[n_pages:n_pages + 2]
    rows_sc, win_sc = rest[n_pages + 2:]
    lk = past + PAGE_SIZE
    nc = past // CMP_BLOCK
    ns = -(-(past + dec_seq) // SEL_BLOCK)
    for j in range(n_pages):
        rows_sc[:, j * PAGE_SIZE:(j + 1) * PAGE_SIZE] = pages[j][...]
    pad4 = jnp.zeros((PAGE_SIZE - SUBLANES, 4 * HD), F32)
    rows_sc[:, past:lk] = jnp.concatenate([new_ref[...], pad4], axis=0).T
    pad2 = jnp.zeros((PAGE_SIZE - SUBLANES, 2 * HD), F32)
    win_sc[:, 0:w_buf] = wint_ref[...]
    win_sc[:, w_buf:w_buf + PAGE_SIZE] = jnp.concatenate([winnew_ref[...], pad2], axis=0).T

    r = rows_sc[...]
    kc_t = _dot_split(r[0:HD, 0:past] * wtile_ref[0:1, :], group_ref[...])
    vc_t = _dot_split(r[HD:2 * HD, 0:past] * wtile_ref[1:2, :], group_ref[...])
    qs = _stack_heads(q_ref[...])
    qb = qs.astype(BF16)
    nrow = C_HEADS * SUBLANES
    t1 = past + lax.broadcasted_iota(jnp.int32, (SUBLANES, 1), 0)
    t4 = jnp.concatenate([t1] * C_HEADS, axis=0)

    sc = _dot(qs, kc_t, precision=HIGHEST)
    n_idx = lax.broadcasted_iota(jnp.int32, sc.shape, 1)
    p_cmp = _masked_softmax(sc, ((n_idx + 1) * CMP_BLOCK - 1 <= t4) & (n_idx < nc))
    o_cmp = _dot_nt(p_cmp, vc_t, precision=HIGHEST)
    psum = p_cmp[0:SUBLANES]
    for g in range(1, C_HEADS):
        psum = psum + p_cmp[g * SUBLANES:(g + 1) * SUBLANES]
    imp = _dot(psum, pair_ref[...], precision=HIGHEST)
    chosen = _select_blocks(imp, t1, ns, 1)
    sx = _dot(chosen.astype(BF16), e_ref[...])
    sx = jnp.concatenate([sx] * C_HEADS, axis=0)
    kpos = lax.broadcasted_iota(jnp.int32, (nrow, lk), 1)
    smask = (sx > 0.5) & (kpos <= t4)
    p_sel = _masked_softmax(_dot(qb, r[2 * HD:3 * HD, :].astype(BF16)), smask)
    o_sel = _dot_nt(p_sel.astype(BF16), r[3 * HD:4 * HD, :].astype(BF16))

    w = win_sc[...]
    wpos = past - w_buf + lax.broadcasted_iota(jnp.int32, (nrow, w_buf + PAGE_SIZE), 1)
    wmask = (wpos <= t4) & (wpos > t4 - WINDOW) & (wpos >= 0)
    p_win = _masked_softmax(_dot(qb, w[0:HD, :].astype(BF16)), wmask)
    o_win = _dot_nt(p_win.astype(BF16), w[HD:2 * HD, :].astype(BF16))
    o_ref[...] = _combine(o_cmp, o_sel, o_win, gate_ref[...], SUBLANES)
    wks_ref[...] = pltpu.roll(w, w_buf + PAGE_SIZE - dec_seq, 1)[:, 0:w_buf]


def _nsa_sample(page_table, cache, layer, qc8, new8, win_buf, winnew8, gates8, cmp_w, dec_seq):
    nb, n_pages = page_table.shape
    past = n_pages * PAGE_SIZE
    w_buf = win_buf.shape[3]
    lk = past + PAGE_SIZE
    assert past % CMP_BLOCK == 0 and dec_seq < CMP_BLOCK and dec_seq <= SUBLANES
    nc = past // CMP_BLOCK
    assert nc <= LANES
    blocks = np.arange(LANES)[None, :]
    group = (np.arange(past)[:, None] // CMP_BLOCK == blocks).astype(np.float32)
    pair = ((np.arange(LANES)[:, None] // (SEL_BLOCK // CMP_BLOCK) == blocks)
            & (np.arange(LANES)[:, None] < nc)).astype(np.float32)
    expand = (np.arange(LANES)[:, None] == np.arange(lk)[None, :] // SEL_BLOCK).astype(np.float32)
    wtile = jnp.tile(cmp_w, (1, past // CMP_BLOCK))
    kern = functools.partial(_nsa_sample_kernel, n_pages=n_pages, past=past, w_buf=w_buf, dec_seq=dec_seq)

    def page_spec(j):
        return pl.BlockSpec((None, None, 4 * HD, PAGE_SIZE), lambda b, pt: (layer, pt[b, j], 0, 0))

    def per_b(width):
        return pl.BlockSpec((None, SUBLANES, width), lambda b, pt: (b, 0, 0))

    const = lambda b, pt: (0, 0)
    grid_spec = pltpu.PrefetchScalarGridSpec(
        num_scalar_prefetch=1,
        grid=(nb,),
        in_specs=[per_b(C_WIDTH), per_b(4 * HD),
                  pl.BlockSpec((None, None, 2 * HD, w_buf), lambda b, pt: (layer, b, 0, 0)),
                  per_b(2 * HD), per_b(LANES),
                  pl.BlockSpec((2, past), const), pl.BlockSpec((past, LANES), const),
                  pl.BlockSpec((LANES, LANES), const), pl.BlockSpec((LANES, lk), const)]
                 + [page_spec(j) for j in range(n_pages)],
        out_specs=(per_b(C_WIDTH), pl.BlockSpec((None, 2 * HD, w_buf), lambda b, pt: (b, 0, 0))),
        scratch_shapes=[pltpu.VMEM((4 * HD, lk), F32), pltpu.VMEM((2 * HD, w_buf + PAGE_SIZE), F32)],
    )
    return pl.pallas_call(
        kern,
        grid_spec=grid_spec,
        out_shape=(jax.ShapeDtypeStruct((nb, SUBLANES, C_WIDTH), F32), jax.ShapeDtypeStruct((nb, 2 * HD, w_buf), F32)),
        compiler_params=_params(("parallel",)),
        name="nsa_sample",
    )(page_table, qc8, new8, win_buf, winnew8, gates8, wtile, jnp.asarray(group, dtype=BF16), jnp.asarray(pair),
      jnp.asarray(expand, dtype=BF16), *([cache] * n_pages))


def _merge_kernel(x_ref, oa_ref, za_ref, os_ref, zs_ref, oc_ref, zc_ref, w_ref, gf_ref, y_ref, *, final):
    mix = jnp.concatenate([oa_ref[...] * jax.nn.silu(za_ref[...]),
                           os_ref[...] * jax.nn.silu(zs_ref[...]),
                           oc_ref[...] * jax.nn.silu(zc_ref[...])], axis=1)
    y = x_ref[...] + _dot(mix.astype(BF16), w_ref[...])
    if final:
        ms = jnp.mean(y * y, axis=-1, keepdims=True)
        y = y * lax.rsqrt(ms + RMS_EPS) * gf_ref[...]
    y_ref[...] = y


def _merge(x, oa, za, os_, zs, oc, zc, w_out, final_g, tm, seq_len, os_time_major, final):
    n = x.shape[0]
    nt = seq_len // tm
    row = lambda i: (i, 0)
    os_spec = pl.BlockSpec((tm, S_WIDTH), (lambda i: (i % nt, i // nt)) if os_time_major else row)
    return pl.pallas_call(
        functools.partial(_merge_kernel, final=final),
        grid=(n // tm,),
        in_specs=[pl.BlockSpec((tm, D_MODEL), row), pl.BlockSpec((tm, A_WIDTH), row), pl.BlockSpec((tm, A_WIDTH), row),
                  os_spec, pl.BlockSpec((tm, S_WIDTH), row),
                  pl.BlockSpec((tm, C_WIDTH), row), pl.BlockSpec((tm, C_WIDTH), row),
                  pl.BlockSpec((D_MODEL, D_MODEL), lambda i: (0, 0)), pl.BlockSpec((1, D_MODEL), lambda i: (0, 0))],
        out_specs=pl.BlockSpec((tm, D_MODEL), row),
        out_shape=jax.ShapeDtypeStruct((n, D_MODEL), F32),
        compiler_params=_params(("parallel",)),
        name="merge",
    )(x, oa, za, os_, zs, oc, zc, w_out, final_g)


def _rope_tables(pos):
    half = HD // 2
    inv = ROPE_THETA ** (-jnp.arange(half, dtype=F32) / half)
    ang = pos.astype(F32)[:, None] * inv[None, :]
    cos = jnp.tile(jnp.cos(ang), (1, LANES // half))
    sin = jnp.tile(jnp.sin(ang), (1, LANES // half))
    lane = jnp.arange(LANES)
    sin = jnp.where((lane % HD) < half, -sin, sin)
    is_k = lane < HD
    return jnp.concatenate([cos, sin, jnp.where(is_k, cos, 1.0), jnp.where(is_k, sin, 0.0)], axis=1)


def _layer_weights(layer, norm_g, w_in, w_out, diff_lambda, diff_norm_g, ssm_A_re, ssm_A_im, ssm_log_dt,
                   ssm_B_re, ssm_B_im, ssm_C_re, ssm_C_im, ssm_D, ssm_w_glu, ssm_b_glu, nsa_cmp_w):
    eye = jnp.eye(S_GROUPS, dtype=F32)
    b_re = jnp.einsum("gpc,gh->gchp", ssm_B_re[layer], eye).reshape(S_WIDTH, GP)
    b_im = jnp.einsum("gpc,gh->gchp", ssm_B_im[layer], eye).reshape(S_WIDTH, GP)
    c_re = jnp.einsum("gcp,gh->gphc", ssm_C_re[layer], eye).reshape(GP, S_WIDTH)
    c_im = jnp.einsum("gcp,gh->gphc", ssm_C_im[layer], eye).reshape(GP, S_WIDTH)
    s5 = {
        "bbd": jnp.concatenate([b_re, b_im], axis=1).astype(BF16),
        "cbd": jnp.concatenate([c_re, -c_im], axis=0).astype(BF16),
        "d": ssm_D[layer].reshape(1, S_WIDTH),
        "a_re": ssm_A_re[layer].reshape(1, GP),
        "a_im": ssm_A_im[layer].reshape(1, GP),
        "log_dt": jnp.broadcast_to(ssm_log_dt[layer][:, None], (S_GROUPS, S_STATE)).reshape(1, GP),
        "w_glu": ssm_w_glu[layer].astype(BF16),
        "b_glu": ssm_b_glu[layer].reshape(1, S_WIDTH),
    }
    return {
        "norm_g": norm_g[layer].reshape(1, D_MODEL),
        "w_in": w_in,
        "layer": layer,
        "w_out": w_out[layer].astype(BF16),
        "diff_lambda": diff_lambda[layer],
        "diff_norm_g": diff_norm_g[layer].reshape(1, 2 * HD),
        "s5": s5,
        "wcol": jnp.broadcast_to(nsa_cmp_w[layer][:, :, None], (2, CMP_BLOCK, HD)),
        "cmp_w": nsa_cmp_w[layer],
        "lam_init": 0.8 - 0.6 * math.exp(-0.3 * layer),
    }


def _prompt_layer(x, lw, tab, nb, seq_len, final_g, final, dr_all):
    tm = min(256, seq_len)
    qa, dr, kvb, za, us, zs, qc, nsa, win, zc, gates, nsa_t, win_t, va_t = _inproj(
        x, lw["norm_g"], lw["w_in"], lw["layer"], tab, seq_len, tm, True, dr_all)
    oa = _diff_prompt(qa, kvb, va_t, lw["diff_lambda"], lw["diff_norm_g"], nb, seq_len, lw["lam_init"])
    zero = jnp.zeros((nb, GP), F32)
    os_, h_re, h_im = _s5(us, lw["s5"], zero, zero, nb, min(128, seq_len), True)
    oc = _nsa_prompt(qc, nsa, nsa_t, win, win_t, gates, lw["wcol"], lw["cmp_w"], nb, seq_len)
    y = _merge(x, oa, za, os_, zs, oc, zc, lw["w_out"], final_g, tm, seq_len, True, final)
    return y, dr, nsa_t, win_t, h_re, h_im


def _pad_rows(a, nb, t):
    a = a.reshape(nb, t, a.shape[-1])
    return jnp.pad(a, ((0, 0), (0, SUBLANES - t), (0, 0)))


def _sample_layer(x, lw, tab, layer, nb, t, page_table, cache_diff, cache_nsa, cache_win, h0_re, h0_im, final_g, final):
    n = nb * t
    qa, dr, kvb, za, us, zs, qc, nsa, win, zc, gates = _inproj(
        x, lw["norm_g"], lw["w_in"], lw["layer"], tab, t, n, False)
    oa8 = _diff_sample(page_table, cache_diff, layer, _pad_rows(qa, nb, t), _pad_rows(kvb, nb, t),
                       lw["diff_lambda"], lw["diff_norm_g"], lw["lam_init"])
    oa = oa8[:, :t].reshape(n, A_WIDTH)
    u_tb = us.reshape(nb, t, S_WIDTH).transpose(1, 0, 2).reshape(n, S_WIDTH)
    os_tb, h_re, h_im = _s5(u_tb, lw["s5"], h0_re, h0_im, nb, t, False)
    os_ = os_tb.reshape(t, nb, S_WIDTH).transpose(1, 0, 2).reshape(n, S_WIDTH)
    oc8, wks_t = _nsa_sample(page_table, cache_nsa, layer, _pad_rows(qc, nb, t), _pad_rows(nsa, nb, t), cache_win,
                             _pad_rows(win, nb, t), _pad_rows(gates, nb, t), lw["cmp_w"], t)
    oc = oc8[:, :t].reshape(n, C_WIDTH)
    y = _merge(x, oa, za, os_, zs, oc, zc, lw["w_out"], final_g, n, t, False, final)
    return y, dr, nsa, wks_t, h_re, h_im


def kernel(x_prompt, x_sample, cache_diff_kv, cache_nsa_kv, cache_win_kv, state_ssm_re, state_ssm_im, page_table, norm_g, w_in, w_out, diff_lambda, diff_norm_g, ssm_A_re, ssm_A_im, ssm_log_dt, ssm_B_re, ssm_B_im, ssm_C_re, ssm_C_im, ssm_D, ssm_w_glu, ssm_b_glu, nsa_cmp_w, final_norm_g):
    nbp, seq_len, _ = x_prompt.shape
    nbs, dec_seq, _ = x_sample.shape
    depth = w_in.shape[0]
    n_pool = cache_diff_kv.shape[1]
    past = page_table.shape[1] * PAGE_SIZE
    w_buf = cache_win_kv.shape[2]
    cache_diff = cache_diff_kv.reshape(depth, n_pool, PAGE_SIZE * 2 * A_HEADS, 2 * HD)
    cache_nsa = cache_nsa_kv.transpose(0, 1, 3, 4, 5, 2).reshape(depth, n_pool, 4 * HD, PAGE_SIZE)
    cache_win = cache_win_kv.transpose(0, 1, 3, 4, 5, 2).reshape(depth, nbs, 2 * HD, w_buf)
    tab_p = _rope_tables(jnp.arange(seq_len, dtype=jnp.int32))
    tab_s = jnp.tile(_rope_tables(past + jnp.arange(dec_seq, dtype=jnp.int32)), (nbs, 1))
    final_g = final_norm_g.reshape(1, D_MODEL)

    xp = x_prompt.reshape(nbp * seq_len, D_MODEL)
    xs = x_sample.reshape(nbs * dec_seq, D_MODEL)
    outs = [[] for _ in range(10)]
    w_keep = min(WINDOW, seq_len)
    w_in_b = _prep_w_in(w_in)
    dr_all = None
    for layer in range(depth):
        lw = _layer_weights(layer, norm_g, w_in_b, w_out, diff_lambda, diff_norm_g, ssm_A_re, ssm_A_im, ssm_log_dt,
                            ssm_B_re, ssm_B_im, ssm_C_re, ssm_C_im, ssm_D, ssm_w_glu, ssm_b_glu, nsa_cmp_w)
        final = layer == depth - 1
        xp, dr_all, nsa_t, win_t, h_re, h_im = _prompt_layer(xp, lw, tab_p, nbp, seq_len, final_g, final, dr_all)
        outs[2].append(nsa_t)
        outs[4].append(win_t[:, :, seq_len - w_keep:])
        outs[6].append(h_re.reshape(nbp, S_GROUPS, S_STATE))
        outs[7].append(h_im.reshape(nbp, S_GROUPS, S_STATE))
        xs, dr, nsa, wks_t, h_re, h_im = _sample_layer(
            xs, lw, tab_s, layer, nbs, dec_seq, page_table, cache_diff, cache_nsa, cache_win,
            state_ssm_re[layer].reshape(nbs, GP), state_ssm_im[layer].reshape(nbs, GP), final_g, final)
        outs[1].append(dr.reshape(nbs, dec_seq, 2, A_HEADS, 2 * HD))
        outs[3].append(nsa.reshape(nbs, dec_seq, 4, 1, HD))
        outs[5].append(wks_t)
        outs[8].append(h_re.reshape(nbs, S_GROUPS, S_STATE))
        outs[9].append(h_im.reshape(nbs, S_GROUPS, S_STATE))
    outs[0] = dr_all.reshape(depth, nbp, seq_len, 2, A_HEADS, 2 * HD)
    stacked = [o if i == 0 else jnp.stack(o) for i, o in enumerate(outs)]

    def rows_first(a, n_rows):
        return a.reshape(a.shape[0], a.shape[1], n_rows, 1, HD, a.shape[3]).transpose(0, 1, 5, 2, 3, 4)

    stacked[2] = rows_first(stacked[2], 4)
    stacked[4] = rows_first(stacked[4], 2)
    stacked[5] = rows_first(stacked[5], 2)
    return (xp.reshape(nbp, seq_len, D_MODEL), xs.reshape(nbs, dec_seq, D_MODEL), *stacked)
```

```python
import functools
import math

import jax
import jax.numpy as jnp
import numpy as np
from jax import lax
from jax.experimental import pallas as pl
from jax.experimental.pallas import tpu as pltpu

F32 = jnp.float32
BF16 = jnp.bfloat16

D_MODEL = 1024
PAGE_SIZE = 128
HD = 64
A_WIDTH = D_MODEL // 2
A_HEADS = A_WIDTH // (2 * HD)
S_WIDTH = D_MODEL // 4
S_CH = 16
S_GROUPS = S_WIDTH // S_CH
S_STATE = 64
C_WIDTH = D_MODEL - A_WIDTH - S_WIDTH
C_HEADS = C_WIDTH // HD
CMP_BLOCK = 32
SEL_BLOCK = 64
TOP_N = 8
WINDOW = 512
ROPE_THETA = 10000.0
RMS_EPS = 1e-6
MASK_VALUE = -1e30
FORCE_SCORE = 1e4
SCALE = HD ** -0.5
GP = S_GROUPS * S_STATE

LANES = 128
SUBLANES = 8
VMEM_LIMIT_BYTES = 56 * 1024 * 1024

COL_QA = 0
COL_KA = COL_QA + A_WIDTH
COL_VA = COL_KA + A_WIDTH
COL_ZA = COL_VA + A_WIDTH
COL_US = COL_ZA + A_WIDTH
COL_ZS = COL_US + S_WIDTH
COL_QC = COL_ZS + S_WIDTH
COL_KVC = COL_QC + C_WIDTH
COL_GC = COL_KVC + 6 * HD
N_GATES = 3 * C_HEADS
COL_ZC = COL_GC + N_GATES
IN_WIDTH = COL_ZC + C_WIDTH
IN_PAD = -(-IN_WIDTH // LANES) * LANES
W_PREP_TILE = 256

DIFF_Q_TILE = 512
DIFF_K_TILE = 512
DIFF_HEADS_PER_STEP = 2
Q_TILE = 256
SEL_TILE = 512
WIN_TILE = 256
HIGHEST = lax.Precision.HIGHEST


def _params(semantics):
    return pltpu.CompilerParams(dimension_semantics=semantics, vmem_limit_bytes=VMEM_LIMIT_BYTES)


def _dot_nt(a, b, precision=None):
    return lax.dot_general(a, b, (((1,), (1,)), ((), ())), preferred_element_type=F32, precision=precision)


def _dot(a, b, precision=None):
    return jnp.dot(a, b, preferred_element_type=F32, precision=precision)


def _dot_split(a, b):
    hi = a.astype(BF16)
    lo = (a - hi.astype(F32)).astype(BF16)
    return _dot(hi, b) + _dot(lo, b)


def _inproj_kernel(x_ref, g_ref, w_ref, tab_ref, *refs, carried):
    (qa_ref, dr_ref, kvb_ref, za_ref, us_ref, zs_ref, qc_ref, nsa_ref, win_ref, zc_ref, gate_ref,
     *transposed_refs) = refs[carried:]
    x = x_ref[...]
    ms = jnp.mean(x * x, axis=-1, keepdims=True)
    xn = (x * lax.rsqrt(ms + RMS_EPS) * g_ref[...]).astype(BF16)
    tm = x.shape[0]
    cos = tab_ref[:, 0:LANES]
    sin = tab_ref[:, LANES:2 * LANES]
    cos_kv = tab_ref[:, 2 * LANES:3 * LANES]
    sin_kv = tab_ref[:, 3 * LANES:4 * LANES]
    lane = lax.broadcasted_iota(jnp.int32, (tm, LANES), 1)
    first_half = (lane % HD) < (HD // 2)

    def rope(v, c, s):
        partner = jnp.where(first_half, pltpu.roll(v, LANES - HD // 2, 1), pltpu.roll(v, HD // 2, 1))
        return v * c + partner * s

    def proj(c0, width):
        return _dot(xn, w_ref[:, c0:c0 + width])

    def chunks(y):
        return [y[:, j * LANES:(j + 1) * LANES] for j in range(y.shape[1] // LANES)]

    y = proj(COL_QA, A_WIDTH)
    for j, c in enumerate(chunks(y)):
        qa_ref[:, j * LANES:(j + 1) * LANES] = (rope(c, cos, sin) * SCALE).astype(qa_ref.dtype)
    y = proj(COL_KA, A_WIDTH)
    for j, c in enumerate(chunks(y)):
        r = rope(c, cos, sin)
        dr_ref[pl.ds(j, tm, stride=2 * A_HEADS), :] = r
        kvb_ref[:, j * LANES:(j + 1) * LANES] = r.astype(BF16)
    y_va = proj(COL_VA, A_WIDTH)
    for j, c in enumerate(chunks(y_va)):
        dr_ref[pl.ds(A_HEADS + j, tm, stride=2 * A_HEADS), :] = c
    kvb_ref[:, A_WIDTH:2 * A_WIDTH] = y_va.astype(BF16)
    za_ref[...] = proj(COL_ZA, A_WIDTH)
    y = proj(COL_US, 2 * S_WIDTH)
    us_ref[...] = y[:, :S_WIDTH]
    zs_ref[...] = y[:, S_WIDTH:]
    y = proj(COL_QC, C_WIDTH)
    for j, c in enumerate(chunks(y)):
        qc_ref[:, j * LANES:(j + 1) * LANES] = rope(c, cos, sin) * SCALE
    y = proj(COL_KVC, 6 * HD)
    kv = [rope(c, cos_kv, sin_kv) for c in chunks(y)]
    nsa_ref[:, 0:LANES] = kv[0]
    nsa_ref[:, LANES:2 * LANES] = kv[1]
    win_ref[...] = kv[2]
    if transposed_refs:
        nsa_t_ref, win_t_ref, va_t_ref = transposed_refs
        nsa_t_ref[0:LANES, :] = kv[0].T
        nsa_t_ref[LANES:2 * LANES, :] = kv[1].T
        win_t_ref[...] = kv[2].T
        for j, c in enumerate(chunks(y_va)):
            va_t_ref[j * LANES:(j + 1) * LANES, :] = c.T.astype(BF16)
    y = proj(COL_GC, IN_PAD - COL_GC)
    gate_ref[...] = jax.nn.sigmoid(y[:, 0:LANES])
    zc_ref[...] = y[:, N_GATES:N_GATES + C_WIDTH]


def _prep_w_in_kernel(wt_ref, o_ref):
    for layer in range(o_ref.shape[0]):
        o_ref[layer] = wt_ref[:, layer, :].T.astype(BF16)


def _prep_w_in(w_in):
    depth = w_in.shape[0]
    wt = jnp.pad(w_in.transpose(2, 0, 1), ((0, IN_PAD - IN_WIDTH), (0, 0), (0, 0)))
    return pl.pallas_call(
        _prep_w_in_kernel,
        grid=(IN_PAD // W_PREP_TILE,),
        in_specs=[pl.BlockSpec((W_PREP_TILE, depth, D_MODEL), lambda i: (i, 0, 0))],
        out_specs=pl.BlockSpec((depth, D_MODEL, W_PREP_TILE), lambda i: (0, 0, i)),
        out_shape=jax.ShapeDtypeStruct((depth, D_MODEL, IN_PAD), BF16),
        compiler_params=_params(("parallel",)),
        name="prep_w_in",
    )(wt)


def _inproj(x, norm_g, w, layer, tab, seq_len, tm, prompt, carried=()):
    n = x.shape[0]
    nt = tab.shape[0] // tm
    grid = (n // tm,)
    row = lambda i: (i, 0)
    nb = n // seq_len
    if prompt:
        us_shape = jax.ShapeDtypeStruct((seq_len, nb * S_WIDTH), F32)
        us_spec = pl.BlockSpec((tm, S_WIDTH), lambda i: (i % nt, i // nt))
    else:
        us_shape = jax.ShapeDtypeStruct((n, S_WIDTH), F32)
        us_spec = pl.BlockSpec((tm, S_WIDTH), row)
    rows_per_token = 2 * A_HEADS
    depth = w.shape[0]
    if prompt:
        dr_shape = jax.ShapeDtypeStruct((depth, n * rows_per_token, 2 * HD), F32)
        dr_spec = pl.BlockSpec((None, tm * rows_per_token, 2 * HD), lambda i: (layer, i, 0))
    else:
        dr_shape = jax.ShapeDtypeStruct((n * rows_per_token, 2 * HD), F32)
        dr_spec = pl.BlockSpec((tm * rows_per_token, 2 * HD), row)
    out_shape = (
        jax.ShapeDtypeStruct((n, A_WIDTH), F32 if prompt else BF16),
        dr_shape,
        jax.ShapeDtypeStruct((n, 2 * A_WIDTH), BF16),
        jax.ShapeDtypeStruct((n, A_WIDTH), F32),
        us_shape,
        jax.ShapeDtypeStruct((n, S_WIDTH), F32),
        jax.ShapeDtypeStruct((n, C_WIDTH), F32),
        jax.ShapeDtypeStruct((n, 4 * HD), F32),
        jax.ShapeDtypeStruct((n, 2 * HD), F32),
        jax.ShapeDtypeStruct((n, C_WIDTH), F32),
        jax.ShapeDtypeStruct((n, LANES), F32),
    )
    out_specs = (
        pl.BlockSpec((tm, A_WIDTH), row), dr_spec,
        pl.BlockSpec((tm, 2 * A_WIDTH), row),
        pl.BlockSpec((tm, A_WIDTH), row), us_spec, pl.BlockSpec((tm, S_WIDTH), row),
        pl.BlockSpec((tm, C_WIDTH), row), pl.BlockSpec((tm, 4 * HD), row), pl.BlockSpec((tm, 2 * HD), row),
        pl.BlockSpec((tm, C_WIDTH), row), pl.BlockSpec((tm, LANES), row),
    )
    if prompt:
        out_shape += (jax.ShapeDtypeStruct((depth, nb, 4 * HD, seq_len), F32),
                      jax.ShapeDtypeStruct((depth, nb, 2 * HD, seq_len), F32),
                      jax.ShapeDtypeStruct((nb, A_WIDTH, seq_len), BF16))
        out_specs += (pl.BlockSpec((None, None, 4 * HD, tm), lambda i: (layer, i // nt, 0, i % nt)),
                      pl.BlockSpec((None, None, 2 * HD, tm), lambda i: (layer, i // nt, 0, i % nt)),
                      pl.BlockSpec((None, A_WIDTH, tm), lambda i: (i // nt, 0, i % nt)))
    carried = list(carried)
    return pl.pallas_call(
        functools.partial(_inproj_kernel, carried=len(carried)),
        grid=grid,
        in_specs=[pl.BlockSpec((tm, D_MODEL), row),
                  pl.BlockSpec((1, D_MODEL), lambda i: (0, 0)),
                  pl.BlockSpec((None, D_MODEL, IN_PAD), lambda i: (layer, 0, 0)),
                  pl.BlockSpec((tm, 4 * LANES), lambda i: (i % nt, 0))]
                 + [pl.BlockSpec(memory_space=pl.ANY) for _ in carried],
        out_specs=out_specs,
        out_shape=out_shape,
        input_output_aliases={4: 1, 5: 11, 6: 12} if carried else {},
        compiler_params=_params(("parallel",)),
        name="inproj",
    )(x, norm_g, w, tab, *carried)


def _flash_t_init(m_sc, l_sc, acc_sc):
    m_sc[...] = jnp.full(m_sc.shape, MASK_VALUE, F32)
    l_sc[...] = jnp.zeros(l_sc.shape, F32)
    acc_sc[...] = jnp.zeros(acc_sc.shape, F32)


def _flash_t_update(s, v_t, m_sc, l_sc, acc_sc):
    m_prev = m_sc[...]
    m_new = jnp.maximum(m_prev, jnp.max(s, axis=0, keepdims=True))
    alpha = jnp.exp(m_prev - m_new)
    p = jnp.exp(s - m_new)
    l_sc[...] = alpha * l_sc[...] + jnp.sum(p, axis=0, keepdims=True)
    acc_sc[...] = alpha * acc_sc[...] + _dot(v_t, p.astype(BF16))
    m_sc[...] = m_new


def _flash_t(n_plain, n_tail, chains, mask_plain, mask_tail, unroll=1):
    for scores, _, s_sc, m_sc, l_sc, acc_sc in chains:
        _flash_t_init(m_sc, l_sc, acc_sc)
        s_sc[0] = scores(0)

    def step(j, mask, prefetch):
        cur = [c[2][j % 2] for c in chains]
        if prefetch:
            for scores, _, s_sc, _, _, _ in chains:
                s_sc[(j + 1) % 2] = scores(j + 1)
        for s_cur, (_, values_t, _, m_sc, l_sc, acc_sc) in zip(cur, chains):
            if mask is not None:
                s_cur = mask(s_cur, j)
            _flash_t_update(s_cur, values_t(j), m_sc, l_sc, acc_sc)

    def body(jj, carry):
        for u in range(unroll):
            step(jj * unroll + u, mask_plain, True)
        return carry

    lax.fori_loop(0, n_plain // unroll, body, 0)
    for t in range(n_tail):
        step(n_plain + t, mask_tail, t + 1 < n_tail)


def _lambda(dl_ref, lam_init):
    dl = dl_ref[...]
    a = jnp.sum(dl[0:1] * dl[1:2], axis=-1, keepdims=True)
    b = jnp.sum(dl[2:3] * dl[3:4], axis=-1, keepdims=True)
    return jnp.exp(a) - jnp.exp(b) + lam_init


def _split_components(q):
    lane = lax.broadcasted_iota(jnp.int32, q.shape, 1)
    zero = jnp.zeros_like(q)
    return jnp.concatenate([jnp.where(lane < HD, q, zero), jnp.where(lane >= HD, q, zero)], axis=0)


def _head_norm(o, g, lam_init):
    ms = jnp.mean(o * o, axis=-1, keepdims=True)
    return o * lax.rsqrt(ms + RMS_EPS) * g * (1.0 - lam_init)


def _diff_prompt_kernel(dl_ref, g_ref, q_ref, k_ref, vt_ref, o_ref, s_sc, m_sc, l_sc, acc_sc, *, tq, tk, lam_init):
    qi = pl.program_id(2)
    heads = q_ref.shape[1] // (2 * HD)

    def causal(s, j):
        key = j * tk + lax.broadcasted_iota(jnp.int32, s.shape, 0)
        qry = lax.broadcasted_iota(jnp.int32, s.shape, 1)
        qry = qi * tq + jnp.where(qry >= tq, qry - tq, qry)
        return jnp.where(key <= qry, s, MASK_VALUE)

    def chain(h):
        cols = slice(h * 2 * HD, (h + 1) * 2 * HD)
        q_t = q_ref[:, cols].T
        row = lax.broadcasted_iota(jnp.int32, q_t.shape, 0)
        zero = jnp.zeros_like(q_t)
        q2_t = jnp.concatenate([jnp.where(row < HD, q_t, zero), jnp.where(row >= HD, q_t, zero)], axis=1).astype(BF16)

        def scores(j):
            return _dot(k_ref[pl.ds(pl.multiple_of(j * tk, tk), tk), cols], q2_t)

        def values_t(j):
            return vt_ref[h * 2 * HD:(h + 1) * 2 * HD, pl.ds(pl.multiple_of(j * tk, tk), tk)]

        return scores, values_t, s_sc.at[h], m_sc.at[h], l_sc.at[h], acc_sc.at[h]

    _flash_t(qi * (tq // tk), tq // tk, [chain(h) for h in range(heads)], None, causal, unroll=tq // tk)
    lam = _lambda(dl_ref, lam_init)
    for h in range(heads):
        o_all = acc_sc[h] / l_sc[h]
        o = (o_all[:, :tq] - lam * o_all[:, tq:]).T
        o_ref[:, h * 2 * HD:(h + 1) * 2 * HD] = _head_norm(o, g_ref[...], lam_init)


def _diff_prompt(qa, kvb, va_t, diff_lambda, diff_norm_g, nb, seq_len, lam_init):
    tq = min(DIFF_Q_TILE, seq_len)
    tk = min(DIFF_K_TILE, seq_len)
    nq = seq_len // tq
    hp = DIFF_HEADS_PER_STEP
    width = hp * 2 * HD
    kern = functools.partial(_diff_prompt_kernel, tq=tq, tk=tk, lam_init=lam_init)
    return pl.pallas_call(
        kern,
        grid=(nb, A_HEADS // hp, nq),
        in_specs=[pl.BlockSpec((4, HD), lambda b, h, i: (0, 0)),
                  pl.BlockSpec((1, 2 * HD), lambda b, h, i: (0, 0)),
                  pl.BlockSpec((tq, width), lambda b, h, i: (b * nq + i, h)),
                  pl.BlockSpec((seq_len, width), lambda b, h, i: (b, h)),
                  pl.BlockSpec((None, width, seq_len), lambda b, h, i: (b, h, 0))],
        out_specs=pl.BlockSpec((tq, width), lambda b, h, i: (b * nq + i, h)),
        out_shape=jax.ShapeDtypeStruct((nb * seq_len, A_WIDTH), F32),
        scratch_shapes=[pltpu.VMEM((hp, 2, tk, 2 * tq), F32), pltpu.VMEM((hp, 1, 2 * tq), F32),
                        pltpu.VMEM((hp, 1, 2 * tq), F32), pltpu.VMEM((hp, 2 * HD, 2 * tq), F32)],
        compiler_params=_params(("parallel", "parallel", "arbitrary")),
        name="diff_prompt",
    )(diff_lambda, diff_norm_g, qa, kvb, va_t)


def _diff_sample_kernel(pt_ref, dl_ref, g_ref, q_ref, new_ref, *rest, n_pages, past, lam_init):
    del pt_ref
    pages = rest[:n_pages]
    o_ref = rest[n_pages]
    kv_sc = rest[n_pages + 1]
    rows_per_token = 2 * A_HEADS
    for j in range(n_pages):
        for c in range(rows_per_token):
            blk = pages[j][pl.ds(c, PAGE_SIZE, stride=rows_per_token), :]
            kv_sc[j * PAGE_SIZE:(j + 1) * PAGE_SIZE, c * 2 * HD:(c + 1) * 2 * HD] = blk.astype(BF16)
    new8 = new_ref[...].astype(F32)
    pad = jnp.zeros((PAGE_SIZE - SUBLANES, 2 * A_WIDTH), F32)
    kv_sc[past:past + PAGE_SIZE, :] = jnp.concatenate([new8, pad], axis=0).astype(BF16)
    lk = past + PAGE_SIZE
    q = q_ref[...]
    lam = _lambda(dl_ref, lam_init)
    zero = jnp.zeros((2 * SUBLANES, 2 * HD), BF16)
    q_rows = []
    for h in range(A_HEADS):
        q2 = _split_components(q[:, h * 2 * HD:(h + 1) * 2 * HD])
        q_rows.append(jnp.concatenate([zero] * h + [q2] + [zero] * (A_HEADS - 1 - h), axis=1))
    q_bd = jnp.concatenate(q_rows, axis=0)
    nrow = 2 * SUBLANES * A_HEADS
    kpos = lax.broadcasted_iota(jnp.int32, (nrow, lk), 1)
    tok = lax.broadcasted_iota(jnp.int32, (nrow, lk), 0) % SUBLANES
    mask = kpos <= past + tok
    s = jnp.where(mask, _dot_nt(q_bd, kv_sc[:, 0:A_WIDTH]), MASK_VALUE)
    m = jnp.max(s, axis=1, keepdims=True)
    e = jnp.where(mask, jnp.exp(s - m), 0.0)
    p = e / jnp.maximum(jnp.sum(e, axis=1, keepdims=True), 1e-30)
    pv = _dot(p.astype(BF16), kv_sc[:, A_WIDTH:2 * A_WIDTH])
    for h in range(A_HEADS):
        blk = pv[h * 2 * SUBLANES:(h + 1) * 2 * SUBLANES, h * 2 * HD:(h + 1) * 2 * HD]
        o = blk[:SUBLANES] - lam * blk[SUBLANES:]
        o_ref[:, h * 2 * HD:(h + 1) * 2 * HD] = _head_norm(o, g_ref[...], lam_init)


def _diff_sample(page_table, cache, layer, qa8, new8, diff_lambda, diff_norm_g, lam_init):
    nb, n_pages = page_table.shape
    past = n_pages * PAGE_SIZE
    kern = functools.partial(_diff_sample_kernel, n_pages=n_pages, past=past, lam_init=lam_init)

    def page_spec(j):
        return pl.BlockSpec((None, None, PAGE_SIZE * 2 * A_HEADS, 2 * HD), lambda b, pt: (layer, pt[b, j], 0, 0))

    grid_spec = pltpu.PrefetchScalarGridSpec(
        num_scalar_prefetch=1,
        grid=(nb,),
        in_specs=[pl.BlockSpec((4, HD), lambda b, pt: (0, 0)),
                  pl.BlockSpec((1, 2 * HD), lambda b, pt: (0, 0)),
                  pl.BlockSpec((None, SUBLANES, A_WIDTH), lambda b, pt: (b, 0, 0)),
                  pl.BlockSpec((None, SUBLANES, 2 * A_WIDTH), lambda b, pt: (b, 0, 0))]
                 + [page_spec(j) for j in range(n_pages)],
        out_specs=pl.BlockSpec((None, SUBLANES, A_WIDTH), lambda b, pt: (b, 0, 0)),
        scratch_shapes=[pltpu.VMEM((past + PAGE_SIZE, 2 * A_WIDTH), BF16)],
    )
    return pl.pallas_call(
        kern,
        grid_spec=grid_spec,
        out_shape=jax.ShapeDtypeStruct((nb, SUBLANES, A_WIDTH), F32),
        compiler_params=_params(("parallel",)),
        name="diff_sample",
    )(page_table, diff_lambda, diff_norm_g, qa8, new8, *([cache] * n_pages))


def _s5_kernel(u_ref, bbd_ref, cbd_ref, d_ref, are_ref, aim_ref, ldt_ref, wglu_ref, bglu_ref, h0re_ref, h0im_ref,
               y_ref, hre_ref, him_ref, buf, hs_re, hs_im, *row_scratch, nb, tc):
    @pl.when(pl.program_id(0) == 0)
    def _():
        hs_re[...] = h0re_ref[...]
        hs_im[...] = h0im_ref[...]

    a_re = are_ref[...]
    a_im = aim_ref[...]
    dt = jnp.exp(ldt_ref[...])
    mag = jnp.exp(dt * a_re)
    ab_re = mag * jnp.cos(dt * a_im)
    ab_im = mag * jnp.sin(dt * a_im)
    den = a_re * a_re + a_im * a_im
    num_re = ab_re - 1.0
    c_re = (num_re * a_re + ab_im * a_im) / den
    c_im = (ab_im * a_re - num_re * a_im) / den
    full = (SUBLANES, GP)
    ab_re, ab_im, c_re, c_im = [jnp.broadcast_to(v, full) for v in (ab_re, ab_im, c_re, c_im)]

    if row_scratch:
        rows_sc, = row_scratch
        halves = S_WIDTH // LANES
        for b in range(nb):
            for h in range(halves):
                c0 = b * S_WIDTH + h * LANES
                rows_sc[h, pl.ds(b, tc, stride=nb), :] = u_ref[:, c0:c0 + LANES]
        u = jnp.concatenate([rows_sc[h] for h in range(halves)], axis=1)
    else:
        u = u_ref[...]
    buf[...] = _dot(u.astype(BF16), bbd_ref[...])

    def group(rg, carry):
        r0 = pl.multiple_of(rg * SUBLANES, SUBLANES)

        def step(t, h):
            hr, hi = h
            row = pl.multiple_of(t * nb + r0, SUBLANES)
            br = buf[pl.ds(row, SUBLANES), 0:GP]
            bi = buf[pl.ds(row, SUBLANES), GP:2 * GP]
            nr = ab_re * hr - ab_im * hi + (c_re * br - c_im * bi)
            ni = ab_re * hi + ab_im * hr + (c_re * bi + c_im * br)
            buf[pl.ds(row, SUBLANES), 0:GP] = nr
            buf[pl.ds(row, SUBLANES), GP:2 * GP] = ni
            return nr, ni

        hr, hi = lax.fori_loop(0, tc, step, (hs_re[pl.ds(r0, SUBLANES), :], hs_im[pl.ds(r0, SUBLANES), :]))
        hs_re[pl.ds(r0, SUBLANES), :] = hr
        hs_im[pl.ds(r0, SUBLANES), :] = hi
        return carry

    lax.fori_loop(0, nb // SUBLANES, group, 0)
    y = _dot(buf[...].astype(BF16), cbd_ref[...]) + d_ref[...] * u
    y = jax.nn.gelu(y)
    z = _dot(y.astype(BF16), wglu_ref[...]) + bglu_ref[...]
    y = y * jax.nn.sigmoid(z)
    if row_scratch:
        for h in range(halves):
            rows_sc[h] = y[:, h * LANES:(h + 1) * LANES]
        for b in range(nb):
            for h in range(halves):
                c0 = b * S_WIDTH + h * LANES
                y_ref[:, c0:c0 + LANES] = rows_sc[h, pl.ds(b, tc, stride=nb), :]
    else:
        y_ref[...] = y
    hre_ref[...] = hs_re[...]
    him_ref[...] = hs_im[...]


def _s5(u, sp, h0_re, h0_im, nb, tc, batch_on_lanes):
    rc = tc * nb
    steps = u.shape[0] * u.shape[1] // (rc * S_WIDTH)
    const = lambda i: (0, 0)
    kern = functools.partial(_s5_kernel, nb=nb, tc=tc)
    io_block = (tc, nb * S_WIDTH) if batch_on_lanes else (rc, S_WIDTH)
    row_scratch = [pltpu.VMEM((S_WIDTH // LANES, rc, LANES), F32)] if batch_on_lanes else []
    return pl.pallas_call(
        kern,
        grid=(steps,),
        in_specs=[pl.BlockSpec(io_block, lambda i: (i, 0)),
                  pl.BlockSpec((S_WIDTH, 2 * GP), const), pl.BlockSpec((2 * GP, S_WIDTH), const),
                  pl.BlockSpec((1, S_WIDTH), const),
                  pl.BlockSpec((1, GP), const), pl.BlockSpec((1, GP), const), pl.BlockSpec((1, GP), const),
                  pl.BlockSpec((S_WIDTH, S_WIDTH), const), pl.BlockSpec((1, S_WIDTH), const),
                  pl.BlockSpec((nb, GP), const), pl.BlockSpec((nb, GP), const)],
        out_specs=(pl.BlockSpec(io_block, lambda i: (i, 0)),
                   pl.BlockSpec((nb, GP), const), pl.BlockSpec((nb, GP), const)),
        out_shape=(jax.ShapeDtypeStruct(u.shape, F32),
                   jax.ShapeDtypeStruct((nb, GP), F32), jax.ShapeDtypeStruct((nb, GP), F32)),
        scratch_shapes=[pltpu.VMEM((rc, 2 * GP), F32), pltpu.VMEM((nb, GP), F32), pltpu.VMEM((nb, GP), F32)]
                       + row_scratch,
        compiler_params=_params(("arbitrary",)),
        name="s5",
    )(u, sp["bbd"], sp["cbd"], sp["d"], sp["a_re"], sp["a_im"], sp["log_dt"], sp["w_glu"], sp["b_glu"], h0_re, h0_im)


def _masked_softmax(s, mask):
    s = jnp.where(mask, s, MASK_VALUE)
    m = jnp.max(s, axis=-1, keepdims=True)
    e = jnp.where(mask, jnp.exp(s - m), 0.0)
    return e / jnp.maximum(jnp.sum(e, axis=-1, keepdims=True), 1e-30)


def _stack_heads(q):
    return jnp.concatenate([q[:, g * HD:(g + 1) * HD] for g in range(C_HEADS)], axis=0)


def _compress(rows, wcol):
    n = rows.shape[0] // CMP_BLOCK
    return jnp.sum(rows.reshape(n, CMP_BLOCK, HD) * wcol[None], axis=1)


def _select_blocks(imp, t, ns, axis):
    blk = lax.broadcasted_iota(jnp.int32, imp.shape, axis)
    cur = t // SEL_BLOCK
    valid = blk <= cur
    forced = jnp.where(blk == 0, 1.0, jnp.where(blk == cur, 1.0, jnp.where(blk == cur - 1, 1.0, 0.0)))
    score = jnp.where(valid, imp + FORCE_SCORE * forced, -1.0)
    score = jnp.where(blk < ns, score, -2.0)
    rank = jnp.zeros(imp.shape, F32)
    for s2 in range(ns):
        other = score[s2:s2 + 1, :] if axis == 0 else score[:, s2:s2 + 1]
        ahead = jnp.where(other > score, 1.0, jnp.where(other == score, jnp.where(blk > s2, 1.0, 0.0), 0.0))
        rank = rank + ahead
    chosen = jnp.where(rank < float(min(TOP_N, ns)), 1.0, 0.0)
    return jnp.where(blk < ns, chosen, 0.0)


def _combine(o_cmp, o_sel, o_win, gates, r):
    outs = []
    for g in range(C_HEADS):
        sl = slice(g * r, (g + 1) * r)
        outs.append(gates[:, 3 * g:3 * g + 1] * o_cmp[sl] + gates[:, 3 * g + 1:3 * g + 2] * o_sel[sl]
                    + gates[:, 3 * g + 2:3 * g + 3] * o_win[sl])
    return jnp.concatenate(outs, axis=1)


def _nsa_prompt_kernel(q_ref, nsa_ref, nsat_ref, win_ref, wint_ref, gate_ref, wcol_ref, wtile_ref, group_ref,
                       pairt_ref, et_ref, o_ref,
                       kc_sc, vct_sc, selk_sc, selvt_sc, wink_sc, winvt_sc, s_sel_sc, s_win_sc,
                       m_sc, l_sc, acc_sc, m2_sc, l2_sc, acc2_sc, *, seq_len):
    i = pl.program_id(1)
    nc = seq_len // CMP_BLOCK
    ns = seq_len // SEL_BLOCK
    nsp = min(LANES, -(-ns // SUBLANES) * SUBLANES)

    @pl.when(i == 0)
    def _():
        r = nsa_ref[...]
        kc_sc[...] = jnp.zeros(kc_sc.shape, F32)
        kc_sc[0:nc, :] = _compress(r[:, 0:HD], wcol_ref[0])
        vct_sc[...] = _dot_split(nsat_ref[HD:2 * HD, :] * wtile_ref[1:2, :], group_ref[...])
        selk_sc[...] = r[:, 2 * HD:3 * HD].astype(BF16)
        selvt_sc[...] = nsat_ref[3 * HD:4 * HD, :].astype(BF16)
        wink_sc[...] = win_ref[:, 0:HD].astype(BF16)
        winvt_sc[...] = wint_ref[HD:2 * HD, :].astype(BF16)

    q_t = q_ref[...].T
    qs_t = jnp.concatenate([q_t[g * HD:(g + 1) * HD] for g in range(C_HEADS)], axis=1)
    qb_t = qs_t.astype(BF16)
    t_q = i * Q_TILE + lax.broadcasted_iota(jnp.int32, (1, Q_TILE), 1)
    t_row = jnp.concatenate([t_q] * C_HEADS, axis=1)

    sc = _dot(kc_sc[...], qs_t, precision=HIGHEST)
    n_idx = lax.broadcasted_iota(jnp.int32, sc.shape, 0)
    cmask = ((n_idx + 1) * CMP_BLOCK - 1 <= t_row) & (n_idx < nc)
    sc = jnp.where(cmask, sc, MASK_VALUE)
    e = jnp.where(cmask, jnp.exp(sc - jnp.max(sc, axis=0, keepdims=True)), 0.0)
    p_t = e / jnp.maximum(jnp.sum(e, axis=0, keepdims=True), 1e-30)
    o_cmp_t = _dot(vct_sc[...], p_t, precision=HIGHEST)
    psum_t = p_t[:, 0:Q_TILE]
    for g in range(1, C_HEADS):
        psum_t = psum_t + p_t[:, g * Q_TILE:(g + 1) * Q_TILE]
    imp_t = _dot(pairt_ref[...], psum_t, precision=HIGHEST)
    chosen_t = _select_blocks(imp_t[0:nsp], t_q, ns, 0)
    if nsp < LANES:
        chosen_t = jnp.concatenate([chosen_t, jnp.zeros((LANES - nsp, Q_TILE), F32)], axis=0)
    chosen_b = chosen_t.astype(BF16)

    def sel_off(j):
        return pl.multiple_of(j * SEL_TILE, SEL_TILE)

    def sel_bias(sv, j):
        x = _dot(et_ref[pl.ds(sel_off(j), SEL_TILE), :], chosen_b)
        bias = (x - 1.0) * (-MASK_VALUE)
        return sv + jnp.concatenate([bias] * C_HEADS, axis=1)

    def sel_last(sv, j):
        key = j * SEL_TILE + lax.broadcasted_iota(jnp.int32, sv.shape, 0)
        return jnp.where(key <= t_row, sel_bias(sv, j), MASK_VALUE)

    _flash_t(((i + 1) * Q_TILE - 1) // SEL_TILE, 1,
             [(lambda j: _dot(selk_sc[pl.ds(sel_off(j), SEL_TILE), :], qb_t),
               lambda j: selvt_sc[:, pl.ds(sel_off(j), SEL_TILE)], s_sel_sc, m_sc, l_sc, acc_sc)],
             sel_bias, sel_last)
    o_sel_t = acc_sc[...] / jnp.maximum(l_sc[...], 1e-30)

    q_tiles = Q_TILE // WIN_TILE
    first = jnp.maximum(i * q_tiles - WINDOW // WIN_TILE, 0)

    def win_off(j):
        return pl.multiple_of((first + j) * WIN_TILE, WIN_TILE)

    def win_mask(sv, j):
        key = (first + j) * WIN_TILE + lax.broadcasted_iota(jnp.int32, sv.shape, 0)
        return jnp.where((key <= t_row) & (key > t_row - WINDOW), sv, MASK_VALUE)

    _flash_t((i + 1) * q_tiles - first - 1, 1,
             [(lambda j: _dot(wink_sc[pl.ds(win_off(j), WIN_TILE), :], qb_t),
               lambda j: winvt_sc[:, pl.ds(win_off(j), WIN_TILE)], s_win_sc, m2_sc, l2_sc, acc2_sc)],
             win_mask, win_mask)
    o_win_t = acc2_sc[...] / jnp.maximum(l2_sc[...], 1e-30)

    g_t = gate_ref[...].T
    outs = []
    for g in range(C_HEADS):
        cs = slice(g * Q_TILE, (g + 1) * Q_TILE)
        outs.append(g_t[3 * g:3 * g + 1] * o_cmp_t[:, cs] + g_t[3 * g + 1:3 * g + 2] * o_sel_t[:, cs]
                    + g_t[3 * g + 2:3 * g + 3] * o_win_t[:, cs])
    o_ref[...] = jnp.concatenate(outs, axis=0).T


def _nsa_prompt(qc, nsa, nsa_t, win, win_t, layer, gates, wcol, cmp_w, nb, seq_len):
    nqb = seq_len // Q_TILE
    nc = seq_len // CMP_BLOCK
    assert nc <= LANES and seq_len % SEL_TILE == 0 and Q_TILE <= SEL_TILE and seq_len % Q_TILE == 0
    blocks = np.arange(LANES)
    pair_t = ((blocks[None, :] // (SEL_BLOCK // CMP_BLOCK) == blocks[:, None]) & (blocks[None, :] < nc)).astype(np.float32)
    expand_t = (np.arange(seq_len)[:, None] // SEL_BLOCK == blocks[None, :]).astype(np.float32)
    group = (np.arange(seq_len)[:, None] // CMP_BLOCK == blocks[None, :]).astype(np.float32)
    wtile = jnp.tile(cmp_w, (1, nc))
    kern = functools.partial(_nsa_prompt_kernel, seq_len=seq_len)
    n = C_HEADS * Q_TILE
    per_b = lambda b, i: (b, 0)
    per_b3 = lambda b, i: (layer, b, 0, 0)
    const = lambda b, i: (0, 0)
    return pl.pallas_call(
        kern,
        grid=(nb, nqb),
        in_specs=[pl.BlockSpec((Q_TILE, C_WIDTH), lambda b, i: (b * nqb + i, 0)),
                  pl.BlockSpec((seq_len, 4 * HD), per_b), pl.BlockSpec((None, None, 4 * HD, seq_len), per_b3),
                  pl.BlockSpec((seq_len, 2 * HD), per_b), pl.BlockSpec((None, None, 2 * HD, seq_len), per_b3),
                  pl.BlockSpec((Q_TILE, LANES), lambda b, i: (b * nqb + i, 0)),
                  pl.BlockSpec((2, CMP_BLOCK, HD), lambda b, i: (0, 0, 0)),
                  pl.BlockSpec((2, seq_len), const), pl.BlockSpec((seq_len, LANES), const),
                  pl.BlockSpec((LANES, LANES), const), pl.BlockSpec((seq_len, LANES), const)],
        out_specs=pl.BlockSpec((Q_TILE, C_WIDTH), lambda b, i: (b * nqb + i, 0)),
        out_shape=jax.ShapeDtypeStruct((nb * seq_len, C_WIDTH), F32),
        scratch_shapes=[pltpu.VMEM((LANES, HD), F32), pltpu.VMEM((HD, LANES), F32),
                        pltpu.VMEM((seq_len, HD), BF16), pltpu.VMEM((HD, seq_len), BF16),
                        pltpu.VMEM((seq_len, HD), BF16), pltpu.VMEM((HD, seq_len), BF16),
                        pltpu.VMEM((2, SEL_TILE, n), F32), pltpu.VMEM((2, WIN_TILE, n), F32),
                        pltpu.VMEM((1, n), F32), pltpu.VMEM((1, n), F32), pltpu.VMEM((HD, n), F32),
                        pltpu.VMEM((1, n), F32), pltpu.VMEM((1, n), F32), pltpu.VMEM((HD, n), F32)],
        compiler_params=_params(("parallel", "arbitrary")),
        name="nsa_prompt",
    )(qc, nsa, nsa_t, win, win_t, gates, wcol, wtile, jnp.asarray(group, dtype=BF16), jnp.asarray(pair_t),
      jnp.asarray(expand_t, dtype=BF16))


def _nsa_sample_kernel(pt_ref, q_ref, new_ref, wint_ref, winnew_ref, gate_ref, wtile_ref, group_ref, pair_ref, e_ref,
                       *rest, n_pages, past, w_buf, dec_seq):
    del pt_ref
    pages = rest[:n_pages]
    o_ref, wks_ref = rest[n_pages:n_pages + 2]
    rows_sc, win_sc = rest[n_pages + 2:]
    lk = past + PAGE_SIZE
    nc = past // CMP_BLOCK
    ns = -(-(past + dec_seq) // SEL_BLOCK)
    for j in range(n_pages):
        rows_sc[:, j * PAGE_SIZE:(j + 1) * PAGE_SIZE] = pages[j][...]
    pad4 = jnp.zeros((PAGE_SIZE - SUBLANES, 4 * HD), F32)
    rows_sc[:, past:lk] = jnp.concatenate([new_ref[...], pad4], axis=0).T
    pad2 = jnp.zeros((PAGE_SIZE - SUBLANES, 2 * HD), F32)
    win_sc[:, 0:w_buf] = wint_ref[...]
    win_sc[:, w_buf:w_buf + PAGE_SIZE] = jnp.concatenate([winnew_ref[...], pad2], axis=0).T

    r = rows_sc[...]
    kc_t = _dot_split(r[0:HD, 0:past] * wtile_ref[0:1, :], group_ref[...])
    vc_t = _dot_split(r[HD:2 * HD, 0:past] * wtile_ref[1:2, :], group_ref[...])
    qs = _stack_heads(q_ref[...])
    qb = qs.astype(BF16)
    nrow = C_HEADS * SUBLANES
    t1 = past + lax.broadcasted_iota(jnp.int32, (SUBLANES, 1), 0)
    t4 = jnp.concatenate([t1] * C_HEADS, axis=0)

    sc = _dot(qs, kc_t, precision=HIGHEST)
    n_idx = lax.broadcasted_iota(jnp.int32, sc.shape, 1)
    p_cmp = _masked_softmax(sc, ((n_idx + 1) * CMP_BLOCK - 1 <= t4) & (n_idx < nc))
    o_cmp = _dot_nt(p_cmp, vc_t, precision=HIGHEST)
    psum = p_cmp[0:SUBLANES]
    for g in range(1, C_HEADS):
        psum = psum + p_cmp[g * SUBLANES:(g + 1) * SUBLANES]
    imp = _dot(psum, pair_ref[...], precision=HIGHEST)
    chosen = _select_blocks(imp, t1, ns, 1)
    sx = _dot(chosen.astype(BF16), e_ref[...])
    sx = jnp.concatenate([sx] * C_HEADS, axis=0)
    kpos = lax.broadcasted_iota(jnp.int32, (nrow, lk), 1)
    smask = (sx > 0.5) & (kpos <= t4)
    p_sel = _masked_softmax(_dot(qb, r[2 * HD:3 * HD, :].astype(BF16)), smask)
    o_sel = _dot_nt(p_sel.astype(BF16), r[3 * HD:4 * HD, :].astype(BF16))

    w = win_sc[...]
    wpos = past - w_buf + lax.broadcasted_iota(jnp.int32, (nrow, w_buf + PAGE_SIZE), 1)
    wmask = (wpos <= t4) & (wpos > t4 - WINDOW) & (wpos >= 0)
    p_win = _masked_softmax(_dot(qb, w[0:HD, :].astype(BF16)), wmask)
    o_win = _dot_nt(p_win.astype(BF16), w[HD:2 * HD, :].astype(BF16))
    o_ref[...] = _combine(o_cmp, o_sel, o_win, gate_ref[...], SUBLANES)
    wks_ref[...] = pltpu.roll(w, w_buf + PAGE_SIZE - dec_seq, 1)[:, 0:w_buf]


def _nsa_sample(page_table, cache, layer, qc8, new8, win_buf, winnew8, gates8, cmp_w, dec_seq):
    nb, n_pages = page_table.shape
    past = n_pages * PAGE_SIZE
    w_buf = win_buf.shape[3]
    lk = past + PAGE_SIZE
    assert past % CMP_BLOCK == 0 and dec_seq < CMP_BLOCK and dec_seq <= SUBLANES
    nc = past // CMP_BLOCK
    assert nc <= LANES
    blocks = np.arange(LANES)[None, :]
    group = (np.arange(past)[:, None] // CMP_BLOCK == blocks).astype(np.float32)
    pair = ((np.arange(LANES)[:, None] // (SEL_BLOCK // CMP_BLOCK) == blocks)
            & (np.arange(LANES)[:, None] < nc)).astype(np.float32)
    expand = (np.arange(LANES)[:, None] == np.arange(lk)[None, :] // SEL_BLOCK).astype(np.float32)
    wtile = jnp.tile(cmp_w, (1, past // CMP_BLOCK))
    kern = functools.partial(_nsa_sample_kernel, n_pages=n_pages, past=past, w_buf=w_buf, dec_seq=dec_seq)

    def page_spec(j):
        return pl.BlockSpec((None, None, 4 * HD, PAGE_SIZE), lambda b, pt: (layer, pt[b, j], 0, 0))

    def per_b(width):
        return pl.BlockSpec((None, SUBLANES, width), lambda b, pt: (b, 0, 0))

    const = lambda b, pt: (0, 0)
    grid_spec = pltpu.PrefetchScalarGridSpec(
        num_scalar_prefetch=1,
        grid=(nb,),
        in_specs=[per_b(C_WIDTH), per_b(4 * HD),
                  pl.BlockSpec((None, None, 2 * HD, w_buf), lambda b, pt: (layer, b, 0, 0)),
                  per_b(2 * HD), per_b(LANES),
                  pl.BlockSpec((2, past), const), pl.BlockSpec((past, LANES), const),
                  pl.BlockSpec((LANES, LANES), const), pl.BlockSpec((LANES, lk), const)]
                 + [page_spec(j) for j in range(n_pages)],
        out_specs=(per_b(C_WIDTH), pl.BlockSpec((None, 2 * HD, w_buf), lambda b, pt: (b, 0, 0))),
        scratch_shapes=[pltpu.VMEM((4 * HD, lk), F32), pltpu.VMEM((2 * HD, w_buf + PAGE_SIZE), F32)],
    )
    return pl.pallas_call(
        kern,
        grid_spec=grid_spec,
        out_shape=(jax.ShapeDtypeStruct((nb, SUBLANES, C_WIDTH), F32), jax.ShapeDtypeStruct((nb, 2 * HD, w_buf), F32)),
        compiler_params=_params(("parallel",)),
        name="nsa_sample",
    )(page_table, qc8, new8, win_buf, winnew8, gates8, wtile, jnp.asarray(group, dtype=BF16), jnp.asarray(pair),
      jnp.asarray(expand, dtype=BF16), *([cache] * n_pages))


def _merge_kernel(x_ref, oa_ref, za_ref, os_ref, zs_ref, oc_ref, zc_ref, w_ref, gf_ref, y_ref, *, final):
    mix = jnp.concatenate([oa_ref[...] * jax.nn.silu(za_ref[...]),
                           os_ref[...] * jax.nn.silu(zs_ref[...]),
                           oc_ref[...] * jax.nn.silu(zc_ref[...])], axis=1)
    y = x_ref[...] + _dot(mix.astype(BF16), w_ref[...])
    if final:
        ms = jnp.mean(y * y, axis=-1, keepdims=True)
        y = y * lax.rsqrt(ms + RMS_EPS) * gf_ref[...]
    y_ref[...] = y


def _merge(x, oa, za, os_, zs, oc, zc, w_out, final_g, tm, seq_len, os_time_major, final):
    n = x.shape[0]
    nt = seq_len // tm
    row = lambda i: (i, 0)
    os_spec = pl.BlockSpec((tm, S_WIDTH), (lambda i: (i % nt, i // nt)) if os_time_major else row)
    return pl.pallas_call(
        functools.partial(_merge_kernel, final=final),
        grid=(n // tm,),
        in_specs=[pl.BlockSpec((tm, D_MODEL), row), pl.BlockSpec((tm, A_WIDTH), row), pl.BlockSpec((tm, A_WIDTH), row),
                  os_spec, pl.BlockSpec((tm, S_WIDTH), row),
                  pl.BlockSpec((tm, C_WIDTH), row), pl.BlockSpec((tm, C_WIDTH), row),
                  pl.BlockSpec((D_MODEL, D_MODEL), lambda i: (0, 0)), pl.BlockSpec((1, D_MODEL), lambda i: (0, 0))],
        out_specs=pl.BlockSpec((tm, D_MODEL), row),
        out_shape=jax.ShapeDtypeStruct((n, D_MODEL), F32),
        compiler_params=_params(("parallel",)),
        name="merge",
    )(x, oa, za, os_, zs, oc, zc, w_out, final_g)


def _rope_tables(pos):
    half = HD // 2
    inv = ROPE_THETA ** (-jnp.arange(half, dtype=F32) / half)
    ang = pos.astype(F32)[:, None] * inv[None, :]
    cos = jnp.tile(jnp.cos(ang), (1, LANES // half))
    sin = jnp.tile(jnp.sin(ang), (1, LANES // half))
    lane = jnp.arange(LANES)
    sin = jnp.where((lane % HD) < half, -sin, sin)
    is_k = lane < HD
    return jnp.concatenate([cos, sin, jnp.where(is_k, cos, 1.0), jnp.where(is_k, sin, 0.0)], axis=1)


def _layer_weights(layer, norm_g, w_in, w_out, diff_lambda, diff_norm_g, ssm_A_re, ssm_A_im, ssm_log_dt,
                   ssm_B_re, ssm_B_im, ssm_C_re, ssm_C_im, ssm_D, ssm_w_glu, ssm_b_glu, nsa_cmp_w):
    eye = jnp.eye(S_GROUPS, dtype=F32)
    b_re = jnp.einsum("gpc,gh->gchp", ssm_B_re[layer], eye).reshape(S_WIDTH, GP)
    b_im = jnp.einsum("gpc,gh->gchp", ssm_B_im[layer], eye).reshape(S_WIDTH, GP)
    c_re = jnp.einsum("gcp,gh->gphc", ssm_C_re[layer], eye).reshape(GP, S_WIDTH)
    c_im = jnp.einsum("gcp,gh->gphc", ssm_C_im[layer], eye).reshape(GP, S_WIDTH)
    s5 = {
        "bbd": jnp.concatenate([b_re, b_im], axis=1).astype(BF16),
        "cbd": jnp.concatenate([c_re, -c_im], axis=0).astype(BF16),
        "d": ssm_D[layer].reshape(1, S_WIDTH),
        "a_re": ssm_A_re[layer].reshape(1, GP),
        "a_im": ssm_A_im[layer].reshape(1, GP),
        "log_dt": jnp.broadcast_to(ssm_log_dt[layer][:, None], (S_GROUPS, S_STATE)).reshape(1, GP),
        "w_glu": ssm_w_glu[layer].astype(BF16),
        "b_glu": ssm_b_glu[layer].reshape(1, S_WIDTH),
    }
    return {
        "norm_g": norm_g[layer].reshape(1, D_MODEL),
        "w_in": w_in,
        "layer": layer,
        "w_out": w_out[layer].astype(BF16),
        "diff_lambda": diff_lambda[layer],
        "diff_norm_g": diff_norm_g[layer].reshape(1, 2 * HD),
        "s5": s5,
        "wcol": jnp.broadcast_to(nsa_cmp_w[layer][:, :, None], (2, CMP_BLOCK, HD)),
        "cmp_w": nsa_cmp_w[layer],
        "lam_init": 0.8 - 0.6 * math.exp(-0.3 * layer),
    }


def _prompt_layer(x, lw, tab, nb, seq_len, final_g, final, carried):
    tm = min(256, seq_len)
    qa, dr, kvb, za, us, zs, qc, nsa, win, zc, gates, nsa_t, win_t, va_t = _inproj(
        x, lw["norm_g"], lw["w_in"], lw["layer"], tab, seq_len, tm, True, carried)
    oa = _diff_prompt(qa, kvb, va_t, lw["diff_lambda"], lw["diff_norm_g"], nb, seq_len, lw["lam_init"])
    zero = jnp.zeros((nb, GP), F32)
    os_, h_re, h_im = _s5(us, lw["s5"], zero, zero, nb, min(128, seq_len), True)
    oc = _nsa_prompt(qc, nsa, nsa_t, win, win_t, lw["layer"], gates, lw["wcol"], lw["cmp_w"], nb, seq_len)
    y = _merge(x, oa, za, os_, zs, oc, zc, lw["w_out"], final_g, tm, seq_len, True, final)
    return y, (dr, nsa_t, win_t), h_re, h_im


def _pad_rows(a, nb, t):
    a = a.reshape(nb, t, a.shape[-1])
    return jnp.pad(a, ((0, 0), (0, SUBLANES - t), (0, 0)))


def _sample_layer(x, lw, tab, layer, nb, t, page_table, cache_diff, cache_nsa, cache_win, h0_re, h0_im, final_g, final):
    n = nb * t
    qa, dr, kvb, za, us, zs, qc, nsa, win, zc, gates = _inproj(
        x, lw["norm_g"], lw["w_in"], lw["layer"], tab, t, n, False)
    oa8 = _diff_sample(page_table, cache_diff, layer, _pad_rows(qa, nb, t), _pad_rows(kvb, nb, t),
                       lw["diff_lambda"], lw["diff_norm_g"], lw["lam_init"])
    oa = oa8[:, :t].reshape(n, A_WIDTH)
    u_tb = us.reshape(nb, t, S_WIDTH).transpose(1, 0, 2).reshape(n, S_WIDTH)
    os_tb, h_re, h_im = _s5(u_tb, lw["s5"], h0_re, h0_im, nb, t, False)
    os_ = os_tb.reshape(t, nb, S_WIDTH).transpose(1, 0, 2).reshape(n, S_WIDTH)
    oc8, wks_t = _nsa_sample(page_table, cache_nsa, layer, _pad_rows(qc, nb, t), _pad_rows(nsa, nb, t), cache_win,
                             _pad_rows(win, nb, t), _pad_rows(gates, nb, t), lw["cmp_w"], t)
    oc = oc8[:, :t].reshape(n, C_WIDTH)
    y = _merge(x, oa, za, os_, zs, oc, zc, lw["w_out"], final_g, n, t, False, final)
    return y, dr, nsa, wks_t, h_re, h_im


def kernel(x_prompt, x_sample, cache_diff_kv, cache_nsa_kv, cache_win_kv, state_ssm_re, state_ssm_im, page_table, norm_g, w_in, w_out, diff_lambda, diff_norm_g, ssm_A_re, ssm_A_im, ssm_log_dt, ssm_B_re, ssm_B_im, ssm_C_re, ssm_C_im, ssm_D, ssm_w_glu, ssm_b_glu, nsa_cmp_w, final_norm_g):
    nbp, seq_len, _ = x_prompt.shape
    nbs, dec_seq, _ = x_sample.shape
    depth = w_in.shape[0]
    n_pool = cache_diff_kv.shape[1]
    past = page_table.shape[1] * PAGE_SIZE
    w_buf = cache_win_kv.shape[2]
    cache_diff = cache_diff_kv.reshape(depth, n_pool, PAGE_SIZE * 2 * A_HEADS, 2 * HD)
    cache_nsa = cache_nsa_kv.transpose(0, 1, 3, 4, 5, 2).reshape(depth, n_pool, 4 * HD, PAGE_SIZE)
    cache_win = cache_win_kv.transpose(0, 1, 3, 4, 5, 2).reshape(depth, nbs, 2 * HD, w_buf)
    tab_p = _rope_tables(jnp.arange(seq_len, dtype=jnp.int32))
    tab_s = jnp.tile(_rope_tables(past + jnp.arange(dec_seq, dtype=jnp.int32)), (nbs, 1))
    final_g = final_norm_g.reshape(1, D_MODEL)

    xp = x_prompt.reshape(nbp * seq_len, D_MODEL)
    xs = x_sample.reshape(nbs * dec_seq, D_MODEL)
    outs = [[] for _ in range(10)]
    w_keep = min(WINDOW, seq_len)
    w_in_b = _prep_w_in(w_in)
    carried = ()
    for layer in range(depth):
        lw = _layer_weights(layer, norm_g, w_in_b, w_out, diff_lambda, diff_norm_g, ssm_A_re, ssm_A_im, ssm_log_dt,
                            ssm_B_re, ssm_B_im, ssm_C_re, ssm_C_im, ssm_D, ssm_w_glu, ssm_b_glu, nsa_cmp_w)
        final = layer == depth - 1
        xp, carried, h_re, h_im = _prompt_layer(xp, lw, tab_p, nbp, seq_len, final_g, final, carried)
        outs[6].append(h_re.reshape(nbp, S_GROUPS, S_STATE))
        outs[7].append(h_im.reshape(nbp, S_GROUPS, S_STATE))
        xs, dr, nsa, wks_t, h_re, h_im = _sample_layer(
            xs, lw, tab_s, layer, nbs, dec_seq, page_table, cache_diff, cache_nsa, cache_win,
            state_ssm_re[layer].reshape(nbs, GP), state_ssm_im[layer].reshape(nbs, GP), final_g, final)
        outs[1].append(dr.reshape(nbs, dec_seq, 2, A_HEADS, 2 * HD))
        outs[3].append(nsa.reshape(nbs, dec_seq, 4, 1, HD))
        outs[5].append(wks_t)
        outs[8].append(h_re.reshape(nbs, S_GROUPS, S_STATE))
        outs[9].append(h_im.reshape(nbs, S_GROUPS, S_STATE))
    dr_all, nsa_all, win_all = carried
    outs[0] = dr_all.reshape(depth, nbp, seq_len, 2, A_HEADS, 2 * HD)
    outs[2] = nsa_all
    outs[4] = win_all[:, :, :, seq_len - w_keep:]
    stacked = [o if i in (0, 2, 4) else jnp.stack(o) for i, o in enumerate(outs)]

    def rows_first(a, n_rows):
        return a.reshape(a.shape[0], a.shape[1], n_rows, 1, HD, a.shape[3]).transpose(0, 1, 5, 2, 3, 4)

    stacked[2] = rows_first(stacked[2], 4)
    stacked[4] = rows_first(stacked[4], 2)
    stacked[5] = rows_first(stacked[5], 2)
    return (xp.reshape(nbp, seq_len, D_MODEL), xs.reshape(nbs, dec_seq, D_MODEL), *stacked)
```
